```python
import math
import jax, jax.numpy as jnp
from jax import lax
import numpy as np

D_MODEL = 2048
BATCH = 8
SEQ = 2048
DEPTH = 2
DEC_BATCH = 128
DEC_SEQ = 8
PAST_LEN = 2048
PAGE_SIZE = 128

H_A = 8
DH_A = 128
HKV_A = 2
H_IDX = 16
D_IDX = 64
INDEX_TOPK_MAX = 256
H_B = 4
DQK_B = 64
DV_B = 2 * DQK_B
H_MEM = 4
DH_MEM = 128
N_MEM = 256
D_MIX = H_A * DH_A + H_B * DV_B + H_MEM * DH_MEM
D_FF = ((8 * D_MODEL + 3 * 256 - 1) // (3 * 256)) * 256
ROPE_THETA = 500000.0
ROPE_DIV = 4
Q_BLOCK = 128
LN_EPS = 1e-5
RMS_EPS = 1e-5
DEEPNORM_ALPHA = (2.0 * DEPTH) ** 0.25
DEEPNORM_BETA = (8.0 * DEPTH) ** -0.25
IN_SPLIT_SIZES = (H_A * DH_A, HKV_A * DH_A, HKV_A * DH_A, H_B * 2 * DQK_B, H_B * 2 * DQK_B,
                  H_B * DV_B, H_MEM * DH_MEM, H_IDX * D_IDX, D_IDX, H_IDX)
N_IN = sum(IN_SPLIT_SIZES)

kernel_name = 'hybrid_dsa_diffattn_memxattn_step'


def _layer_norm(x, g, b):
    xf = x.astype(jnp.float32)
    mu = jnp.mean(xf, axis=-1, keepdims=True)
    var = jnp.mean(jnp.square(xf - mu), axis=-1, keepdims=True)
    y = (xf - mu) * lax.rsqrt(var + LN_EPS) * g.astype(jnp.float32) + b.astype(jnp.float32)
    return y.astype(x.dtype)


def _rope(x, pos):
    d = x.shape[-1]
    rot = d // ROPE_DIV
    half = rot // 2
    inv_freq = jnp.float32(ROPE_THETA) ** (-jnp.arange(half, dtype=jnp.float32) / half)
    ang = pos.astype(jnp.float32)[:, None] * inv_freq[None, :]
    cos = jnp.cos(ang)[:, None, :]
    sin = jnp.sin(ang)[:, None, :]
    xf = x.astype(jnp.float32)
    x1 = xf[..., :half]
    x2 = xf[..., half:rot]
    out = jnp.concatenate([x1 * cos - x2 * sin, x2 * cos + x1 * sin, xf[..., rot:]], axis=-1)
    return out.astype(x.dtype)


def _project(h, w_in, pos):
    B, T, _ = h.shape
    splits = [int(s) for s in np.cumsum(IN_SPLIT_SIZES)[:-1]]
    z = jnp.einsum('btd,de->bte', h, w_in)
    q_a, k_a, v_a, q_b, k_b, v_b, q_m, q_i, k_i, w_i = jnp.split(z, splits, axis=-1)
    q_a = _rope(q_a.reshape(B, T, H_A, DH_A), pos)
    kv_a = jnp.stack([_rope(k_a.reshape(B, T, HKV_A, DH_A), pos), v_a.reshape(B, T, HKV_A, DH_A)], axis=2)
    q_b = _rope(q_b.reshape(B, T, 2 * H_B, DQK_B), pos).reshape(B, T, H_B, 2, DQK_B)
    k_b = _rope(k_b.reshape(B, T, 2 * H_B, DQK_B), pos).reshape(B, T, H_B, DV_B)
    kv_b = jnp.stack([k_b, v_b.reshape(B, T, H_B, DV_B)], axis=2)
    q_m = q_m.reshape(B, T, H_MEM, DH_MEM)
    q_i = _rope(q_i.reshape(B, T, H_IDX, D_IDX), pos)
    k_i = _rope(k_i.reshape(B, T, 1, D_IDX), pos)[:, :, 0]
    return q_a, kv_a, q_b, kv_b, q_m, q_i, k_i, w_i


def _gather_pages(pool, page_table):
    g = pool[page_table]
    return g.reshape((page_table.shape[0], page_table.shape[1] * PAGE_SIZE) + pool.shape[2:])


def _indexer_scores(q_i, w_i, k_i, q_pos, k_pos):
    s = jax.nn.relu(jnp.einsum('bthd,bsd->bths', q_i, k_i).astype(jnp.float32))
    scores = jnp.einsum('bths,bth->bts', s, w_i.astype(jnp.float32))
    return jnp.where(k_pos[None, None, :] <= q_pos[None, :, None], scores, -jnp.inf)


def _sparse_attend(q_a, kv_sel, valid):
    B, T = q_a.shape[:2]
    q = q_a.reshape(B, T, HKV_A, H_A // HKV_A, DH_A)
    s = jnp.einsum('btgrd,btkgd->btgrk', q, kv_sel[:, :, :, 0]).astype(jnp.float32) * (DH_A ** -0.5)
    s = jnp.where(valid[:, :, None, None, :], s, -jnp.inf)
    p = jax.nn.softmax(s, axis=-1).astype(kv_sel.dtype)
    o = jnp.einsum('btgrk,btkgd->btgrd', p, kv_sel[:, :, :, 1])
    return o.reshape(B, T, H_A * DH_A)


def _dsa_prompt(q_a, kv_a, q_i, w_i, k_i):
    B, S = q_a.shape[:2]
    topk = min(INDEX_TOPK_MAX, S // 4)
    k_pos = jnp.arange(S, dtype=jnp.int32)

    def block(i):
        t0 = i * Q_BLOCK
        qa = lax.dynamic_slice_in_dim(q_a, t0, Q_BLOCK, axis=1)
        qi = lax.dynamic_slice_in_dim(q_i, t0, Q_BLOCK, axis=1)
        wi = lax.dynamic_slice_in_dim(w_i, t0, Q_BLOCK, axis=1)
        q_pos = t0 + jnp.arange(Q_BLOCK, dtype=jnp.int32)
        _, idx = lax.top_k(_indexer_scores(qi, wi, k_i, q_pos, k_pos), topk)
        kv_sel = jax.vmap(lambda kv, ix: kv[ix])(kv_a, idx)
        return _sparse_attend(qa, kv_sel, idx <= q_pos[None, :, None])

    out = lax.map(block, jnp.arange(S // Q_BLOCK, dtype=jnp.int32))
    return jnp.moveaxis(out, 0, 1).reshape(B, S, H_A * DH_A)


def _dsa_sample(q_a, kv_a_new, q_i, w_i, k_i_new, pool_kv_a, pool_idx_k, page_table, q_pos):
    B, T = q_a.shape[:2]
    n_pages = page_table.shape[1]
    past = n_pages * PAGE_SIZE
    n_keys = past + T
    topk = min(INDEX_TOPK_MAX, n_keys // 4)
    k_i_all = jnp.concatenate([_gather_pages(pool_idx_k, page_table), k_i_new], axis=1)
    scores = _indexer_scores(q_i, w_i, k_i_all, q_pos, jnp.arange(n_keys, dtype=jnp.int32))
    _, idx = lax.top_k(scores, topk)
    page = jnp.minimum(idx // PAGE_SIZE, n_pages - 1)
    phys_page = jax.vmap(lambda pt, pg: pt[pg])(page_table, page)
    rows = phys_page * PAGE_SIZE + idx % PAGE_SIZE
    flat = pool_kv_a.reshape((-1,) + pool_kv_a.shape[2:])
    past_sel = flat[rows]
    new_sel = jax.vmap(lambda kv, ix: kv[ix])(kv_a_new, jnp.clip(idx - past, 0, T - 1))
    kv_sel = jnp.where((idx < past)[..., None, None, None], past_sel, new_sel)
    return _sparse_attend(q_a, kv_sel, idx <= q_pos[None, :, None])


def _split_kv_b(kv_b):
    B, S = kv_b.shape[:2]
    return kv_b[:, :, 0].reshape(B, S, H_B, 2, DQK_B), kv_b[:, :, 1]


def _diff_attend(q_b, k_b, v_b, lam, q_pos, k_pos):
    s = jnp.einsum('bthcd,bshcd->bhcts', q_b, k_b).astype(jnp.float32) * (DQK_B ** -0.5)
    s = jnp.where(k_pos[None, :] <= q_pos[:, None], s, -jnp.inf)
    p = jax.nn.softmax(s, axis=-1)
    a = p[:, :, 0] - lam * p[:, :, 1]
    return jnp.einsum('bhts,bshd->bthd', a.astype(v_b.dtype), v_b)


def _diff_prompt(q_b, kv_b, lam, pos):
    B, S = q_b.shape[:2]
    k_b, v_b = _split_kv_b(kv_b)

    def block(i):
        t0 = i * Q_BLOCK
        qb = lax.dynamic_slice_in_dim(q_b, t0, Q_BLOCK, axis=1)
        return _diff_attend(qb, k_b, v_b, lam, t0 + jnp.arange(Q_BLOCK, dtype=jnp.int32), pos)

    out = lax.map(block, jnp.arange(S // Q_BLOCK, dtype=jnp.int32))
    return jnp.moveaxis(out, 0, 1).reshape(B, S, H_B, DV_B)


def _diff_out(o, g, lam_init):
    B, T = o.shape[:2]
    of = o.astype(jnp.float32)
    of = of * lax.rsqrt(jnp.mean(jnp.square(of), axis=-1, keepdims=True) + RMS_EPS)
    of = of * g.astype(jnp.float32) * (1.0 - lam_init)
    return of.astype(o.dtype).reshape(B, T, H_B * DV_B)


def _mem_kv(mem, w_mem_kv):
    B, M, _ = mem.shape
    return jnp.einsum('bmd,de->bme', mem, w_mem_kv).reshape(B, M, 2, H_MEM, DH_MEM)


def _mem_attend(q_m, mem_kv):
    B, T = q_m.shape[:2]
    s = jnp.einsum('bthd,bmhd->bhtm', q_m, mem_kv[:, :, 0]).astype(jnp.float32) * (DH_MEM ** -0.5)
    p = jax.nn.softmax(s, axis=-1).astype(mem_kv.dtype)
    o = jnp.einsum('bhtm,bmhd->bthd', p, mem_kv[:, :, 1])
    return o.reshape(B, T, H_MEM * DH_MEM)


def _finish_layer(x, mixed, w_o, ln1_g, ln1_b, w_gate, w_up, w_down, ln2_g, ln2_b):
    a = jnp.einsum('bte,ed->btd', mixed, w_o)
    h = _layer_norm(DEEPNORM_ALPHA * x + a, ln1_g, ln1_b)
    f = jnp.einsum('btf,fd->btd', jax.nn.silu(jnp.einsum('btd,df->btf', h, w_gate)) * jnp.einsum('btd,df->btf', h, w_up), w_down)
    return _layer_norm(DEEPNORM_ALPHA * h + f, ln2_g, ln2_b)


def setup_inputs(seed: int = 0) -> dict:
    key = jax.random.key(seed)
    ks = jax.random.split(key, 24)
    f32 = jnp.float32
    n_pages = PAST_LEN // PAGE_SIZE
    n_used = DEC_BATCH * n_pages
    n_phys = n_used + n_used // 4
    perm = jax.random.permutation(ks[0], n_phys)
    page_table = perm[:n_used].reshape(DEC_BATCH, n_pages).astype(jnp.int32)
    nrm = lambda k, shape, scale=1.0: jax.random.normal(k, shape, f32) * scale
    return {
        'x_prompt': nrm(ks[1], (BATCH, SEQ, D_MODEL)),
        'x_sample': nrm(ks[2], (DEC_BATCH, DEC_SEQ, D_MODEL)),
        'mem_prompt': nrm(ks[3], (BATCH, N_MEM, D_MODEL)),
        'cache_kv_a': nrm(ks[4], (DEPTH, n_phys, PAGE_SIZE, 2, HKV_A, DH_A)),
        'cache_idx_k': nrm(ks[5], (DEPTH, n_phys, PAGE_SIZE, D_IDX)),
        'cache_kv_b': nrm(ks[6], (DEPTH, n_phys, PAGE_SIZE, 2, H_B, DV_B)),
        'cache_mem_kv': nrm(ks[7], (DEPTH, DEC_BATCH, N_MEM, 2, H_MEM, DH_MEM)),
        'page_table': page_table,
        'w_in': nrm(ks[8], (DEPTH, D_MODEL, N_IN), D_MODEL ** -0.5),
        'w_mem_kv': nrm(ks[9], (DEPTH, D_MODEL, 2 * H_MEM * DH_MEM), D_MODEL ** -0.5),
        'lambda_q1': nrm(ks[10], (DEPTH, DQK_B), 0.1),
        'lambda_k1': nrm(ks[11], (DEPTH, DQK_B), 0.1),
        'lambda_q2': nrm(ks[12], (DEPTH, DQK_B), 0.1),
        'lambda_k2': nrm(ks[13], (DEPTH, DQK_B), 0.1),
        'subln_g': 1.0 + nrm(ks[14], (DEPTH, DV_B), 0.02),
        'w_o': nrm(ks[15], (DEPTH, D_MIX, D_MODEL), DEEPNORM_BETA * D_MIX ** -0.5),
        'ln1_g': 1.0 + nrm(ks[16], (DEPTH, D_MODEL), 0.02),
        'ln1_b': nrm(ks[17], (DEPTH, D_MODEL), 0.02),
        'w_gate': nrm(ks[18], (DEPTH, D_MODEL, D_FF), D_MODEL ** -0.5),
        'w_up': nrm(ks[19], (DEPTH, D_MODEL, D_FF), D_MODEL ** -0.5),
        'w_down': nrm(ks[20], (DEPTH, D_FF, D_MODEL), DEEPNORM_BETA * D_FF ** -0.5),
        'ln2_g': 1.0 + nrm(ks[21], (DEPTH, D_MODEL), 0.02),
        'ln2_b': nrm(ks[22], (DEPTH, D_MODEL), 0.02),
    }


def reference(x_prompt, x_sample, mem_prompt, cache_kv_a, cache_idx_k, cache_kv_b, cache_mem_kv, page_table,
              w_in, w_mem_kv, lambda_q1, lambda_k1, lambda_q2, lambda_k2, subln_g, w_o,
              ln1_g, ln1_b, w_gate, w_up, w_down, ln2_g, ln2_b):
    f32 = jnp.float32
    pos_p = jnp.arange(x_prompt.shape[1], dtype=jnp.int32)
    pos_s = PAST_LEN + jnp.arange(x_sample.shape[1], dtype=jnp.int32)
    n_keys_s = PAST_LEN + x_sample.shape[1]
    xp, xs = x_prompt, x_sample
    kv_a_p, idx_k_p, kv_b_p, mem_kv_p = [], [], [], []
    kv_a_s, idx_k_s, kv_b_s = [], [], []
    for l in range(DEPTH):
        lam_init = 0.8 - 0.6 * math.exp(-0.3 * l)
        lam = (jnp.exp(jnp.sum(lambda_q1[l].astype(f32) * lambda_k1[l].astype(f32)))
               - jnp.exp(jnp.sum(lambda_q2[l].astype(f32) * lambda_k2[l].astype(f32))) + lam_init)
        post = (w_o[l], ln1_g[l], ln1_b[l], w_gate[l], w_up[l], w_down[l], ln2_g[l], ln2_b[l])

        q_a, kv_a, q_b, kv_b, q_m, q_i, k_i, w_i = _project(xp, w_in[l], pos_p)
        mem_kv = _mem_kv(mem_prompt, w_mem_kv[l])
        mixed = jnp.concatenate([
            _dsa_prompt(q_a, kv_a, q_i, w_i, k_i),
            _diff_out(_diff_prompt(q_b, kv_b, lam, pos_p), subln_g[l], lam_init),
            _mem_attend(q_m, mem_kv)], axis=-1)
        xp = _finish_layer(xp, mixed, *post)
        kv_a_p.append(kv_a)
        idx_k_p.append(k_i)
        kv_b_p.append(kv_b)
        mem_kv_p.append(mem_kv)

        q_a, kv_a, q_b, kv_b, q_m, q_i, k_i, w_i = _project(xs, w_in[l], pos_s)
        o_a = _dsa_sample(q_a, kv_a, q_i, w_i, k_i, cache_kv_a[l], cache_idx_k[l], page_table, pos_s)
        kv_b_all = jnp.concatenate([_gather_pages(cache_kv_b[l], page_table), kv_b], axis=1)
        k_b_all, v_b_all = _split_kv_b(kv_b_all)
        o_b = _diff_attend(q_b, k_b_all, v_b_all, lam, pos_s, jnp.arange(n_keys_s, dtype=jnp.int32))
        mixed = jnp.concatenate([o_a, _diff_out(o_b, subln_g[l], lam_init), _mem_attend(q_m, cache_mem_kv[l])], axis=-1)
        xs = _finish_layer(xs, mixed, *post)
        kv_a_s.append(kv_a)
        idx_k_s.append(k_i)
        kv_b_s.append(kv_b)

    return (xp, xs, jnp.stack(kv_a_p), jnp.stack(idx_k_p), jnp.stack(kv_b_p), jnp.stack(mem_kv_p),
            jnp.stack(kv_a_s), jnp.stack(idx_k_s), jnp.stack(kv_b_s))
```

```python
import functools
import math

import jax
import jax.numpy as jnp
from jax import lax
from jax.experimental import pallas as pl
from jax.experimental.pallas import tpu as pltpu

F32 = jnp.float32
I32 = jnp.int32
MXU_DTYPE = jnp.bfloat16

H_A, DH_A, HKV_A = 8, 128, 2
H_IDX, D_IDX = 16, 64
INDEX_TOPK_MAX = 256
H_B, DQK_B, DV_B = 4, 64, 128
H_MEM, DH_MEM = 4, 128
PAGE_SIZE = 128
ROPE_THETA = 500000.0
ROPE_DIV = 4
LN_EPS = 1e-5
RMS_EPS = 1e-5

LANES = 128
VMEM_LIMIT_BYTES = 56 * 1024 * 1024

W_QA, W_KVA, W_QB, W_KVB, W_QI, W_QM = 1024, 512, 512, 1024, 1024, 512
C_QA, C_KVA, C_QB, C_KVB, C_QI, C_QM, C_TAIL = 0, 8, 12, 16, 24, 32, 36
N_IN_PAD = 37 * LANES

NEG_INF = float("-inf")
INT_MIN = -(2 ** 31)
NEG_INF_KEY = -2139095041


def _cparams(sem):
    return pltpu.CompilerParams(dimension_semantics=sem, vmem_limit_bytes=VMEM_LIMIT_BYTES)


def _mm(a, b):
    return jnp.dot(a, b, preferred_element_type=F32)


def _mm_nt(a, b):
    return lax.dot_general(a, b, (((1,), (1,)), ((), ())), preferred_element_type=F32)


def _layer_norm(y, g, b):
    mu = jnp.mean(y, axis=-1, keepdims=True)
    d = y - mu
    var = jnp.mean(d * d, axis=-1, keepdims=True)
    return d * lax.rsqrt(var + LN_EPS) * g + b


def _exp_and_sum(s):
    m = jnp.max(s, axis=-1, keepdims=True)
    e = jnp.exp(s - m)
    return e, jnp.sum(e, axis=-1, keepdims=True)


def _ordered_key(x):
    k = pltpu.bitcast(x, I32)
    return jnp.where(k < 0, k ^ jnp.int32(0x7FFFFFFF), k)


def _rope_tables(pos, head_dim):
    rot = head_dim // ROPE_DIV
    half = rot // 2
    t = pos.shape[0]
    inv_freq = jnp.float32(ROPE_THETA) ** (-jnp.arange(half, dtype=F32) / half)
    ang = pos.astype(F32)[:, None] * inv_freq[None, :]
    cos, sin = jnp.cos(ang), jnp.sin(ang)
    c = jnp.concatenate([cos, cos, jnp.ones((t, head_dim - rot), F32)], axis=-1)
    s_hi = jnp.concatenate([-sin, jnp.zeros((t, head_dim - half), F32)], axis=-1)
    s_lo = jnp.concatenate([jnp.zeros((t, half), F32), sin, jnp.zeros((t, head_dim - rot), F32)], axis=-1)
    rep = LANES // head_dim
    return tuple(jnp.tile(a, (1, rep)) for a in (c, s_hi, s_lo))


def _proj_kernel(x_ref, w_ref, ca, sa1, sa2, cb, sb1, sb2,
                 qa_o, kva_o, qb_o, kvb_o, qi_o, qm_o, ki_o, wi_o):
    xb = x_ref[...].astype(MXU_DTYPE)

    def rope(z, c, s_hi, s_lo, half):
        return z * c[...] + pltpu.roll(z, LANES - half, 1) * s_hi[...] + pltpu.roll(z, half, 1) * s_lo[...]

    def rope_a(z):
        return rope(z, ca, sa1, sa2, DH_A // ROPE_DIV // 2)

    def rope_b(z):
        return rope(z, cb, sb1, sb2, DQK_B // ROPE_DIV // 2)

    tm = x_ref.shape[0]

    def emit(out_ref, c0, kinds, interleave=False):
        n = len(kinds)
        z = _mm(xb, w_ref[:, c0 * LANES:(c0 + n) * LANES])
        for j, kind in enumerate(kinds):
            zj = z[:, j * LANES:(j + 1) * LANES]
            if kind == "a":
                zj = rope_a(zj)
            elif kind == "b":
                zj = rope_b(zj)
            if interleave:
                out_ref[pl.ds(j, tm, stride=n), :] = zj
            else:
                out_ref[:, j * LANES:(j + 1) * LANES] = zj

    emit(qa_o, C_QA, "a" * 8)
    emit(kva_o, C_KVA, "aa--", interleave=True)
    emit(qb_o, C_QB, "bbbb")
    emit(kvb_o, C_KVB, "bbbb----", interleave=True)
    emit(qi_o, C_QI, "b" * 8)
    emit(qm_o, C_QM, "----")
    z = _mm(xb, w_ref[:, C_TAIL * LANES:(C_TAIL + 1) * LANES])
    lane = lax.broadcasted_iota(I32, z.shape, 1)
    zt = jnp.where(lane < D_IDX, rope_b(z), z)
    ki_o[...] = zt[:, :D_IDX]
    wi_o[...] = zt[:, D_IDX:D_IDX + H_IDX]


def _project(x2d, w_in_p, layer, tabs, n_tab_blocks, tm):
    m, d = x2d.shape
    shapes = ((1, W_QA), (W_KVA // LANES, LANES), (1, W_QB), (W_KVB // LANES, LANES), (1, W_QI), (1, W_QM),
              (1, D_IDX), (1, H_IDX))
    tab_spec = pl.BlockSpec((tm, LANES), lambda i: (i % n_tab_blocks, 0))
    return pl.pallas_call(
        _proj_kernel,
        grid=(m // tm,),
        in_specs=[pl.BlockSpec((tm, d), lambda i: (i, 0)),
                  pl.BlockSpec((None, d, N_IN_PAD), lambda i: (layer, 0, 0), pipeline_mode=pl.Buffered(1))]
                 + [tab_spec] * 6,
        out_specs=[pl.BlockSpec((tm * r, w), lambda i: (i, 0)) for r, w in shapes],
        out_shape=[jax.ShapeDtypeStruct((m * r, w), F32) for r, w in shapes],
        compiler_params=_cparams(("parallel",)),
        name="proj_rope",
    )(x2d, w_in_p, *tabs)


def _matmul_kernel(x_ref, w_ref, o_ref):
    tm = x_ref.shape[0]
    n = w_ref.shape[1] // LANES
    z = _mm(x_ref[...].astype(MXU_DTYPE), w_ref[...])
    for j in range(n):
        o_ref[pl.ds(j, tm, stride=n), :] = z[:, j * LANES:(j + 1) * LANES]


def _matmul(x2d, w, layer, tm):
    m, d = x2d.shape
    n = w.shape[-1]
    return pl.pallas_call(
        _matmul_kernel,
        grid=(m // tm,),
        in_specs=[pl.BlockSpec((tm, d), lambda i: (i, 0)),
                  pl.BlockSpec((None, d, n), lambda i: (layer, 0, 0), pipeline_mode=pl.Buffered(1))],
        out_specs=pl.BlockSpec((tm * n // LANES, LANES), lambda i: (i, 0)),
        out_shape=jax.ShapeDtypeStruct((m * n // LANES, LANES), F32),
        compiler_params=_cparams(("parallel",)),
        name="mem_kv_proj",
    )(x2d, w)


def _lambda_value(lq1, lk1, lq2, lk2, lam_init):
    a = jnp.sum(lq1[...] * lk1[...], axis=-1, keepdims=True)
    b = jnp.sum(lq2[...] * lk2[...], axis=-1, keepdims=True)
    return jnp.exp(a) - jnp.exp(b) + lam_init


def _diff_finish(o, g_ref, lam_init):
    o = o * lax.rsqrt(jnp.mean(o * o, axis=-1, keepdims=True) + RMS_EPS)
    return o * g_ref[...] * (1.0 - lam_init)


def _kth_threshold(key_ref, rows, n_chunks, topk):
    ones = jnp.ones((LANES, LANES), MXU_DTYPE)

    def count_ge(cand):
        acc = jnp.zeros((rows, LANES), F32)
        for c in range(n_chunks):
            acc = acc + jnp.where(key_ref[:, c * LANES:(c + 1) * LANES] >= cand, 1.0, 0.0)
        return _mm(acc.astype(MXU_DTYPE), ones)

    def body(it, t):
        cand = t + lax.shift_left(jnp.int32(1), jnp.int32(31) - it)
        return jnp.where(count_ge(cand) >= topk, cand, t)

    t = lax.fori_loop(0, 32, body, jnp.full((rows, LANES), INT_MIN, I32))
    return t, count_ge(t)


def _selection_bias(key_ref, bias_ref, t, n_ge, rows, n_chunks, topk):
    floor_t = jnp.maximum(t, jnp.int32(NEG_INF_KEY + 1))
    tie_rows = jnp.where((n_ge > topk) & (t > NEG_INF_KEY), 1.0, 0.0)
    has_tie = jnp.max(tie_rows) > 0.5

    @pl.when(jnp.logical_not(has_tie))
    def _():
        for c in range(n_chunks):
            sl = slice(c * LANES, (c + 1) * LANES)
            bias_ref[:, sl] = jnp.where(key_ref[:, sl] >= floor_t, 0.0, NEG_INF)

    @pl.when(has_tie)
    def _():
        ones = jnp.ones((LANES, LANES), MXU_DTYPE)
        r_i = lax.broadcasted_iota(I32, (LANES, LANES), 0)
        c_i = lax.broadcasted_iota(I32, (LANES, LANES), 1)
        strict_upper = jnp.where(r_i < c_i, 1.0, 0.0).astype(MXU_DTYPE)
        n_gt = jnp.zeros((rows, LANES), F32)
        for c in range(n_chunks):
            sl = slice(c * LANES, (c + 1) * LANES)
            n_gt = n_gt + _mm(jnp.where(key_ref[:, sl] > t, 1.0, 0.0).astype(MXU_DTYPE), ones)
        need = topk - n_gt
        run = jnp.zeros((rows, LANES), F32)
        for c in range(n_chunks):
            sl = slice(c * LANES, (c + 1) * LANES)
            k = key_ref[:, sl]
            eq = jnp.where(k == t, 1.0, 0.0).astype(MXU_DTYPE)
            before = run + _mm(eq, strict_upper)
            run = run + _mm(eq, ones)
            keep = (k > t) | ((k == t) & (before < need))
            bias_ref[:, sl] = jnp.where(keep & (k > NEG_INF_KEY), 0.0, NEG_INF)


Q_BLOCK = 128


def _half_lanes(x, c):
    lane = lax.broadcasted_iota(I32, x.shape, 1)
    return jnp.where((lane >= c * DQK_B) & (lane < (c + 1) * DQK_B), x, 0.0)


def _prompt_attn_kernel(qa_ref, kva_ref, qi_ref, wi_ref, ki_ref, qb_ref, kvb_ref, qm_ref, mem_ref,
                        lq1, lk1, lq2, lk2, g_ref, o_ref,
                        ka_s, va_s, kb_s, vb_s, ki_s, km_s, vm_s, key_ref, bias_ref, *, topk, lam_init):
    seq = ki_ref.shape[0]
    n_mem = km_s.shape[1]
    n_chunks = seq // LANES
    n_kva, n_kvb, n_kvm = 2 * HKV_A, 2 * H_B, 2 * H_MEM
    qi = pl.program_id(1)

    @pl.when(qi == 0)
    def _():
        for g in range(HKV_A):
            ka_s[g] = kva_ref[pl.ds(g, seq, stride=n_kva), :].astype(MXU_DTYPE)
            va_s[g] = kva_ref[pl.ds(HKV_A + g, seq, stride=n_kva), :].astype(MXU_DTYPE)
        for h in range(H_B):
            kb_s[h] = kvb_ref[pl.ds(h, seq, stride=n_kvb), :].astype(MXU_DTYPE)
            vb_s[h] = kvb_ref[pl.ds(H_B + h, seq, stride=n_kvb), :].astype(MXU_DTYPE)
        for h in range(H_MEM):
            km_s[h] = mem_ref[pl.ds(h, n_mem, stride=n_kvm), :].astype(MXU_DTYPE)
            vm_s[h] = mem_ref[pl.ds(H_MEM + h, n_mem, stride=n_kvm), :].astype(MXU_DTYPE)
        ki_s[...] = ki_ref[...].astype(MXU_DTYPE)

    q_pos = qi * Q_BLOCK + lax.broadcasted_iota(I32, (Q_BLOCK, seq), 0)
    k_pos = lax.broadcasted_iota(I32, (Q_BLOCK, seq), 1)
    causal = k_pos <= q_pos

    ki = ki_s[...]
    sc = jnp.zeros((Q_BLOCK, seq), F32)
    for h in range(H_IDX):
        qh = qi_ref[:, h * D_IDX:(h + 1) * D_IDX].astype(MXU_DTYPE)
        sc = sc + wi_ref[:, h:h + 1] * jnp.maximum(_mm_nt(qh, ki), 0.0)
    key_ref[...] = _ordered_key(jnp.where(causal, sc, NEG_INF))
    t, n_ge = _kth_threshold(key_ref, Q_BLOCK, n_chunks, topk)
    _selection_bias(key_ref, bias_ref, t, n_ge, Q_BLOCK, n_chunks, topk)

    scale_a = DH_A ** -0.5
    rep = H_A // HKV_A
    for h in range(H_A):
        g = h // rep
        q = qa_ref[:, h * DH_A:(h + 1) * DH_A].astype(MXU_DTYPE)
        e, l = _exp_and_sum(_mm_nt(q, ka_s[g]) * scale_a + bias_ref[...])
        o_ref[:, h * DH_A:(h + 1) * DH_A] = _mm(e.astype(MXU_DTYPE), va_s[g]) / l

    lam = _lambda_value(lq1, lk1, lq2, lk2, lam_init)
    scale_b = DQK_B ** -0.5
    causal_bias = jnp.where(causal, 0.0, NEG_INF)
    off_b = H_A * DH_A
    for h in range(H_B):
        qh = qb_ref[:, h * DV_B:(h + 1) * DV_B]
        parts = []
        for c in range(2):
            q = _half_lanes(qh, c).astype(MXU_DTYPE)
            e, l = _exp_and_sum(_mm_nt(q, kb_s[h]) * scale_b + causal_bias)
            parts.append(_mm(e.astype(MXU_DTYPE), vb_s[h]) / l)
        o = parts[0] - lam * parts[1]
        o_ref[:, off_b + h * DV_B:off_b + (h + 1) * DV_B] = _diff_finish(o, g_ref, lam_init)

    scale_m = DH_MEM ** -0.5
    off_m = off_b + H_B * DV_B
    for h in range(H_MEM):
        q = qm_ref[:, h * DH_MEM:(h + 1) * DH_MEM].astype(MXU_DTYPE)
        e, l = _exp_and_sum(_mm_nt(q, km_s[h]) * scale_m)
        o_ref[:, off_m + h * DH_MEM:off_m + (h + 1) * DH_MEM] = _mm(e.astype(MXU_DTYPE), vm_s[h]) / l


def _prompt_attention(qa, kva, qi, wi, ki, qb, kvb, qm, mem_kv, lam_params, g, batch, seq, layer, lam_init):
    n_q = seq // Q_BLOCK
    n_kva, n_kvb, n_kvm = 2 * HKV_A, 2 * H_B, 2 * H_MEM
    n_mem = mem_kv.shape[0] // (batch * n_kvm)
    topk = min(INDEX_TOPK_MAX, seq // 4)
    d_mix = H_A * DH_A + H_B * DV_B + H_MEM * DH_MEM

    def q_spec(w):
        return pl.BlockSpec((Q_BLOCK, w), lambda b, i: (b * n_q + i, 0))

    def kv_spec(rows, w):
        return pl.BlockSpec((rows, w), lambda b, i: (b, 0))

    vec_spec = pl.BlockSpec((None, 1, DQK_B), lambda b, i: (layer, 0, 0))
    kernel = functools.partial(_prompt_attn_kernel, topk=topk, lam_init=lam_init)
    return pl.pallas_call(
        kernel,
        grid=(batch, n_q),
        in_specs=[q_spec(W_QA), kv_spec(seq * n_kva, LANES), q_spec(W_QI), q_spec(H_IDX), kv_spec(seq, D_IDX),
                  q_spec(W_QB), kv_spec(seq * n_kvb, LANES), q_spec(W_QM), kv_spec(n_mem * n_kvm, LANES),
                  vec_spec, vec_spec, vec_spec, vec_spec,
                  pl.BlockSpec((None, 1, DV_B), lambda b, i: (layer, 0, 0))],
        out_specs=pl.BlockSpec((Q_BLOCK, d_mix), lambda b, i: (b * n_q + i, 0)),
        out_shape=jax.ShapeDtypeStruct((batch * seq, d_mix), F32),
        scratch_shapes=[pltpu.VMEM((HKV_A, seq, DH_A), MXU_DTYPE), pltpu.VMEM((HKV_A, seq, DH_A), MXU_DTYPE),
                        pltpu.VMEM((H_B, seq, DV_B), MXU_DTYPE), pltpu.VMEM((H_B, seq, DV_B), MXU_DTYPE),
                        pltpu.VMEM((seq, D_IDX), MXU_DTYPE),
                        pltpu.VMEM((H_MEM, n_mem, DH_MEM), MXU_DTYPE), pltpu.VMEM((H_MEM, n_mem, DH_MEM), MXU_DTYPE),
                        pltpu.VMEM((Q_BLOCK, seq), I32), pltpu.VMEM((Q_BLOCK, seq), F32)],
        compiler_params=_cparams(("parallel", "arbitrary")),
        name="prompt_attention",
    )(qa, kva, qi, wi, ki, qb, kvb, qm, mem_kv, *lam_params, g)


def _pad_rows(x, rows):
    return jnp.concatenate([x, jnp.zeros((rows - x.shape[0], x.shape[1]), x.dtype)], axis=0)


def _sample_attn_kernel(pt_ref, qa_ref, kvan_ref, qi_ref, wcol_ref, kin_ref, qb_ref, kvbn_ref, qm_ref, mem_ref,
                        lq1, lk1, lq2, lk2, g_ref, *rest, n_pages, topk, lam_init):
    del pt_ref
    idx_pages = rest[:n_pages]
    kva_pages = rest[n_pages:2 * n_pages]
    kvb_pages = rest[2 * n_pages:3 * n_pages]
    o_ref, key_ref, bias_ref, sa_ref, sb_ref = rest[3 * n_pages:]
    t_new = qa_ref.shape[0]
    past = n_pages * PAGE_SIZE
    n_chunks = n_pages + 1
    width = n_chunks * LANES

    lane = lax.broadcasted_iota(I32, (t_new, LANES), 1)
    tok = lax.broadcasted_iota(I32, (t_new, LANES), 0)
    new_ok = lane <= tok

    q_ht = jnp.concatenate([qi_ref[:, h * D_IDX:(h + 1) * D_IDX] for h in range(H_IDX)], axis=0).astype(MXU_DTYPE)
    wcol = wcol_ref[...]

    def index_scores(keys):
        s = jnp.maximum(_mm_nt(q_ht, keys.astype(MXU_DTYPE)), 0.0) * wcol
        acc = s[0:t_new]
        for h in range(1, H_IDX):
            acc = acc + s[h * t_new:(h + 1) * t_new]
        return acc

    for j in range(n_pages):
        key_ref[:, j * LANES:(j + 1) * LANES] = _ordered_key(index_scores(idx_pages[j][...]))
    s_new = index_scores(_pad_rows(kin_ref[...], LANES))
    key_ref[:, past:past + LANES] = _ordered_key(jnp.where(new_ok, s_new, NEG_INF))
    t, n_ge = _kth_threshold(key_ref, t_new, n_chunks, topk)
    _selection_bias(key_ref, bias_ref, t, n_ge, t_new, n_chunks, topk)

    def chunk_rows(ref, j, rows, n):
        return ref[pl.ds(j, rows, stride=n), :].astype(MXU_DTYPE)

    def new_rows(ref, j, n):
        return _pad_rows(ref[pl.ds(j, t_new, stride=n), :], LANES).astype(MXU_DTYPE)

    scale_a = DH_A ** -0.5
    rep = H_A // HKV_A
    n_kva = 2 * HKV_A
    bias_a = jnp.concatenate([bias_ref[...]] * rep, axis=0)
    for g in range(HKV_A):
        q = jnp.concatenate([qa_ref[:, (g * rep + r) * DH_A:(g * rep + r + 1) * DH_A] for r in range(rep)],
                            axis=0).astype(MXU_DTYPE)
        for j in range(n_pages):
            sa_ref[:, j * LANES:(j + 1) * LANES] = _mm_nt(q, chunk_rows(kva_pages[j], g, PAGE_SIZE, n_kva))
        sa_ref[:, past:past + LANES] = _mm_nt(q, new_rows(kvan_ref, g, n_kva))
        e, l = _exp_and_sum(sa_ref[...] * scale_a + bias_a)
        e = e.astype(MXU_DTYPE)
        acc = _mm(e[:, past:past + LANES], new_rows(kvan_ref, HKV_A + g, n_kva))
        for j in range(n_pages):
            acc = acc + _mm(e[:, j * LANES:(j + 1) * LANES], chunk_rows(kva_pages[j], HKV_A + g, PAGE_SIZE, n_kva))
        acc = acc / l
        for r in range(rep):
            h = g * rep + r
            o_ref[:, h * DH_A:(h + 1) * DH_A] = acc[r * t_new:(r + 1) * t_new]

    lam = _lambda_value(lq1, lk1, lq2, lk2, lam_init)
    scale_b = DQK_B ** -0.5
    n_kvb = 2 * H_B
    for h in range(H_B):
        qh = qb_ref[:, h * DV_B:(h + 1) * DV_B]
        q = jnp.concatenate([_half_lanes(qh, 0), _half_lanes(qh, 1)], axis=0).astype(MXU_DTYPE)
        rows = slice(2 * h * t_new, (2 * h + 2) * t_new)
        for j in range(n_pages):
            sb_ref[rows, j * LANES:(j + 1) * LANES] = _mm_nt(q, chunk_rows(kvb_pages[j], h, PAGE_SIZE, n_kvb))
        sb_ref[rows, past:past + LANES] = _mm_nt(q, new_rows(kvbn_ref, h, n_kvb))
    new_bias = jnp.where(new_ok, 0.0, NEG_INF)
    bias_b = jnp.concatenate(
        [jnp.zeros((n_kvb * t_new, past), F32), jnp.concatenate([new_bias] * n_kvb, axis=0)], axis=1)
    e, l = _exp_and_sum(sb_ref[...] * scale_b + bias_b)
    e = e.astype(MXU_DTYPE)
    off_b = H_A * DH_A
    for h in range(H_B):
        rows = slice(2 * h * t_new, (2 * h + 2) * t_new)
        acc = _mm(e[rows, past:past + LANES], new_rows(kvbn_ref, H_B + h, n_kvb))
        for j in range(n_pages):
            acc = acc + _mm(e[rows, j * LANES:(j + 1) * LANES], chunk_rows(kvb_pages[j], H_B + h, PAGE_SIZE, n_kvb))
        acc = acc / l[rows]
        o = acc[0:t_new] - lam * acc[t_new:2 * t_new]
        o_ref[:, off_b + h * DV_B:off_b + (h + 1) * DV_B] = _diff_finish(o, g_ref, lam_init)

    scale_m = DH_MEM ** -0.5
    off_m = off_b + H_B * DV_B
    n_kvm = 2 * H_MEM
    n_mem = mem_ref.shape[0] // n_kvm
    for h in range(H_MEM):
        q = qm_ref[:, h * DH_MEM:(h + 1) * DH_MEM].astype(MXU_DTYPE)
        e, l = _exp_and_sum(_mm_nt(q, chunk_rows(mem_ref, h, n_mem, n_kvm)) * scale_m)
        o_ref[:, off_m + h * DH_MEM:off_m + (h + 1) * DH_MEM] = (
            _mm(e.astype(MXU_DTYPE), chunk_rows(mem_ref, H_MEM + h, n_mem, n_kvm)) / l)


def _sample_attention(page_table, qa, kva_new, qi, wcol, ki_new, qb, kvb_new, qm, pool_idx, pool_kva, pool_kvb,
                      pool_mem, lam_params, g, batch, t_new, layer, lam_init):
    n_pages = page_table.shape[1]
    n_kva, n_kvb, n_kvm = 2 * HKV_A, 2 * H_B, 2 * H_MEM
    n_phys = pool_idx.shape[0] // lam_params[0].shape[0]
    mem_rows = pool_mem.shape[0] // (lam_params[0].shape[0] * batch)
    topk = min(INDEX_TOPK_MAX, (n_pages * PAGE_SIZE + t_new) // 4)
    d_mix = H_A * DH_A + H_B * DV_B + H_MEM * DH_MEM
    width = (n_pages + 1) * LANES

    def row_spec(rows, w):
        return pl.BlockSpec((rows, w), lambda b, pt: (b, 0))

    def idx_page_spec(j):
        return pl.BlockSpec((None, PAGE_SIZE, D_IDX), lambda b, pt: (layer * n_phys + pt[b, j], 0, 0))

    def page_spec(j, n):
        return pl.BlockSpec((PAGE_SIZE * n, LANES), lambda b, pt: (layer * n_phys + pt[b, j], 0))

    vec_spec = pl.BlockSpec((None, 1, DQK_B), lambda b, pt: (layer, 0, 0))
    in_specs = ([row_spec(t_new, W_QA), row_spec(t_new * n_kva, LANES), row_spec(t_new, W_QI),
                 row_spec(H_IDX * t_new, 1), row_spec(t_new, D_IDX), row_spec(t_new, W_QB),
                 row_spec(t_new * n_kvb, LANES), row_spec(t_new, W_QM),
                 pl.BlockSpec((mem_rows, LANES), lambda b, pt: (layer * batch + b, 0)),
                 vec_spec, vec_spec, vec_spec, vec_spec,
                 pl.BlockSpec((None, 1, DV_B), lambda b, pt: (layer, 0, 0))]
                + [idx_page_spec(j) for j in range(n_pages)]
                + [page_spec(j, n_kva) for j in range(n_pages)]
                + [page_spec(j, n_kvb) for j in range(n_pages)])
    kernel = functools.partial(_sample_attn_kernel, n_pages=n_pages, topk=topk, lam_init=lam_init)
    grid_spec = pltpu.PrefetchScalarGridSpec(
        num_scalar_prefetch=1,
        grid=(batch,),
        in_specs=in_specs,
        out_specs=pl.BlockSpec((t_new, d_mix), lambda b, pt: (b, 0)),
        scratch_shapes=[pltpu.VMEM((t_new, width), I32), pltpu.VMEM((t_new, width), F32),
                        pltpu.VMEM((H_A // HKV_A * t_new, width), F32),
                        pltpu.VMEM((2 * H_B * t_new, width), F32)],
    )
    return pl.pallas_call(
        kernel,
        grid_spec=grid_spec,
        out_shape=jax.ShapeDtypeStruct((batch * t_new, d_mix), F32),
        compiler_params=_cparams(("parallel",)),
        name="sample_attention",
    )(page_table, qa, kva_new, qi, wcol, ki_new, qb, kvb_new, qm, pool_mem, *lam_params, g,
      *([pool_idx] * n_pages), *([pool_kva] * n_pages), *([pool_kvb] * n_pages))


def _attn_out_kernel(x_ref, mix_ref, wo_ref, g_ref, b_ref, h_ref, *, alpha):
    a = _mm(mix_ref[...].astype(MXU_DTYPE), wo_ref[...])
    h_ref[...] = _layer_norm(alpha * x_ref[...] + a, g_ref[...], b_ref[...])


def _attn_out(x2d, mixed, w_o, ln_g, ln_b, layer, alpha, tm):
    m, d = x2d.shape
    row = pl.BlockSpec((tm, d), lambda i: (i, 0))
    vec = pl.BlockSpec((None, 1, d), lambda i: (layer, 0, 0))
    return pl.pallas_call(
        functools.partial(_attn_out_kernel, alpha=alpha),
        grid=(m // tm,),
        in_specs=[row, pl.BlockSpec((tm, mixed.shape[1]), lambda i: (i, 0)),
                  pl.BlockSpec((None, mixed.shape[1], d), lambda i: (layer, 0, 0), pipeline_mode=pl.Buffered(1)),
                  vec, vec],
        out_specs=row,
        out_shape=jax.ShapeDtypeStruct((m, d), F32),
        compiler_params=_cparams(("parallel",)),
        name="attn_out_ln",
    )(x2d, mixed, w_o, ln_g, ln_b)


def _ffn_kernel(h_ref, wg_ref, wu_ref, wd_ref, g_ref, b_ref, o_ref, hb_ref, acc_ref, *, alpha):
    j = pl.program_id(1)

    @pl.when(j == 0)
    def _():
        hb_ref[...] = h_ref[...].astype(MXU_DTYPE)
        acc_ref[...] = jnp.zeros_like(acc_ref)

    hb = hb_ref[...]
    gate = _mm(hb, wg_ref[...])
    up = _mm(hb, wu_ref[...])
    act = gate * jax.nn.sigmoid(gate) * up
    acc_ref[...] += _mm(act.astype(MXU_DTYPE), wd_ref[...])

    @pl.when(j == pl.num_programs(1) - 1)
    def _():
        o_ref[...] = _layer_norm(alpha * h_ref[...] + acc_ref[...], g_ref[...], b_ref[...])


def _ffn(h2d, w_gate, w_up, w_down, ln_g, ln_b, layer, alpha, tm, tf):
    m, d = h2d.shape
    f = w_gate.shape[-1]
    row = pl.BlockSpec((tm, d), lambda i, j: (i, 0))
    vec = pl.BlockSpec((None, 1, d), lambda i, j: (layer, 0, 0))
    return pl.pallas_call(
        functools.partial(_ffn_kernel, alpha=alpha),
        grid=(m // tm, f // tf),
        in_specs=[row,
                  pl.BlockSpec((None, d, tf), lambda i, j: (layer, 0, j)),
                  pl.BlockSpec((None, d, tf), lambda i, j: (layer, 0, j)),
                  pl.BlockSpec((None, tf, d), lambda i, j: (layer, j, 0)),
                  vec, vec],
        out_specs=row,
        out_shape=jax.ShapeDtypeStruct((m, d), F32),
        scratch_shapes=[pltpu.VMEM((tm, d), MXU_DTYPE), pltpu.VMEM((tm, d), F32)],
        compiler_params=_cparams(("parallel", "arbitrary")),
        name="swiglu_ln",
    )(h2d, w_gate, w_up, w_down, ln_g, ln_b)


def _row_tile(m, cap):
    t = min(m, cap)
    while m % t:
        t //= 2
    return t


def _ff_tile(f, cap):
    best = LANES
    for t in range(LANES, cap + 1, LANES):
        if f % t == 0:
            best = t
    return best


def kernel(x_prompt, x_sample, mem_prompt, cache_kv_a, cache_idx_k, cache_kv_b, cache_mem_kv, page_table,
           w_in, w_mem_kv, lambda_q1, lambda_k1, lambda_q2, lambda_k2, subln_g, w_o,
           ln1_g, ln1_b, w_gate, w_up, w_down, ln2_g, ln2_b):
    depth, d_model, _ = w_in.shape
    batch, seq, _ = x_prompt.shape
    dec_batch, t_new, _ = x_sample.shape
    n_pages = page_table.shape[1]
    past = n_pages * PAGE_SIZE
    n_mem = mem_prompt.shape[1]
    alpha = (2.0 * depth) ** 0.25

    n_main = W_QA + W_KVA + W_QB + W_KVB
    w_in_p = jnp.concatenate(
        [w_in[:, :, :n_main], w_in[:, :, n_main + W_QM:n_main + W_QM + W_QI], w_in[:, :, n_main:n_main + W_QM],
         w_in[:, :, n_main + W_QM + W_QI:],
         jnp.zeros((depth, d_model, N_IN_PAD - w_in.shape[2]), w_in.dtype)], axis=-1).astype(MXU_DTYPE)
    w_mem_c = w_mem_kv.astype(MXU_DTYPE)
    w_o_c, w_gate_c, w_up_c, w_down_c = (w.astype(MXU_DTYPE) for w in (w_o, w_gate, w_up, w_down))
    vec3 = lambda a: a.astype(F32).reshape(depth, 1, a.shape[-1])
    lam_params = tuple(vec3(a) for a in (lambda_q1, lambda_k1, lambda_q2, lambda_k2))
    g_sub = vec3(subln_g)
    ln1g, ln1b, ln2g, ln2b = (vec3(a) for a in (ln1_g, ln1_b, ln2_g, ln2_b))

    pool_kva = cache_kv_a.reshape(-1, LANES)
    pool_idx = cache_idx_k.reshape(depth * cache_idx_k.shape[1], PAGE_SIZE, D_IDX)
    pool_kvb = cache_kv_b.reshape(-1, LANES)
    pool_mem = cache_mem_kv.reshape(-1, LANES)

    tm_p = _row_tile(seq, 512)
    m_s = dec_batch * t_new
    tm_s = _row_tile(m_s, 512)
    pos_p = jnp.arange(seq, dtype=I32)
    pos_s = past + (jnp.arange(tm_s, dtype=I32) % t_new)
    tabs_p = _rope_tables(pos_p, DH_A) + _rope_tables(pos_p, DQK_B)
    tabs_s = _rope_tables(pos_s, DH_A) + _rope_tables(pos_s, DQK_B)

    xp = x_prompt.reshape(batch * seq, d_model)
    xs = x_sample.reshape(m_s, d_model)
    mem2d = mem_prompt.reshape(batch * n_mem, d_model)
    tf = _ff_tile(w_gate.shape[-1], 512)

    outs = [[] for _ in range(7)]
    for l in range(depth):
        lam_init = 0.8 - 0.6 * math.exp(-0.3 * l)

        qa, kva, qb, kvb, qi, qm, ki, wi = _project(xp, w_in_p, l, tabs_p, seq // tm_p, tm_p)
        mem_kv = _matmul(mem2d, w_mem_c, l, _row_tile(batch * n_mem, 512))
        mixed = _prompt_attention(qa, kva, qi, wi, ki, qb, kvb, qm, mem_kv, lam_params, g_sub,
                                  batch, seq, l, lam_init)
        h = _attn_out(xp, mixed, w_o_c, ln1g, ln1b, l, alpha, tm_p)
        xp = _ffn(h, w_gate_c, w_up_c, w_down_c, ln2g, ln2b, l, alpha, tm_p, tf)
        outs[0].append(kva.reshape(batch, seq, 2, HKV_A, DH_A))
        outs[1].append(ki.reshape(batch, seq, D_IDX))
        outs[2].append(kvb.reshape(batch, seq, 2, H_B, DV_B))
        outs[3].append(mem_kv.reshape(batch, n_mem, 2, H_MEM, DH_MEM))

        qa, kva, qb, kvb, qi, qm, ki, wi = _project(xs, w_in_p, l, tabs_s, 1, tm_s)
        wcol = wi.reshape(dec_batch, t_new, H_IDX).transpose(0, 2, 1).reshape(dec_batch * H_IDX * t_new, 1)
        mixed = _sample_attention(page_table, qa, kva, qi, wcol, ki, qb, kvb, qm, pool_idx, pool_kva, pool_kvb,
                                  pool_mem, lam_params, g_sub, dec_batch, t_new, l, lam_init)
        h = _attn_out(xs, mixed, w_o_c, ln1g, ln1b, l, alpha, tm_s)
        xs = _ffn(h, w_gate_c, w_up_c, w_down_c, ln2g, ln2b, l, alpha, tm_s, tf)
        outs[4].append(kva.reshape(dec_batch, t_new, 2, HKV_A, DH_A))
        outs[5].append(ki.reshape(dec_batch, t_new, D_IDX))
        outs[6].append(kvb.reshape(dec_batch, t_new, 2, H_B, DV_B))

    return (xp.reshape(batch, seq, d_model), xs.reshape(dec_batch, t_new, d_model)) + tuple(
        jnp.stack(o) for o in outs)
```

```python
import functools
import math

import jax
import jax.numpy as jnp
from jax import lax
from jax.experimental import pallas as pl
from jax.experimental.pallas import tpu as pltpu

F32 = jnp.float32
I32 = jnp.int32
MXU_DTYPE = jnp.bfloat16

H_A, DH_A, HKV_A = 8, 128, 2
H_IDX, D_IDX = 16, 64
INDEX_TOPK_MAX = 256
H_B, DQK_B, DV_B = 4, 64, 128
H_MEM, DH_MEM = 4, 128
PAGE_SIZE = 128
ROPE_THETA = 500000.0
ROPE_DIV = 4
LN_EPS = 1e-5
RMS_EPS = 1e-5
D_MIX = H_A * DH_A + H_B * DV_B + H_MEM * DH_MEM
N_KVA, N_KVB, N_KVM = 2 * HKV_A, 2 * H_B, 2 * H_MEM

LANES = 128
VMEM_LIMIT_BYTES = 56 * 1024 * 1024

W_QA, W_KVA, W_QB, W_KVB, W_QI, W_QM = 1024, 512, 512, 1024, 1024, 512
C_QA, C_KVA, C_QB, C_KVB, C_QI, C_QM, C_TAIL = 0, 8, 12, 16, 24, 32, 36
N_IN_PAD = 37 * LANES

LOG2_E = math.log2(math.e)
NEG_INF = float("-inf")
INT_MIN = -(2 ** 31)
NEG_INF_KEY = -2139095041


def _cparams(sem):
    return pltpu.CompilerParams(dimension_semantics=sem, vmem_limit_bytes=VMEM_LIMIT_BYTES)


def _mm(a, b):
    return jnp.dot(a, b, preferred_element_type=F32)


def _mm_nt(a, b):
    return lax.dot_general(a, b, (((1,), (1,)), ((), ())), preferred_element_type=F32)


def _layer_norm(y, g, b):
    mu = jnp.mean(y, axis=-1, keepdims=True)
    d = y - mu
    var = jnp.mean(d * d, axis=-1, keepdims=True)
    return d * lax.rsqrt(var + LN_EPS) * g + b


def _exp_and_sum(s, axis=-1):
    m = jnp.max(s, axis=axis, keepdims=True)
    e = jnp.exp(s - m)
    return e, jnp.sum(e, axis=axis, keepdims=True)


def _ordered_key(x):
    k = pltpu.bitcast(x, I32)
    return jnp.where(k < 0, k ^ jnp.int32(0x7FFFFFFF), k)


def _half_lanes(x, c):
    lane = lax.broadcasted_iota(I32, x.shape, 1)
    return jnp.where((lane >= c * DQK_B) & (lane < (c + 1) * DQK_B), x, 0.0)


def _lambda_value(lq1, lk1, lq2, lk2, lam_init):
    a = jnp.sum(lq1[...] * lk1[...], axis=-1, keepdims=True)
    b = jnp.sum(lq2[...] * lk2[...], axis=-1, keepdims=True)
    return jnp.exp(a) - jnp.exp(b) + lam_init


def _diff_finish(o, g_ref, lam_init):
    o = o * lax.rsqrt(jnp.mean(o * o, axis=-1, keepdims=True) + RMS_EPS)
    return o * g_ref[...] * (1.0 - lam_init)


def _rope_tables(pos, head_dim):
    rot = head_dim // ROPE_DIV
    half = rot // 2
    t = pos.shape[0]
    inv_freq = jnp.float32(ROPE_THETA) ** (-jnp.arange(half, dtype=F32) / half)
    ang = pos.astype(F32)[:, None] * inv_freq[None, :]
    cos, sin = jnp.cos(ang), jnp.sin(ang)
    c = jnp.concatenate([cos, cos, jnp.ones((t, head_dim - rot), F32)], axis=-1)
    s_hi = jnp.concatenate([-sin, jnp.zeros((t, head_dim - half), F32)], axis=-1)
    s_lo = jnp.concatenate([jnp.zeros((t, half), F32), sin, jnp.zeros((t, head_dim - rot), F32)], axis=-1)
    rep = LANES // head_dim
    return tuple(jnp.tile(a, (1, rep)) for a in (c, s_hi, s_lo))


def _proj_kernel(x_ref, w_ref, ca, sa1, sa2, cb, sb1, sb2,
                 qa_o, kva_o, qb_o, kvb_o, qi_o, qm_o, ki_o, wi_o):
    xb = x_ref[...].astype(MXU_DTYPE)
    tm = x_ref.shape[0]

    def rope(z, c, s_hi, s_lo, half):
        return z * c[...] + pltpu.roll(z, LANES - half, 1) * s_hi[...] + pltpu.roll(z, half, 1) * s_lo[...]

    def rope_a(z):
        return rope(z, ca, sa1, sa2, DH_A // ROPE_DIV // 2)

    def rope_b(z):
        return rope(z, cb, sb1, sb2, DQK_B // ROPE_DIV // 2)

    def emit(out_ref, c0, kinds, interleave=False):
        n = len(kinds)
        z = _mm(xb, w_ref[:, c0 * LANES:(c0 + n) * LANES])
        for j, kind in enumerate(kinds):
            zj = z[:, j * LANES:(j + 1) * LANES]
            if kind == "a":
                zj = rope_a(zj)
            elif kind == "b":
                zj = rope_b(zj)
            if interleave:
                out_ref[pl.ds(j, tm, stride=n), :] = zj
            else:
                out_ref[:, j * LANES:(j + 1) * LANES] = zj

    emit(qa_o, C_QA, "a" * 8)
    emit(kva_o, C_KVA, "aa--", interleave=True)
    emit(qb_o, C_QB, "bbbb")
    emit(kvb_o, C_KVB, "bbbb----", interleave=True)
    emit(qi_o, C_QI, "b" * 8)
    emit(qm_o, C_QM, "----")
    z = _mm(xb, w_ref[:, C_TAIL * LANES:(C_TAIL + 1) * LANES])
    lane = lax.broadcasted_iota(I32, z.shape, 1)
    zt = jnp.where(lane < D_IDX, rope_b(z), z)
    ki_o[...] = zt[:, :D_IDX]
    wi_o[...] = zt[:, D_IDX:D_IDX + H_IDX]


def _project(x2d, w_in_p, layer, tabs, n_tab_blocks, tm):
    m, d = x2d.shape
    shapes = ((1, W_QA), (N_KVA, LANES), (1, W_QB), (N_KVB, LANES), (1, W_QI), (1, W_QM), (1, D_IDX), (1, H_IDX))
    tab_spec = pl.BlockSpec((tm, LANES), lambda i: (i % n_tab_blocks, 0))
    return pl.pallas_call(
        _proj_kernel,
        grid=(m // tm,),
        in_specs=[pl.BlockSpec((tm, d), lambda i: (i, 0)),
                  pl.BlockSpec((None, d, N_IN_PAD), lambda i: (layer, 0, 0), pipeline_mode=pl.Buffered(1))]
                 + [tab_spec] * 6,
        out_specs=[pl.BlockSpec((tm * r, w), lambda i: (i, 0)) for r, w in shapes],
        out_shape=[jax.ShapeDtypeStruct((m * r, w), F32) for r, w in shapes],
        compiler_params=_cparams(("parallel",)),
        name="proj_rope",
    )(x2d, w_in_p, *tabs)


def _matmul_kernel(x_ref, w_ref, o_ref):
    tm = x_ref.shape[0]
    n = w_ref.shape[1] // LANES
    z = _mm(x_ref[...].astype(MXU_DTYPE), w_ref[...])
    for j in range(n):
        o_ref[pl.ds(j, tm, stride=n), :] = z[:, j * LANES:(j + 1) * LANES]


def _matmul(x2d, w, layer, tm):
    m, d = x2d.shape
    n = w.shape[-1]
    return pl.pallas_call(
        _matmul_kernel,
        grid=(m // tm,),
        in_specs=[pl.BlockSpec((tm, d), lambda i: (i, 0)),
                  pl.BlockSpec((None, d, n), lambda i: (layer, 0, 0), pipeline_mode=pl.Buffered(1))],
        out_specs=pl.BlockSpec((tm * n // LANES, LANES), lambda i: (i, 0)),
        out_shape=jax.ShapeDtypeStruct((m * n // LANES, LANES), F32),
        compiler_params=_cparams(("parallel",)),
        name="mem_kv_proj",
    )(x2d, w)


def _kth_threshold(key_ref, rows, n_chunks, topk):
    ones = jnp.ones((LANES, LANES), MXU_DTYPE)

    def count_ge(cand):
        acc = jnp.zeros((rows, LANES), F32)
        for c in range(n_chunks):
            acc = acc + jnp.where(key_ref[:, c * LANES:(c + 1) * LANES] >= cand, 1.0, 0.0)
        return _mm(acc.astype(MXU_DTYPE), ones)

    def body(it, t):
        cand = t + lax.shift_left(jnp.int32(1), jnp.int32(31) - it)
        return jnp.where(count_ge(cand) >= topk, cand, t)

    t = lax.fori_loop(0, 32, body, jnp.full((rows, LANES), INT_MIN, I32))
    return t, count_ge(t)


def _selection_bias(key_ref, bias_ref, t, n_ge, rows, n_chunks, topk):
    floor_t = jnp.maximum(t, jnp.int32(NEG_INF_KEY + 1))
    tie_rows = jnp.where((n_ge > topk) & (t > NEG_INF_KEY), 1.0, 0.0)
    has_tie = jnp.max(tie_rows) > 0.5

    @pl.when(jnp.logical_not(has_tie))
    def _():
        for c in range(n_chunks):
            sl = slice(c * LANES, (c + 1) * LANES)
            bias_ref[:, sl] = jnp.where(key_ref[:, sl] >= floor_t, 0.0, NEG_INF)

    @pl.when(has_tie)
    def _():
        ones = jnp.ones((LANES, LANES), MXU_DTYPE)
        r_i = lax.broadcasted_iota(I32, (LANES, LANES), 0)
        c_i = lax.broadcasted_iota(I32, (LANES, LANES), 1)
        strict_upper = jnp.where(r_i < c_i, 1.0, 0.0).astype(MXU_DTYPE)
        n_gt = jnp.zeros((rows, LANES), F32)
        for c in range(n_chunks):
            sl = slice(c * LANES, (c + 1) * LANES)
            n_gt = n_gt + _mm(jnp.where(key_ref[:, sl] > t, 1.0, 0.0).astype(MXU_DTYPE), ones)
        need = topk - n_gt
        run = jnp.zeros((rows, LANES), F32)
        for c in range(n_chunks):
            sl = slice(c * LANES, (c + 1) * LANES)
            k = key_ref[:, sl]
            eq = jnp.where(k == t, 1.0, 0.0).astype(MXU_DTYPE)
            before = run + _mm(eq, strict_upper)
            run = run + _mm(eq, ones)
            keep = (k > t) | ((k == t) & (before < need))
            bias_ref[:, sl] = jnp.where(keep & (k > NEG_INF_KEY), 0.0, NEG_INF)


def _count_rows(pred_fn, key_ref, n_keys):
    acc = jnp.zeros((LANES, LANES), F32)
    for c in range(n_keys // LANES):
        acc = acc + jnp.where(pred_fn(key_ref[c * LANES:(c + 1) * LANES, :]), 1.0, 0.0)
    return jnp.sum(acc, axis=0, keepdims=True)


def _kth_threshold_t(key_ref, n_keys, topk):
    def body(it, t):
        cand = t + lax.shift_left(jnp.int32(1), jnp.int32(31) - it)
        return jnp.where(_count_rows(lambda k: k >= cand, key_ref, n_keys) >= topk, cand, t)

    t = lax.fori_loop(0, 32, body, jnp.full((1, LANES), INT_MIN, I32))
    return t, _count_rows(lambda k: k >= t, key_ref, n_keys)


def _selection_bias_t(key_ref, bias_ref, t, n_ge, n_keys, topk):
    floor_t = jnp.maximum(t, jnp.int32(NEG_INF_KEY + 1))
    tie_lanes = jnp.where((n_ge > topk) & (t > NEG_INF_KEY), 1.0, 0.0)
    has_tie = jnp.max(tie_lanes) > 0.5

    @pl.when(jnp.logical_not(has_tie))
    def _():
        for c in range(n_keys // LANES):
            sl = slice(c * LANES, (c + 1) * LANES)
            bias_ref[sl, :] = jnp.where(key_ref[sl, :] >= floor_t, 0.0, NEG_INF)

    @pl.when(has_tie)
    def _():
        r_i = lax.broadcasted_iota(I32, (LANES, LANES), 0)
        c_i = lax.broadcasted_iota(I32, (LANES, LANES), 1)
        strict_lower = jnp.where(c_i < r_i, 1.0, 0.0).astype(MXU_DTYPE)
        need = topk - _count_rows(lambda k: k > t, key_ref, n_keys)
        run = jnp.zeros((1, LANES), F32)
        for c in range(n_keys // LANES):
            sl = slice(c * LANES, (c + 1) * LANES)
            k = key_ref[sl, :]
            eq = jnp.where(k == t, 1.0, 0.0)
            before = run + _mm(strict_lower, eq.astype(MXU_DTYPE))
            run = run + jnp.sum(eq, axis=0, keepdims=True)
            keep = (k > t) | ((k == t) & (before < need))
            bias_ref[sl, :] = jnp.where(keep & (k > NEG_INF_KEY), 0.0, NEG_INF)


Q_BLOCK = 128
N_KV_CLASSES = 4


def _prompt_attn_kernel(*refs, topk, lam_init, q0, has_prev):
    (qa_ref, kva_ref, qi_ref, wit_ref, ki_ref, qb_ref, kvb_ref, qm_ref, mem_ref,
     lq1, lk1, lq2, lk2, g_ref) = refs[:14]
    rest = refs[14 + (1 if has_prev else 0):]
    o_ref, ka_s, vat_s, kb_s, vbt_s, ki_s, km_s, vmt_s, key_ref, bias_ref = rest
    n_keys = ki_ref.shape[0]
    n_mem = km_s.shape[1]
    qi = pl.program_id(1)

    @pl.when(qi == 0)
    def _():
        for g in range(HKV_A):
            ka_s[g] = kva_ref[pl.ds(g, n_keys, stride=N_KVA), :].astype(MXU_DTYPE)
            vat_s[g] = kva_ref[pl.ds(HKV_A + g, n_keys, stride=N_KVA), :].T.astype(MXU_DTYPE)
        for h in range(H_B):
            kb_s[h] = kvb_ref[pl.ds(h, n_keys, stride=N_KVB), :].astype(MXU_DTYPE)
            vbt_s[h] = kvb_ref[pl.ds(H_B + h, n_keys, stride=N_KVB), :].T.astype(MXU_DTYPE)
        for h in range(H_MEM):
            km_s[h] = mem_ref[pl.ds(h, n_mem, stride=N_KVM), :].astype(MXU_DTYPE)
            vmt_s[h] = mem_ref[pl.ds(H_MEM + h, n_mem, stride=N_KVM), :].T.astype(MXU_DTYPE)
        ki_s[...] = ki_ref[...].astype(MXU_DTYPE)

    q_pos = (q0 + qi) * Q_BLOCK + lax.broadcasted_iota(I32, (n_keys, Q_BLOCK), 1)
    k_pos = lax.broadcasted_iota(I32, (n_keys, Q_BLOCK), 0)
    causal = k_pos <= q_pos

    ki = ki_s[...]
    sc = jnp.zeros((n_keys, Q_BLOCK), F32)
    for hp in range(H_IDX // 2):
        q2 = jnp.concatenate([qi_ref[:, (2 * hp + j) * D_IDX:(2 * hp + j + 1) * D_IDX] for j in range(2)],
                             axis=0).astype(MXU_DTYPE)
        s = _mm_nt(ki, q2)
        for j in range(2):
            h = 2 * hp + j
            sc = sc + wit_ref[h:h + 1, :] * jnp.maximum(s[:, j * Q_BLOCK:(j + 1) * Q_BLOCK], 0.0)
    key_ref[...] = _ordered_key(jnp.where(causal, sc, NEG_INF))
    t, n_ge = _kth_threshold_t(key_ref, n_keys, topk)
    _selection_bias_t(key_ref, bias_ref, t, n_ge, n_keys, topk)

    def softmax_cols(s, n_cols, scale, bias):
        es, ls = [], []
        for r in range(n_cols):
            x = s[:, r * Q_BLOCK:(r + 1) * Q_BLOCK] + bias
            e = jnp.exp2((x - jnp.max(x, axis=0, keepdims=True)) * (scale * LOG2_E))
            es.append(e.astype(MXU_DTYPE))
            ls.append(jnp.sum(e, axis=0, keepdims=True))
        return jnp.concatenate(es, axis=1), ls

    rep = H_A // HKV_A
    bias_sel = bias_ref[...]
    for g in range(HKV_A):
        q4 = jnp.concatenate([qa_ref[:, (g * rep + r) * DH_A:(g * rep + r + 1) * DH_A] for r in range(rep)],
                             axis=0).astype(MXU_DTYPE)
        e, ls = softmax_cols(_mm_nt(ka_s[g], q4), rep, DH_A ** -0.5, bias_sel)
        ot = _mm(vat_s[g], e)
        for r in range(rep):
            h = g * rep + r
            o_ref[:, h * DH_A:(h + 1) * DH_A] = (ot[:, r * Q_BLOCK:(r + 1) * Q_BLOCK] / ls[r]).T

    lam = _lambda_value(lq1, lk1, lq2, lk2, lam_init)
    causal_bias = jnp.where(causal, 0.0, NEG_INF)
    off_b = H_A * DH_A
    for h in range(H_B):
        qh = qb_ref[:, h * DV_B:(h + 1) * DV_B]
        q2 = jnp.concatenate([_half_lanes(qh, 0), _half_lanes(qh, 1)], axis=0).astype(MXU_DTYPE)
        e, ls = softmax_cols(_mm_nt(kb_s[h], q2), 2, DQK_B ** -0.5, causal_bias)
        ot = _mm(vbt_s[h], e)
        o = (ot[:, :Q_BLOCK] / ls[0] - lam * (ot[:, Q_BLOCK:] / ls[1])).T
        o_ref[:, off_b + h * DV_B:off_b + (h + 1) * DV_B] = _diff_finish(o, g_ref, lam_init)

    off_m = off_b + H_B * DV_B
    for h in range(H_MEM):
        q = qm_ref[:, h * DH_MEM:(h + 1) * DH_MEM].astype(MXU_DTYPE)
        e, l = _exp_and_sum(_mm_nt(km_s[h], q) * DH_MEM ** -0.5, axis=0)
        o_ref[:, off_m + h * DH_MEM:off_m + (h + 1) * DH_MEM] = (_mm(vmt_s[h], e.astype(MXU_DTYPE)) / l).T


def _prompt_attention(qa, kva, qi, wit, ki, qb, kvb, qm, mem_kv, lam_params, g, batch, seq, layer, lam_init):
    n_q = seq // Q_BLOCK
    q_per_class = n_q // N_KV_CLASSES
    n_mem = mem_kv.shape[0] // (batch * N_KVM)
    topk = min(INDEX_TOPK_MAX, seq // 4)
    kva3 = kva.reshape(batch, seq * N_KVA, LANES)
    kvb3 = kvb.reshape(batch, seq * N_KVB, LANES)
    ki3 = ki.reshape(batch, seq, D_IDX)
    mem3 = mem_kv.reshape(batch, n_mem * N_KVM, LANES)
    vec_spec = pl.BlockSpec((None, 1, DQK_B), lambda b, i: (layer, 0, 0))

    def kv_spec(rows, w):
        return pl.BlockSpec((None, rows, w), lambda b, i: (b, 0, 0), pipeline_mode=pl.Buffered(1))

    mixed = None
    for c in range(N_KV_CLASSES):
        q0 = c * q_per_class
        n_keys = (c + 1) * q_per_class * Q_BLOCK

        def q_spec(w, q0=q0):
            return pl.BlockSpec((Q_BLOCK, w), lambda b, i: (b * n_q + q0 + i, 0))

        in_specs = [q_spec(W_QA), kv_spec(n_keys * N_KVA, LANES), q_spec(W_QI),
                    pl.BlockSpec((H_IDX, Q_BLOCK), lambda b, i, q0=q0: (0, b * n_q + q0 + i)),
                    kv_spec(n_keys, D_IDX), q_spec(W_QB), kv_spec(n_keys * N_KVB, LANES), q_spec(W_QM),
                    kv_spec(n_mem * N_KVM, LANES), vec_spec, vec_spec, vec_spec, vec_spec,
                    pl.BlockSpec((None, 1, DV_B), lambda b, i: (layer, 0, 0))]
        args = [qa, kva3, qi, wit, ki3, qb, kvb3, qm, mem3, *lam_params, g]
        aliases = {}
        if mixed is not None:
            in_specs.append(pl.BlockSpec(memory_space=pl.ANY))
            aliases = {len(args): 0}
            args.append(mixed)
        kernel = functools.partial(_prompt_attn_kernel, topk=topk, lam_init=lam_init, q0=q0,
                                   has_prev=mixed is not None)
        mixed = pl.pallas_call(
            kernel,
            grid=(batch, q_per_class),
            in_specs=in_specs,
            out_specs=q_spec(D_MIX),
            out_shape=jax.ShapeDtypeStruct((batch * seq, D_MIX), F32),
            scratch_shapes=[pltpu.VMEM((HKV_A, n_keys, DH_A), MXU_DTYPE), pltpu.VMEM((HKV_A, DH_A, n_keys), MXU_DTYPE),
                            pltpu.VMEM((H_B, n_keys, DV_B), MXU_DTYPE), pltpu.VMEM((H_B, DV_B, n_keys), MXU_DTYPE),
                            pltpu.VMEM((n_keys, D_IDX), MXU_DTYPE),
                            pltpu.VMEM((H_MEM, n_mem, DH_MEM), MXU_DTYPE),
                            pltpu.VMEM((H_MEM, DH_MEM, n_mem), MXU_DTYPE),
                            pltpu.VMEM((n_keys, Q_BLOCK), I32), pltpu.VMEM((n_keys, Q_BLOCK), F32)],
            input_output_aliases=aliases,
            compiler_params=_cparams(("parallel", "arbitrary")),
            name=f"prompt_attention_kv{c}",
        )(*args)
    return mixed


def _pad_rows(x, rows):
    return jnp.concatenate([x, jnp.zeros((rows - x.shape[0], x.shape[1]), x.dtype)], axis=0)


def _sample_index_kernel(pt_ref, qi_ref, wcol_ref, kin_ref, *rest, n_pages, group):
    del pt_ref
    idx_pages = rest[:group * n_pages]
    key_o = rest[group * n_pages]
    t_new = qi_ref.shape[0] // group
    past = n_pages * PAGE_SIZE

    lane = lax.broadcasted_iota(I32, (t_new, LANES), 1)
    tok = lax.broadcasted_iota(I32, (t_new, LANES), 0)
    new_ok = lane <= tok

    for b in range(group):
        rows = slice(b * t_new, (b + 1) * t_new)
        q_ht = jnp.concatenate([qi_ref[rows, h * D_IDX:(h + 1) * D_IDX] for h in range(H_IDX)],
                               axis=0).astype(MXU_DTYPE)
        wcol = wcol_ref[b * H_IDX * t_new:(b + 1) * H_IDX * t_new, :]

        def index_scores(qk):
            s = jnp.maximum(qk, 0.0) * wcol
            acc = s[0:t_new]
            for h in range(1, H_IDX):
                acc = acc + s[h * t_new:(h + 1) * t_new]
            return acc

        for j in range(n_pages):
            qk = _mm(q_ht, idx_pages[b * n_pages + j][...].astype(MXU_DTYPE))
            key_o[rows, j * LANES:(j + 1) * LANES] = _ordered_key(index_scores(qk))
        s_new = index_scores(_mm_nt(q_ht, _pad_rows(kin_ref[rows, :], LANES).astype(MXU_DTYPE)))
        key_o[rows, past:past + LANES] = _ordered_key(jnp.where(new_ok, s_new, NEG_INF))


def _sample_index(page_table, qi, wcol, ki_new, pool_idx, batch, t_new, layer, n_phys, group):
    n_pages = page_table.shape[1]
    width = (n_pages + 1) * LANES

    def row_spec(rows, w):
        return pl.BlockSpec((rows, w), lambda s, pt: (s, 0))

    def page_spec(b, j):
        return pl.BlockSpec((None, D_IDX, PAGE_SIZE),
                            lambda s, pt: (layer * n_phys + pt[s * group + b, j], 0, 0))

    in_specs = ([row_spec(group * t_new, W_QI), row_spec(group * H_IDX * t_new, 1), row_spec(group * t_new, D_IDX)]
                + [page_spec(b, j) for b in range(group) for j in range(n_pages)])
    grid_spec = pltpu.PrefetchScalarGridSpec(
        num_scalar_prefetch=1, grid=(batch // group,), in_specs=in_specs,
        out_specs=pl.BlockSpec((group * t_new, width), lambda s, pt: (s, 0)))
    return pl.pallas_call(
        functools.partial(_sample_index_kernel, n_pages=n_pages, group=group),
        grid_spec=grid_spec,
        out_shape=jax.ShapeDtypeStruct((batch * t_new, width), I32),
        compiler_params=_cparams(("parallel",)),
        name="sample_indexer",
    )(page_table, qi, wcol, ki_new, *([pool_idx] * (group * n_pages)))


def _sample_select_kernel(key_ref, bias_ref, *, topk):
    rows, width = key_ref.shape
    n_chunks = width // LANES
    t, n_ge = _kth_threshold(key_ref, rows, n_chunks, topk)
    _selection_bias(key_ref, bias_ref, t, n_ge, rows, n_chunks, topk)


def _sample_select(keys, topk, tr):
    m, width = keys.shape
    spec = pl.BlockSpec((tr, width), lambda i: (i, 0))
    return pl.pallas_call(
        functools.partial(_sample_select_kernel, topk=topk),
        grid=(m // tr,),
        in_specs=[spec],
        out_specs=spec,
        out_shape=jax.ShapeDtypeStruct((m, width), F32),
        compiler_params=_cparams(("parallel",)),
        name="sample_topk_select",
    )(keys)


def _sample_attn_kernel(pt_ref, qa_ref, kvan_ref, bias_ref, qb_ref, kvbn_ref, qm_ref, mem_ref,
                        lq1, lk1, lq2, lk2, g_ref, *rest, n_pages, lam_init):
    del pt_ref
    kva_pages = rest[:n_pages]
    kvb_pages = rest[n_pages:2 * n_pages]
    o_ref, sa_ref, sb_ref = rest[2 * n_pages:]
    t_new = qa_ref.shape[0]
    past = n_pages * PAGE_SIZE

    lane = lax.broadcasted_iota(I32, (t_new, LANES), 1)
    tok = lax.broadcasted_iota(I32, (t_new, LANES), 0)
    new_ok = lane <= tok

    def chunk_rows(ref, j, rows, n):
        return ref[pl.ds(j, rows, stride=n), :].astype(MXU_DTYPE)

    def new_rows(ref, j, n):
        return _pad_rows(ref[pl.ds(j, t_new, stride=n), :], LANES).astype(MXU_DTYPE)

    scale_a = DH_A ** -0.5
    rep = H_A // HKV_A
    bias_a = jnp.concatenate([bias_ref[...]] * rep, axis=0)
    for g in range(HKV_A):
        q = jnp.concatenate([qa_ref[:, (g * rep + r) * DH_A:(g * rep + r + 1) * DH_A] for r in range(rep)],
                            axis=0).astype(MXU_DTYPE)
        for j in range(n_pages):
            sa_ref[:, j * LANES:(j + 1) * LANES] = _mm_nt(q, chunk_rows(kva_pages[j], g, PAGE_SIZE, N_KVA))
        sa_ref[:, past:past + LANES] = _mm_nt(q, new_rows(kvan_ref, g, N_KVA))
        e, l = _exp_and_sum(sa_ref[...] * scale_a + bias_a)
        e = e.astype(MXU_DTYPE)
        acc = _mm(e[:, past:past + LANES], new_rows(kvan_ref, HKV_A + g, N_KVA))
        for j in range(n_pages):
            acc = acc + _mm(e[:, j * LANES:(j + 1) * LANES], chunk_rows(kva_pages[j], HKV_A + g, PAGE_SIZE, N_KVA))
        acc = acc / l
        for r in range(rep):
            h = g * rep + r
            o_ref[:, h * DH_A:(h + 1) * DH_A] = acc[r * t_new:(r + 1) * t_new]

    lam = _lambda_value(lq1, lk1, lq2, lk2, lam_init)
    scale_b = DQK_B ** -0.5
    for h in range(H_B):
        qh = qb_ref[:, h * DV_B:(h + 1) * DV_B]
        q = jnp.concatenate([_half_lanes(qh, 0), _half_lanes(qh, 1)], axis=0).astype(MXU_DTYPE)
        rows = slice(2 * h * t_new, (2 * h + 2) * t_new)
        for j in range(n_pages):
            sb_ref[rows, j * LANES:(j + 1) * LANES] = _mm_nt(q, chunk_rows(kvb_pages[j], h, PAGE_SIZE, N_KVB))
        sb_ref[rows, past:past + LANES] = _mm_nt(q, new_rows(kvbn_ref, h, N_KVB))
    new_bias = jnp.where(new_ok, 0.0, NEG_INF)
    bias_b = jnp.concatenate(
        [jnp.zeros((N_KVB * t_new, past), F32), jnp.concatenate([new_bias] * N_KVB, axis=0)], axis=1)
    e, l = _exp_and_sum(sb_ref[...] * scale_b + bias_b)
    e = e.astype(MXU_DTYPE)
    off_b = H_A * DH_A
    for h in range(H_B):
        rows = slice(2 * h * t_new, (2 * h + 2) * t_new)
        acc = _mm(e[rows, past:past + LANES], new_rows(kvbn_ref, H_B + h, N_KVB))
        for j in range(n_pages):
            acc = acc + _mm(e[rows, j * LANES:(j + 1) * LANES], chunk_rows(kvb_pages[j], H_B + h, PAGE_SIZE, N_KVB))
        acc = acc / l[rows]
        o = acc[0:t_new] - lam * acc[t_new:2 * t_new]
        o_ref[:, off_b + h * DV_B:off_b + (h + 1) * DV_B] = _diff_finish(o, g_ref, lam_init)

    scale_m = DH_MEM ** -0.5
    off_m = off_b + H_B * DV_B
    n_mem = mem_ref.shape[0] // N_KVM
    for h in range(H_MEM):
        q = qm_ref[:, h * DH_MEM:(h + 1) * DH_MEM].astype(MXU_DTYPE)
        e, l = _exp_and_sum(_mm_nt(q, chunk_rows(mem_ref, h, n_mem, N_KVM)) * scale_m)
        o_ref[:, off_m + h * DH_MEM:off_m + (h + 1) * DH_MEM] = (
            _mm(e.astype(MXU_DTYPE), chunk_rows(mem_ref, H_MEM + h, n_mem, N_KVM)) / l)


def _sample_attention(page_table, qa, kva_new, bias, qb, kvb_new, qm, pool_kva, pool_kvb, pool_mem,
                      lam_params, g, batch, t_new, layer, lam_init, n_phys, mem_rows):
    n_pages = page_table.shape[1]
    width = (n_pages + 1) * LANES

    def row_spec(rows, w):
        return pl.BlockSpec((rows, w), lambda b, pt: (b, 0))

    def page_spec(j, n):
        return pl.BlockSpec((PAGE_SIZE * n, LANES), lambda b, pt: (layer * n_phys + pt[b, j], 0))

    vec_spec = pl.BlockSpec((None, 1, DQK_B), lambda b, pt: (layer, 0, 0))
    in_specs = ([row_spec(t_new, W_QA), row_spec(t_new * N_KVA, LANES), row_spec(t_new, width),
                 row_spec(t_new, W_QB), row_spec(t_new * N_KVB, LANES), row_spec(t_new, W_QM),
                 pl.BlockSpec((mem_rows, LANES), lambda b, pt: (layer * batch + b, 0)),
                 vec_spec, vec_spec, vec_spec, vec_spec,
                 pl.BlockSpec((None, 1, DV_B), lambda b, pt: (layer, 0, 0))]
                + [page_spec(j, N_KVA) for j in range(n_pages)]
                + [page_spec(j, N_KVB) for j in range(n_pages)])
    grid_spec = pltpu.PrefetchScalarGridSpec(
        num_scalar_prefetch=1,
        grid=(batch,),
        in_specs=in_specs,
        out_specs=pl.BlockSpec((t_new, D_MIX), lambda b, pt: (b, 0)),
        scratch_shapes=[pltpu.VMEM((H_A // HKV_A * t_new, width), F32),
                        pltpu.VMEM((N_KVB * t_new, width), F32)],
    )
    return pl.pallas_call(
        functools.partial(_sample_attn_kernel, n_pages=n_pages, lam_init=lam_init),
        grid_spec=grid_spec,
        out_shape=jax.ShapeDtypeStruct((batch * t_new, D_MIX), F32),
        compiler_params=_cparams(("parallel",)),
        name="sample_attention",
    )(page_table, qa, kva_new, bias, qb, kvb_new, qm, pool_mem, *lam_params, g,
      *([pool_kva] * n_pages), *([pool_kvb] * n_pages))


def _attn_out_kernel(x_ref, mix_ref, wo_ref, g_ref, b_ref, h_ref, *, alpha):
    a = _mm(mix_ref[...].astype(MXU_DTYPE), wo_ref[...])
    h_ref[...] = _layer_norm(alpha * x_ref[...] + a, g_ref[...], b_ref[...])


def _attn_out(x2d, mixed, w_o, ln_g, ln_b, layer, alpha, tm):
    m, d = x2d.shape
    row = pl.BlockSpec((tm, d), lambda i: (i, 0))
    vec = pl.BlockSpec((None, 1, d), lambda i: (layer, 0, 0))
    return pl.pallas_call(
        functools.partial(_attn_out_kernel, alpha=alpha),
        grid=(m // tm,),
        in_specs=[row, pl.BlockSpec((tm, mixed.shape[1]), lambda i: (i, 0)),
                  pl.BlockSpec((None, mixed.shape[1], d), lambda i: (layer, 0, 0), pipeline_mode=pl.Buffered(1)),
                  vec, vec],
        out_specs=row,
        out_shape=jax.ShapeDtypeStruct((m, d), F32),
        compiler_params=_cparams(("parallel",)),
        name="attn_out_ln",
    )(x2d, mixed, w_o, ln_g, ln_b)


def _ffn_kernel(h_ref, wg_ref, wu_ref, wd_ref, g_ref, b_ref, o_ref, hb_ref, acc_ref, *, alpha):
    j = pl.program_id(1)

    @pl.when(j == 0)
    def _():
        hb_ref[...] = h_ref[...].astype(MXU_DTYPE)
        acc_ref[...] = jnp.zeros_like(acc_ref)

    hb = hb_ref[...]
    gate = _mm(hb, wg_ref[...])
    up = _mm(hb, wu_ref[...])
    act = gate * jax.nn.sigmoid(gate) * up
    acc_ref[...] += _mm(act.astype(MXU_DTYPE), wd_ref[...])

    @pl.when(j == pl.num_programs(1) - 1)
    def _():
        o_ref[...] = _layer_norm(alpha * h_ref[...] + acc_ref[...], g_ref[...], b_ref[...])


def _ffn(h2d, w_gate, w_up, w_down, ln_g, ln_b, layer, alpha, tm, tf):
    m, d = h2d.shape
    f = w_gate.shape[-1]
    row = pl.BlockSpec((tm, d), lambda i, j: (i, 0))
    vec = pl.BlockSpec((None, 1, d), lambda i, j: (layer, 0, 0))
    return pl.pallas_call(
        functools.partial(_ffn_kernel, alpha=alpha),
        grid=(m // tm, f // tf),
        in_specs=[row,
                  pl.BlockSpec((None, d, tf), lambda i, j: (layer, 0, j)),
                  pl.BlockSpec((None, d, tf), lambda i, j: (layer, 0, j)),
                  pl.BlockSpec((None, tf, d), lambda i, j: (layer, j, 0)),
                  vec, vec],
        out_specs=row,
        out_shape=jax.ShapeDtypeStruct((m, d), F32),
        scratch_shapes=[pltpu.VMEM((tm, d), MXU_DTYPE), pltpu.VMEM((tm, d), F32)],
        compiler_params=_cparams(("parallel", "arbitrary")),
        name="swiglu_ln",
    )(h2d, w_gate, w_up, w_down, ln_g, ln_b)


def _row_tile(m, cap):
    t = min(m, cap)
    while m % t:
        t //= 2
    return t


def _ff_tile(f, cap):
    best = LANES
    for t in range(LANES, cap + 1, LANES):
        if f % t == 0:
            best = t
    return best


def _largest_divisor(n, cap):
    return max(d for d in range(1, cap + 1) if n % d == 0)


def kernel(x_prompt, x_sample, mem_prompt, cache_kv_a, cache_idx_k, cache_kv_b, cache_mem_kv, page_table,
           w_in, w_mem_kv, lambda_q1, lambda_k1, lambda_q2, lambda_k2, subln_g, w_o,
           ln1_g, ln1_b, w_gate, w_up, w_down, ln2_g, ln2_b):
    depth, d_model, _ = w_in.shape
    batch, seq, _ = x_prompt.shape
    dec_batch, t_new, _ = x_sample.shape
    n_pages = page_table.shape[1]
    past = n_pages * PAGE_SIZE
    n_mem = mem_prompt.shape[1]
    n_phys = cache_idx_k.shape[1]
    alpha = (2.0 * depth) ** 0.25

    n_main = W_QA + W_KVA + W_QB + W_KVB
    w_in_p = jnp.concatenate(
        [w_in[:, :, :n_main], w_in[:, :, n_main + W_QM:n_main + W_QM + W_QI], w_in[:, :, n_main:n_main + W_QM],
         w_in[:, :, n_main + W_QM + W_QI:],
         jnp.zeros((depth, d_model, N_IN_PAD - w_in.shape[2]), w_in.dtype)], axis=-1).astype(MXU_DTYPE)
    w_mem_c = w_mem_kv.astype(MXU_DTYPE)
    w_o_c, w_gate_c, w_up_c, w_down_c = (w.astype(MXU_DTYPE) for w in (w_o, w_gate, w_up, w_down))
    vec3 = lambda a: a.astype(F32).reshape(depth, 1, a.shape[-1])
    lam_params = tuple(vec3(a) for a in (lambda_q1, lambda_k1, lambda_q2, lambda_k2))
    g_sub = vec3(subln_g)
    ln1g, ln1b, ln2g, ln2b = (vec3(a) for a in (ln1_g, ln1_b, ln2_g, ln2_b))

    pool_kva = cache_kv_a.reshape(-1, LANES)
    pool_idx = jnp.swapaxes(cache_idx_k, 2, 3).reshape(depth * n_phys, D_IDX, PAGE_SIZE)
    pool_kvb = cache_kv_b.reshape(-1, LANES)
    pool_mem = cache_mem_kv.reshape(-1, LANES)

    tm_p = _row_tile(seq, 512)
    m_s = dec_batch * t_new
    tm_s = _row_tile(m_s, 512)
    pos_p = jnp.arange(seq, dtype=I32)
    pos_s = past + (jnp.arange(tm_s, dtype=I32) % t_new)
    tabs_p = _rope_tables(pos_p, DH_A) + _rope_tables(pos_p, DQK_B)
    tabs_s = _rope_tables(pos_s, DH_A) + _rope_tables(pos_s, DQK_B)

    xp = x_prompt.reshape(batch * seq, d_model)
    xs = x_sample.reshape(m_s, d_model)
    mem2d = mem_prompt.reshape(batch * n_mem, d_model)
    tf = _ff_tile(w_gate.shape[-1], 512)
    topk_s = min(INDEX_TOPK_MAX, (past + t_new) // 4)
    idx_group = _largest_divisor(dec_batch, 4)

    outs = [[] for _ in range(7)]
    for l in range(depth):
        lam_init = 0.8 - 0.6 * math.exp(-0.3 * l)

        qa, kva, qb, kvb, qi, qm, ki, wi = _project(xp, w_in_p, l, tabs_p, seq // tm_p, tm_p)
        mem_kv = _matmul(mem2d, w_mem_c, l, _row_tile(batch * n_mem, 512))
        mixed = _prompt_attention(qa, kva, qi, wi.T, ki, qb, kvb, qm, mem_kv, lam_params, g_sub,
                                  batch, seq, l, lam_init)
        h = _attn_out(xp, mixed, w_o_c, ln1g, ln1b, l, alpha, tm_p)
        xp = _ffn(h, w_gate_c, w_up_c, w_down_c, ln2g, ln2b, l, alpha, tm_p, tf)
        outs[0].append(kva.reshape(batch, seq, 2, HKV_A, DH_A))
        outs[1].append(ki.reshape(batch, seq, D_IDX))
        outs[2].append(kvb.reshape(batch, seq, 2, H_B, DV_B))
        outs[3].append(mem_kv.reshape(batch, n_mem, 2, H_MEM, DH_MEM))

        qa, kva, qb, kvb, qi, qm, ki, wi = _project(xs, w_in_p, l, tabs_s, 1, tm_s)
        wcol = wi.reshape(dec_batch, t_new, H_IDX).transpose(0, 2, 1).reshape(dec_batch * H_IDX * t_new, 1)
        keys = _sample_index(page_table, qi, wcol, ki, pool_idx, dec_batch, t_new, l, n_phys, idx_group)
        bias = _sample_select(keys, topk_s, _row_tile(m_s, 128))
        mixed = _sample_attention(page_table, qa, kva, bias, qb, kvb, qm, pool_kva, pool_kvb, pool_mem,
                                  lam_params, g_sub, dec_batch, t_new, l, lam_init, n_phys, n_mem * N_KVM)
        h = _attn_out(xs, mixed, w_o_c, ln1g, ln1b, l, alpha, tm_s)
        xs = _ffn(h, w_gate_c, w_up_c, w_down_c, ln2g, ln2b, l, alpha, tm_s, tf)
        outs[4].append(kva.reshape(dec_batch, t_new, 2, HKV_A, DH_A))
        outs[5].append(ki.reshape(dec_batch, t_new, D_IDX))
        outs[6].append(kvb.reshape(dec_batch, t_new, 2, H_B, DV_B))

    return (xp.reshape(batch, seq, d_model), xs.reshape(dec_batch, t_new, d_model)) + tuple(
        jnp.stack(o) for o in outs)
```

```python
import functools
import math

import jax
import jax.numpy as jnp
from jax import lax
from jax.experimental import pallas as pl
from jax.experimental.pallas import tpu as pltpu

F32 = jnp.float32
I32 = jnp.int32
MXU_DTYPE = jnp.bfloat16

H_A, DH_A, HKV_A = 8, 128, 2
H_IDX, D_IDX = 16, 64
INDEX_TOPK_MAX = 256
H_B, DQK_B, DV_B = 4, 64, 128
H_MEM, DH_MEM = 4, 128
PAGE_SIZE = 128
ROPE_THETA = 500000.0
ROPE_DIV = 4
LN_EPS = 1e-5
RMS_EPS = 1e-5
D_MIX = H_A * DH_A + H_B * DV_B + H_MEM * DH_MEM
N_KVA, N_KVB, N_KVM = 2 * HKV_A, 2 * H_B, 2 * H_MEM

LANES = 128
VMEM_LIMIT_BYTES = 56 * 1024 * 1024

W_QA, W_KVA, W_QB, W_KVB, W_QI, W_QM = 1024, 512, 512, 1024, 1024, 512
C_QA, C_KVA, C_QB, C_KVB, C_QI, C_QM, C_TAIL = 0, 8, 12, 16, 24, 32, 36
N_IN_PAD = 37 * LANES

LOG2_E = math.log2(math.e)
NEG_INF = float("-inf")
NEG_BIG = -1e30
INT_MIN = -(2 ** 31)
NEG_INF_KEY = -2139095041


def _cparams(sem):
    return pltpu.CompilerParams(dimension_semantics=sem, vmem_limit_bytes=VMEM_LIMIT_BYTES)


def _mm(a, b):
    return jnp.dot(a, b, preferred_element_type=F32)


def _mm_nt(a, b):
    return lax.dot_general(a, b, (((1,), (1,)), ((), ())), preferred_element_type=F32)


def _layer_norm(y, g, b):
    mu = jnp.mean(y, axis=-1, keepdims=True)
    d = y - mu
    var = jnp.mean(d * d, axis=-1, keepdims=True)
    return d * lax.rsqrt(var + LN_EPS) * g + b


def _exp_and_sum(s, axis=-1):
    m = jnp.max(s, axis=axis, keepdims=True)
    e = jnp.exp(s - m)
    return e, jnp.sum(e, axis=axis, keepdims=True)


def _ordered_key(x):
    k = pltpu.bitcast(x, I32)
    return jnp.where(k < 0, k ^ jnp.int32(0x7FFFFFFF), k)


def _half_lanes(x, c):
    lane = lax.broadcasted_iota(I32, x.shape, 1)
    return jnp.where((lane >= c * DQK_B) & (lane < (c + 1) * DQK_B), x, 0.0)


def _lambda_value(lq1, lk1, lq2, lk2, lam_init):
    a = jnp.sum(lq1[...] * lk1[...], axis=-1, keepdims=True)
    b = jnp.sum(lq2[...] * lk2[...], axis=-1, keepdims=True)
    return jnp.exp(a) - jnp.exp(b) + lam_init


def _diff_finish(o, g_ref, lam_init):
    o = o * lax.rsqrt(jnp.mean(o * o, axis=-1, keepdims=True) + RMS_EPS)
    return o * g_ref[...] * (1.0 - lam_init)


def _rope_tables(pos, head_dim):
    rot = head_dim // ROPE_DIV
    half = rot // 2
    t = pos.shape[0]
    inv_freq = jnp.float32(ROPE_THETA) ** (-jnp.arange(half, dtype=F32) / half)
    ang = pos.astype(F32)[:, None] * inv_freq[None, :]
    cos, sin = jnp.cos(ang), jnp.sin(ang)
    c = jnp.concatenate([cos, cos, jnp.ones((t, head_dim - rot), F32)], axis=-1)
    s_hi = jnp.concatenate([-sin, jnp.zeros((t, head_dim - half), F32)], axis=-1)
    s_lo = jnp.concatenate([jnp.zeros((t, half), F32), sin, jnp.zeros((t, head_dim - rot), F32)], axis=-1)
    rep = LANES // head_dim
    return tuple(jnp.tile(a, (1, rep)) for a in (c, s_hi, s_lo))


def _proj_kernel(x_ref, w_ref, ca, sa1, sa2, cb, sb1, sb2,
                 qa_o, kva_o, qb_o, kvb_o, qi_o, qm_o, ki_o, wi_o):
    xb = x_ref[...].astype(MXU_DTYPE)
    tm = x_ref.shape[0]

    def rope(z, c, s_hi, s_lo, half):
        return z * c[...] + pltpu.roll(z, LANES - half, 1) * s_hi[...] + pltpu.roll(z, half, 1) * s_lo[...]

    def rope_a(z):
        return rope(z, ca, sa1, sa2, DH_A // ROPE_DIV // 2)

    def rope_b(z):
        return rope(z, cb, sb1, sb2, DQK_B // ROPE_DIV // 2)

    def emit(out_ref, c0, kinds, interleave=False):
        n = len(kinds)
        z = _mm(xb, w_ref[:, c0 * LANES:(c0 + n) * LANES])
        for j, kind in enumerate(kinds):
            zj = z[:, j * LANES:(j + 1) * LANES]
            if kind == "a":
                zj = rope_a(zj)
            elif kind == "b":
                zj = rope_b(zj)
            if interleave:
                out_ref[pl.ds(j, tm, stride=n), :] = zj
            else:
                out_ref[:, j * LANES:(j + 1) * LANES] = zj

    emit(qa_o, C_QA, "a" * 8)
    emit(kva_o, C_KVA, "aa--", interleave=True)
    emit(qb_o, C_QB, "bbbb")
    emit(kvb_o, C_KVB, "bbbb----", interleave=True)
    emit(qi_o, C_QI, "b" * 8)
    emit(qm_o, C_QM, "----")
    z = _mm(xb, w_ref[:, C_TAIL * LANES:(C_TAIL + 1) * LANES])
    lane = lax.broadcasted_iota(I32, z.shape, 1)
    zt = jnp.where(lane < D_IDX, rope_b(z), z)
    ki_o[...] = zt[:, :D_IDX]
    wi_o[...] = zt[:, D_IDX:D_IDX + H_IDX]


CACHE_OUTPUTS = (1, 3, 6)


def _proj_kernel_with_prev(*refs):
    _proj_kernel(*refs[:8], *refs[8 + len(CACHE_OUTPUTS):])


def _project(x2d, w_in_p, layer, tabs, n_tab_blocks, tm, cache_slots=1, prev_caches=None):
    m, d = x2d.shape
    n_blocks = m // tm
    shapes = ((1, W_QA), (N_KVA, LANES), (1, W_QB), (N_KVB, LANES), (1, W_QI), (1, W_QM), (1, D_IDX), (1, H_IDX))
    tab_spec = pl.BlockSpec((tm, LANES), lambda i: (i % n_tab_blocks, 0))
    slot = layer if cache_slots > 1 else 0
    out_specs, out_shape = [], []
    for k, (r, w) in enumerate(shapes):
        slots, base = (cache_slots, slot * n_blocks) if k in CACHE_OUTPUTS else (1, 0)
        out_specs.append(pl.BlockSpec((tm * r, w), lambda i, base=base: (base + i, 0)))
        out_shape.append(jax.ShapeDtypeStruct((slots * m * r, w), F32))
    in_specs = ([pl.BlockSpec((tm, d), lambda i: (i, 0)),
                 pl.BlockSpec((None, d, N_IN_PAD), lambda i: (layer, 0, 0), pipeline_mode=pl.Buffered(1))]
                + [tab_spec] * 6)
    args = [x2d, w_in_p, *tabs]
    aliases = {}
    if prev_caches is not None:
        aliases = {len(args) + n: k for n, k in enumerate(CACHE_OUTPUTS)}
        in_specs += [pl.BlockSpec(memory_space=pl.ANY)] * len(CACHE_OUTPUTS)
        args += list(prev_caches)
    return pl.pallas_call(
        _proj_kernel if prev_caches is None else _proj_kernel_with_prev,
        grid=(n_blocks,),
        in_specs=in_specs,
        out_specs=out_specs,
        out_shape=out_shape,
        input_output_aliases=aliases,
        compiler_params=_cparams(("parallel",)),
        name="proj_rope",
    )(*args)


def _matmul_kernel(x_ref, w_ref, *rest):
    o_ref = rest[-1]
    tm = x_ref.shape[0]
    n = w_ref.shape[1] // LANES
    z = _mm(x_ref[...].astype(MXU_DTYPE), w_ref[...])
    for j in range(n):
        o_ref[pl.ds(j, tm, stride=n), :] = z[:, j * LANES:(j + 1) * LANES]


def _matmul(x2d, w, layer, tm, prev=None):
    m, d = x2d.shape
    depth, _, n = w.shape
    n_blocks = m // tm
    in_specs = [pl.BlockSpec((tm, d), lambda i: (i, 0)),
                pl.BlockSpec((None, d, n), lambda i: (layer, 0, 0), pipeline_mode=pl.Buffered(1))]
    args = [x2d, w]
    aliases = {}
    if prev is not None:
        in_specs.append(pl.BlockSpec(memory_space=pl.ANY))
        aliases = {len(args): 0}
        args.append(prev)
    return pl.pallas_call(
        _matmul_kernel,
        grid=(n_blocks,),
        in_specs=in_specs,
        out_specs=pl.BlockSpec((tm * n // LANES, LANES), lambda i: (layer * n_blocks + i, 0)),
        out_shape=jax.ShapeDtypeStruct((depth * m * n // LANES, LANES), F32),
        input_output_aliases=aliases,
        compiler_params=_cparams(("parallel",)),
        name="mem_kv_proj",
    )(*args)


def _kth_threshold(key_ref, rows, n_chunks, topk):
    ones = jnp.ones((LANES, LANES), MXU_DTYPE)

    def count_ge(cand):
        acc = jnp.zeros((rows, LANES), F32)
        for c in range(n_chunks):
            acc = acc + jnp.where(key_ref[:, c * LANES:(c + 1) * LANES] >= cand, 1.0, 0.0)
        return _mm(acc.astype(MXU_DTYPE), ones)

    def body(it, t):
        cand = t + lax.shift_left(jnp.int32(1), jnp.int32(31) - it)
        return jnp.where(count_ge(cand) >= topk, cand, t)

    t = lax.fori_loop(0, 32, body, jnp.full((rows, LANES), INT_MIN, I32))
    return t, count_ge(t)


def _selection_bias(key_ref, bias_ref, t, n_ge, rows, n_chunks, topk):
    floor_t = jnp.maximum(t, jnp.int32(NEG_INF_KEY + 1))
    tie_rows = jnp.where((n_ge > topk) & (t > NEG_INF_KEY), 1.0, 0.0)
    has_tie = jnp.max(tie_rows) > 0.5

    @pl.when(jnp.logical_not(has_tie))
    def _():
        for c in range(n_chunks):
            sl = slice(c * LANES, (c + 1) * LANES)
            bias_ref[:, sl] = jnp.where(key_ref[:, sl] >= floor_t, 0.0, NEG_INF)

    @pl.when(has_tie)
    def _():
        ones = jnp.ones((LANES, LANES), MXU_DTYPE)
        r_i = lax.broadcasted_iota(I32, (LANES, LANES), 0)
        c_i = lax.broadcasted_iota(I32, (LANES, LANES), 1)
        strict_upper = jnp.where(r_i < c_i, 1.0, 0.0).astype(MXU_DTYPE)
        n_gt = jnp.zeros((rows, LANES), F32)
        for c in range(n_chunks):
            sl = slice(c * LANES, (c + 1) * LANES)
            n_gt = n_gt + _mm(jnp.where(key_ref[:, sl] > t, 1.0, 0.0).astype(MXU_DTYPE), ones)
        need = topk - n_gt
        run = jnp.zeros((rows, LANES), F32)
        for c in range(n_chunks):
            sl = slice(c * LANES, (c + 1) * LANES)
            k = key_ref[:, sl]
            eq = jnp.where(k == t, 1.0, 0.0).astype(MXU_DTYPE)
            before = run + _mm(eq, strict_upper)
            run = run + _mm(eq, ones)
            keep = (k > t) | ((k == t) & (before < need))
            bias_ref[:, sl] = jnp.where(keep & (k > NEG_INF_KEY), 0.0, NEG_INF)


def _count_rows(pred_fn, key_ref, n_keys):
    acc = jnp.zeros((LANES, LANES), F32)
    for c in range(n_keys // LANES):
        acc = acc + jnp.where(pred_fn(key_ref[c * LANES:(c + 1) * LANES, :]), 1.0, 0.0)
    return jnp.sum(acc, axis=0, keepdims=True)


def _kth_threshold_t(key_ref, n_keys, topk):
    def body(it, t):
        cand = t + lax.shift_left(jnp.int32(1), jnp.int32(31) - it)
        return jnp.where(_count_rows(lambda k: k >= cand, key_ref, n_keys) >= topk, cand, t)

    t = lax.fori_loop(0, 32, body, jnp.full((1, LANES), INT_MIN, I32))
    return t, _count_rows(lambda k: k >= t, key_ref, n_keys)


def _selection_bias_t(key_ref, store_bias, t, n_ge, n_keys, topk):
    floor_t = jnp.maximum(t, jnp.int32(NEG_INF_KEY + 1))
    tie_lanes = jnp.where((n_ge > topk) & (t > NEG_INF_KEY), 1.0, 0.0)
    has_tie = jnp.max(tie_lanes) > 0.5

    @pl.when(jnp.logical_not(has_tie))
    def _():
        for c in range(n_keys // LANES):
            sl = slice(c * LANES, (c + 1) * LANES)
            store_bias(sl, jnp.where(key_ref[sl, :] >= floor_t, 0.0, NEG_BIG))

    @pl.when(has_tie)
    def _():
        r_i = lax.broadcasted_iota(I32, (LANES, LANES), 0)
        c_i = lax.broadcasted_iota(I32, (LANES, LANES), 1)
        strict_lower = jnp.where(c_i < r_i, 1.0, 0.0).astype(MXU_DTYPE)
        need = topk - _count_rows(lambda k: k > t, key_ref, n_keys)
        run = jnp.zeros((1, LANES), F32)
        for c in range(n_keys // LANES):
            sl = slice(c * LANES, (c + 1) * LANES)
            k = key_ref[sl, :]
            eq = jnp.where(k == t, 1.0, 0.0)
            before = run + _mm(strict_lower, eq.astype(MXU_DTYPE))
            run = run + jnp.sum(eq, axis=0, keepdims=True)
            keep = (k > t) | ((k == t) & (before < need))
            store_bias(sl, jnp.where(keep & (k > NEG_INF_KEY), 0.0, NEG_BIG))


Q_BLOCK = 128
N_KV_CLASSES = 8


def _prompt_attn_kernel(*refs, topk, lam_init, q0, has_prev):
    (qa_ref, kva_ref, qi_ref, wit_ref, ki_ref, qb_ref, kvb_ref, qm_ref, mem_ref,
     lq1, lk1, lq2, lk2, g_ref) = refs[:14]
    rest = refs[14 + (1 if has_prev else 0):]
    o_ref, kab_s, vat_s, kbb_s, vbt_s, ki_s, km_s, vmt_s, key_ref = rest
    n_keys = ki_ref.shape[0]
    n_mem = km_s.shape[1]
    qi = pl.program_id(1)
    key_lanes = slice(0, LANES)
    mask_lanes = slice(LANES, 2 * LANES)

    @pl.when(qi == 0)
    def _():
        for g in range(HKV_A):
            kab_s[g, :, key_lanes] = kva_ref[pl.ds(g, n_keys, stride=N_KVA), :].astype(MXU_DTYPE)
            vat_s[g] = kva_ref[pl.ds(HKV_A + g, n_keys, stride=N_KVA), :].T.astype(MXU_DTYPE)
        for h in range(H_B):
            kbb_s[h, :, key_lanes] = kvb_ref[pl.ds(h, n_keys, stride=N_KVB), :].astype(MXU_DTYPE)
            vbt_s[h] = kvb_ref[pl.ds(H_B + h, n_keys, stride=N_KVB), :].T.astype(MXU_DTYPE)
        for h in range(H_MEM):
            km_s[h] = mem_ref[pl.ds(h, n_mem, stride=N_KVM), :].astype(MXU_DTYPE)
            vmt_s[h] = mem_ref[pl.ds(H_MEM + h, n_mem, stride=N_KVM), :].T.astype(MXU_DTYPE)
        ki_s[...] = ki_ref[...].astype(MXU_DTYPE)

    q_pos = (q0 + qi) * Q_BLOCK + lax.broadcasted_iota(I32, (n_keys, Q_BLOCK), 1)
    k_pos = lax.broadcasted_iota(I32, (n_keys, Q_BLOCK), 0)
    causal = k_pos <= q_pos

    ki = ki_s[...]
    sc = jnp.zeros((n_keys, Q_BLOCK), F32)
    for hp in range(H_IDX // 2):
        q2 = jnp.concatenate([qi_ref[:, (2 * hp + j) * D_IDX:(2 * hp + j + 1) * D_IDX] for j in range(2)],
                             axis=0).astype(MXU_DTYPE)
        s = _mm_nt(ki, q2)
        for j in range(2):
            h = 2 * hp + j
            sc = sc + wit_ref[h:h + 1, :] * jnp.maximum(s[:, j * Q_BLOCK:(j + 1) * Q_BLOCK], 0.0)
    key_ref[...] = _ordered_key(jnp.where(causal, sc, NEG_INF))
    t, n_ge = _kth_threshold_t(key_ref, n_keys, topk)

    def store_selection(rows, block):
        for g in range(HKV_A):
            kab_s[g, rows, mask_lanes] = block.astype(MXU_DTYPE)

    _selection_bias_t(key_ref, store_selection, t, n_ge, n_keys, topk)

    eye = jnp.where(lax.broadcasted_iota(I32, (Q_BLOCK, Q_BLOCK), 0) == lax.broadcasted_iota(I32, (Q_BLOCK, Q_BLOCK), 1),
                    1.0, 0.0)

    def with_mask_selector(q_blocks, log2_scale):
        q = jnp.concatenate([qb_ * log2_scale for qb_ in q_blocks], axis=0)
        return jnp.concatenate([q, jnp.concatenate([eye] * len(q_blocks), axis=0)], axis=1).astype(MXU_DTYPE)

    def softmax_cols(s, n_cols):
        es, ls = [], []
        for r in range(n_cols):
            x = s[:, r * Q_BLOCK:(r + 1) * Q_BLOCK]
            e = jnp.exp2(x - jnp.max(x, axis=0, keepdims=True))
            es.append(e.astype(MXU_DTYPE))
            ls.append(jnp.sum(e, axis=0, keepdims=True))
        return jnp.concatenate(es, axis=1), ls

    rep = H_A // HKV_A
    for g in range(HKV_A):
        q4 = with_mask_selector([qa_ref[:, (g * rep + r) * DH_A:(g * rep + r + 1) * DH_A] for r in range(rep)],
                                DH_A ** -0.5 * LOG2_E)
        e, ls = softmax_cols(_mm_nt(kab_s[g], q4), rep)
        ot = _mm(vat_s[g], e)
        for r in range(rep):
            h = g * rep + r
            o_ref[:, h * DH_A:(h + 1) * DH_A] = (ot[:, r * Q_BLOCK:(r + 1) * Q_BLOCK] / ls[r]).T

    lam = _lambda_value(lq1, lk1, lq2, lk2, lam_init)
    causal_mask = jnp.where(causal, 0.0, NEG_BIG).astype(MXU_DTYPE)
    off_b = H_A * DH_A
    for h in range(H_B):
        kbb_s[h, :, mask_lanes] = causal_mask
        qh = qb_ref[:, h * DV_B:(h + 1) * DV_B]
        q2 = with_mask_selector([_half_lanes(qh, 0), _half_lanes(qh, 1)], DQK_B ** -0.5 * LOG2_E)
        e, ls = softmax_cols(_mm_nt(kbb_s[h], q2), 2)
        ot = _mm(vbt_s[h], e)
        o = (ot[:, :Q_BLOCK] / ls[0] - lam * (ot[:, Q_BLOCK:] / ls[1])).T
        o_ref[:, off_b + h * DV_B:off_b + (h + 1) * DV_B] = _diff_finish(o, g_ref, lam_init)

    off_m = off_b + H_B * DV_B
    for h in range(H_MEM):
        q = (qm_ref[:, h * DH_MEM:(h + 1) * DH_MEM] * (DH_MEM ** -0.5 * LOG2_E)).astype(MXU_DTYPE)
        e, ls = softmax_cols(_mm_nt(km_s[h], q), 1)
        o_ref[:, off_m + h * DH_MEM:off_m + (h + 1) * DH_MEM] = (_mm(vmt_s[h], e) / ls[0]).T


def _prompt_attention(qa, kva, qi, wit, ki, qb, kvb, qm, mem_kv, lam_params, g, batch, seq, layer, lam_init):
    depth = lam_params[0].shape[0]
    n_q = seq // Q_BLOCK
    n_classes = min(N_KV_CLASSES, n_q)
    q_per_class = n_q // n_classes
    n_mem = mem_kv.shape[0] // (depth * batch * N_KVM)
    topk = min(INDEX_TOPK_MAX, seq // 4)
    kva3 = kva.reshape(depth * batch, seq * N_KVA, LANES)
    kvb3 = kvb.reshape(depth * batch, seq * N_KVB, LANES)
    ki3 = ki.reshape(depth * batch, seq, D_IDX)
    mem3 = mem_kv.reshape(depth * batch, n_mem * N_KVM, LANES)
    vec_spec = pl.BlockSpec((None, 1, DQK_B), lambda b, i: (layer, 0, 0))

    def kv_spec(rows, w):
        return pl.BlockSpec((None, rows, w), lambda b, i: (layer * batch + b, 0, 0), pipeline_mode=pl.Buffered(1))

    mixed = None
    for c in range(n_classes):
        q0 = c * q_per_class
        n_keys = (c + 1) * q_per_class * Q_BLOCK

        def q_spec(w, q0=q0):
            return pl.BlockSpec((Q_BLOCK, w), lambda b, i: (b * n_q + q0 + i, 0))

        in_specs = [q_spec(W_QA), kv_spec(n_keys * N_KVA, LANES), q_spec(W_QI),
                    pl.BlockSpec((H_IDX, Q_BLOCK), lambda b, i, q0=q0: (0, b * n_q + q0 + i)),
                    kv_spec(n_keys, D_IDX), q_spec(W_QB), kv_spec(n_keys * N_KVB, LANES), q_spec(W_QM),
                    kv_spec(n_mem * N_KVM, LANES), vec_spec, vec_spec, vec_spec, vec_spec,
                    pl.BlockSpec((None, 1, DV_B), lambda b, i: (layer, 0, 0))]
        args = [qa, kva3, qi, wit, ki3, qb, kvb3, qm, mem3, *lam_params, g]
        aliases = {}
        if mixed is not None:
            in_specs.append(pl.BlockSpec(memory_space=pl.ANY))
            aliases = {len(args): 0}
            args.append(mixed)
        kernel = functools.partial(_prompt_attn_kernel, topk=topk, lam_init=lam_init, q0=q0,
                                   has_prev=mixed is not None)
        mixed = pl.pallas_call(
            kernel,
            grid=(batch, q_per_class),
            in_specs=in_specs,
            out_specs=q_spec(D_MIX),
            out_shape=jax.ShapeDtypeStruct((batch * seq, D_MIX), F32),
            scratch_shapes=[pltpu.VMEM((HKV_A, n_keys, 2 * LANES), MXU_DTYPE),
                            pltpu.VMEM((HKV_A, DH_A, n_keys), MXU_DTYPE),
                            pltpu.VMEM((H_B, n_keys, 2 * LANES), MXU_DTYPE), pltpu.VMEM((H_B, DV_B, n_keys), MXU_DTYPE),
                            pltpu.VMEM((n_keys, D_IDX), MXU_DTYPE),
                            pltpu.VMEM((H_MEM, n_mem, DH_MEM), MXU_DTYPE),
                            pltpu.VMEM((H_MEM, DH_MEM, n_mem), MXU_DTYPE),
                            pltpu.VMEM((n_keys, Q_BLOCK), I32)],
            input_output_aliases=aliases,
            compiler_params=_cparams(("parallel", "arbitrary")),
            name=f"prompt_attention_kv{c}",
        )(*args)
    return mixed


def _pad_rows(x, rows):
    return jnp.concatenate([x, jnp.zeros((rows - x.shape[0], x.shape[1]), x.dtype)], axis=0)


def _sample_index_kernel(pt_ref, qi_ref, wcol_ref, kin_ref, *rest, n_pages, group):
    del pt_ref
    idx_pages = rest[:group * n_pages]
    key_o = rest[group * n_pages]
    t_new = qi_ref.shape[0] // group
    past = n_pages * PAGE_SIZE

    lane = lax.broadcasted_iota(I32, (t_new, LANES), 1)
    tok = lax.broadcasted_iota(I32, (t_new, LANES), 0)
    new_ok = lane <= tok

    for b in range(group):
        rows = slice(b * t_new, (b + 1) * t_new)
        q_ht = jnp.concatenate([qi_ref[rows, h * D_IDX:(h + 1) * D_IDX] for h in range(H_IDX)],
                               axis=0).astype(MXU_DTYPE)
        wcol = wcol_ref[b * H_IDX * t_new:(b + 1) * H_IDX * t_new, :]

        def index_scores(qk):
            s = jnp.maximum(qk, 0.0) * wcol
            acc = s[0:t_new]
            for h in range(1, H_IDX):
                acc = acc + s[h * t_new:(h + 1) * t_new]
            return acc

        for j in range(n_pages):
            qk = _mm(q_ht, idx_pages[b * n_pages + j][...].astype(MXU_DTYPE))
            key_o[rows, j * LANES:(j + 1) * LANES] = _ordered_key(index_scores(qk))
        s_new = index_scores(_mm_nt(q_ht, _pad_rows(kin_ref[rows, :], LANES).astype(MXU_DTYPE)))
        key_o[rows, past:past + LANES] = _ordered_key(jnp.where(new_ok, s_new, NEG_INF))


def _sample_index(page_table, qi, wcol, ki_new, pool_idx, batch, t_new, layer, n_phys, group):
    n_pages = page_table.shape[1]
    width = (n_pages + 1) * LANES

    def row_spec(rows, w):
        return pl.BlockSpec((rows, w), lambda s, pt: (s, 0))

    def page_spec(b, j):
        return pl.BlockSpec((None, D_IDX, PAGE_SIZE),
                            lambda s, pt: (layer * n_phys + pt[s * group + b, j], 0, 0))

    in_specs = ([row_spec(group * t_new, W_QI), row_spec(group * H_IDX * t_new, 1), row_spec(group * t_new, D_IDX)]
                + [page_spec(b, j) for b in range(group) for j in range(n_pages)])
    grid_spec = pltpu.PrefetchScalarGridSpec(
        num_scalar_prefetch=1, grid=(batch // group,), in_specs=in_specs,
        out_specs=pl.BlockSpec((group * t_new, width), lambda s, pt: (s, 0)))
    return pl.pallas_call(
        functools.partial(_sample_index_kernel, n_pages=n_pages, group=group),
        grid_spec=grid_spec,
        out_shape=jax.ShapeDtypeStruct((batch * t_new, width), I32),
        compiler_params=_cparams(("parallel",)),
        name="sample_indexer",
    )(page_table, qi, wcol, ki_new, *([pool_idx] * (group * n_pages)))


def _sample_select_kernel(key_ref, bias_ref, *, topk):
    rows, width = key_ref.shape
    n_chunks = width // LANES
    t, n_ge = _kth_threshold(key_ref, rows, n_chunks, topk)
    _selection_bias(key_ref, bias_ref, t, n_ge, rows, n_chunks, topk)


def _sample_select(keys, topk, tr):
    m, width = keys.shape
    spec = pl.BlockSpec((tr, width), lambda i: (i, 0))
    return pl.pallas_call(
        functools.partial(_sample_select_kernel, topk=topk),
        grid=(m // tr,),
        in_specs=[spec],
        out_specs=spec,
        out_shape=jax.ShapeDtypeStruct((m, width), F32),
        compiler_params=_cparams(("parallel",)),
        name="sample_topk_select",
    )(keys)


def _sample_attn_kernel(pt_ref, qa_ref, kvan_ref, bias_ref, qb_ref, kvbn_ref, qm_ref, mem_ref,
                        lq1, lk1, lq2, lk2, g_ref, *rest, n_pages, lam_init):
    del pt_ref
    kva_pages = rest[:n_pages]
    kvb_pages = rest[n_pages:2 * n_pages]
    o_ref, sa_ref, sb_ref = rest[2 * n_pages:]
    t_new = qa_ref.shape[0]
    past = n_pages * PAGE_SIZE

    lane = lax.broadcasted_iota(I32, (t_new, LANES), 1)
    tok = lax.broadcasted_iota(I32, (t_new, LANES), 0)
    new_ok = lane <= tok

    def chunk_rows(ref, j, rows, n):
        return ref[pl.ds(j, rows, stride=n), :].astype(MXU_DTYPE)

    def new_rows(ref, j, n):
        return _pad_rows(ref[pl.ds(j, t_new, stride=n), :], LANES).astype(MXU_DTYPE)

    scale_a = DH_A ** -0.5
    rep = H_A // HKV_A
    bias_a = jnp.concatenate([bias_ref[...]] * rep, axis=0)
    for g in range(HKV_A):
        q = jnp.concatenate([qa_ref[:, (g * rep + r) * DH_A:(g * rep + r + 1) * DH_A] for r in range(rep)],
                            axis=0).astype(MXU_DTYPE)
        for j in range(n_pages):
            sa_ref[:, j * LANES:(j + 1) * LANES] = _mm_nt(q, chunk_rows(kva_pages[j], g, PAGE_SIZE, N_KVA))
        sa_ref[:, past:past + LANES] = _mm_nt(q, new_rows(kvan_ref, g, N_KVA))
        e, l = _exp_and_sum(sa_ref[...] * scale_a + bias_a)
        e = e.astype(MXU_DTYPE)
        acc = _mm(e[:, past:past + LANES], new_rows(kvan_ref, HKV_A + g, N_KVA))
        for j in range(n_pages):
            acc = acc + _mm(e[:, j * LANES:(j + 1) * LANES], chunk_rows(kva_pages[j], HKV_A + g, PAGE_SIZE, N_KVA))
        acc = acc / l
        for r in range(rep):
            h = g * rep + r
            o_ref[:, h * DH_A:(h + 1) * DH_A] = acc[r * t_new:(r + 1) * t_new]

    lam = _lambda_value(lq1, lk1, lq2, lk2, lam_init)
    scale_b = DQK_B ** -0.5
    for h in range(H_B):
        qh = qb_ref[:, h * DV_B:(h + 1) * DV_B]
        q = jnp.concatenate([_half_lanes(qh, 0), _half_lanes(qh, 1)], axis=0).astype(MXU_DTYPE)
        rows = slice(2 * h * t_new, (2 * h + 2) * t_new)
        for j in range(n_pages):
            sb_ref[rows, j * LANES:(j + 1) * LANES] = _mm_nt(q, chunk_rows(kvb_pages[j], h, PAGE_SIZE, N_KVB))
        sb_ref[rows, past:past + LANES] = _mm_nt(q, new_rows(kvbn_ref, h, N_KVB))
    new_bias = jnp.where(new_ok, 0.0, NEG_INF)
    bias_b = jnp.concatenate(
        [jnp.zeros((N_KVB * t_new, past), F32), jnp.concatenate([new_bias] * N_KVB, axis=0)], axis=1)
    e, l = _exp_and_sum(sb_ref[...] * scale_b + bias_b)
    e = e.astype(MXU_DTYPE)
    off_b = H_A * DH_A
    for h in range(H_B):
        rows = slice(2 * h * t_new, (2 * h + 2) * t_new)
        acc = _mm(e[rows, past:past + LANES], new_rows(kvbn_ref, H_B + h, N_KVB))
        for j in range(n_pages):
            acc = acc + _mm(e[rows, j * LANES:(j + 1) * LANES], chunk_rows(kvb_pages[j], H_B + h, PAGE_SIZE, N_KVB))
        acc = acc / l[rows]
        o = acc[0:t_new] - lam * acc[t_new:2 * t_new]
        o_ref[:, off_b + h * DV_B:off_b + (h + 1) * DV_B] = _diff_finish(o, g_ref, lam_init)

    scale_m = DH_MEM ** -0.5
    off_m = off_b + H_B * DV_B
    n_mem = mem_ref.shape[0] // N_KVM
    for h in range(H_MEM):
        q = qm_ref[:, h * DH_MEM:(h + 1) * DH_MEM].astype(MXU_DTYPE)
        e, l = _exp_and_sum(_mm_nt(q, chunk_rows(mem_ref, h, n_mem, N_KVM)) * scale_m)
        o_ref[:, off_m + h * DH_MEM:off_m + (h + 1) * DH_MEM] = (
            _mm(e.astype(MXU_DTYPE), chunk_rows(mem_ref, H_MEM + h, n_mem, N_KVM)) / l)


def _sample_attention(page_table, qa, kva_new, bias, qb, kvb_new, qm, pool_kva, pool_kvb, pool_mem,
                      lam_params, g, batch, t_new, layer, lam_init, n_phys, mem_rows):
    n_pages = page_table.shape[1]
    width = (n_pages + 1) * LANES

    def row_spec(rows, w):
        return pl.BlockSpec((rows, w), lambda b, pt: (b, 0))

    def page_spec(j, n):
        return pl.BlockSpec((PAGE_SIZE * n, LANES), lambda b, pt: (layer * n_phys + pt[b, j], 0))

    vec_spec = pl.BlockSpec((None, 1, DQK_B), lambda b, pt: (layer, 0, 0))
    in_specs = ([row_spec(t_new, W_QA), row_spec(t_new * N_KVA, LANES), row_spec(t_new, width),
                 row_spec(t_new, W_QB), row_spec(t_new * N_KVB, LANES), row_spec(t_new, W_QM),
                 pl.BlockSpec((mem_rows, LANES), lambda b, pt: (layer * batch + b, 0)),
                 vec_spec, vec_spec, vec_spec, vec_spec,
                 pl.BlockSpec((None, 1, DV_B), lambda b, pt: (layer, 0, 0))]
                + [page_spec(j, N_KVA) for j in range(n_pages)]
                + [page_spec(j, N_KVB) for j in range(n_pages)])
    grid_spec = pltpu.PrefetchScalarGridSpec(
        num_scalar_prefetch=1,
        grid=(batch,),
        in_specs=in_specs,
        out_specs=pl.BlockSpec((t_new, D_MIX), lambda b, pt: (b, 0)),
        scratch_shapes=[pltpu.VMEM((H_A // HKV_A * t_new, width), F32),
                        pltpu.VMEM((N_KVB * t_new, width), F32)],
    )
    return pl.pallas_call(
        functools.partial(_sample_attn_kernel, n_pages=n_pages, lam_init=lam_init),
        grid_spec=grid_spec,
        out_shape=jax.ShapeDtypeStruct((batch * t_new, D_MIX), F32),
        compiler_params=_cparams(("parallel",)),
        name="sample_attention",
    )(page_table, qa, kva_new, bias, qb, kvb_new, qm, pool_mem, *lam_params, g,
      *([pool_kva] * n_pages), *([pool_kvb] * n_pages))


def _attn_out_kernel(x_ref, mix_ref, wo_ref, g_ref, b_ref, h_ref, *, alpha):
    a = _mm(mix_ref[...].astype(MXU_DTYPE), wo_ref[...])
    h_ref[...] = _layer_norm(alpha * x_ref[...] + a, g_ref[...], b_ref[...])


def _attn_out(x2d, mixed, w_o, ln_g, ln_b, layer, alpha, tm):
    m, d = x2d.shape
    row = pl.BlockSpec((tm, d), lambda i: (i, 0))
    vec = pl.BlockSpec((None, 1, d), lambda i: (layer, 0, 0))
    return pl.pallas_call(
        functools.partial(_attn_out_kernel, alpha=alpha),
        grid=(m // tm,),
        in_specs=[row, pl.BlockSpec((tm, mixed.shape[1]), lambda i: (i, 0)),
                  pl.BlockSpec((None, mixed.shape[1], d), lambda i: (layer, 0, 0), pipeline_mode=pl.Buffered(1)),
                  vec, vec],
        out_specs=row,
        out_shape=jax.ShapeDtypeStruct((m, d), F32),
        compiler_params=_cparams(("parallel",)),
        name="attn_out_ln",
    )(x2d, mixed, w_o, ln_g, ln_b)


def _ffn_kernel(h_ref, wg_ref, wu_ref, wd_ref, g_ref, b_ref, o_ref, hb_ref, acc_ref, *, alpha):
    j = pl.program_id(1)

    @pl.when(j == 0)
    def _():
        hb_ref[...] = h_ref[...].astype(MXU_DTYPE)
        acc_ref[...] = jnp.zeros_like(acc_ref)

    hb = hb_ref[...]
    gate = _mm(hb, wg_ref[...])
    up = _mm(hb, wu_ref[...])
    act = gate * jax.nn.sigmoid(gate) * up
    acc_ref[...] += _mm(act.astype(MXU_DTYPE), wd_ref[...])

    @pl.when(j == pl.num_programs(1) - 1)
    def _():
        o_ref[...] = _layer_norm(alpha * h_ref[...] + acc_ref[...], g_ref[...], b_ref[...])


def _ffn(h2d, w_gate, w_up, w_down, ln_g, ln_b, layer, alpha, tm, tf):
    m, d = h2d.shape
    f = w_gate.shape[-1]
    row = pl.BlockSpec((tm, d), lambda i, j: (i, 0))
    vec = pl.BlockSpec((None, 1, d), lambda i, j: (layer, 0, 0))
    return pl.pallas_call(
        functools.partial(_ffn_kernel, alpha=alpha),
        grid=(m // tm, f // tf),
        in_specs=[row,
                  pl.BlockSpec((None, d, tf), lambda i, j: (layer, 0, j)),
                  pl.BlockSpec((None, d, tf), lambda i, j: (layer, 0, j)),
                  pl.BlockSpec((None, tf, d), lambda i, j: (layer, j, 0)),
                  vec, vec],
        out_specs=row,
        out_shape=jax.ShapeDtypeStruct((m, d), F32),
        scratch_shapes=[pltpu.VMEM((tm, d), MXU_DTYPE), pltpu.VMEM((tm, d), F32)],
        compiler_params=_cparams(("parallel", "arbitrary")),
        name="swiglu_ln",
    )(h2d, w_gate, w_up, w_down, ln_g, ln_b)


def _row_tile(m, cap):
    t = min(m, cap)
    while m % t:
        t //= 2
    return t


def _ff_tile(f, cap):
    best = LANES
    for t in range(LANES, cap + 1, LANES):
        if f % t == 0:
            best = t
    return best


def _largest_divisor(n, cap):
    return max(d for d in range(1, cap + 1) if n % d == 0)


def kernel(x_prompt, x_sample, mem_prompt, cache_kv_a, cache_idx_k, cache_kv_b, cache_mem_kv, page_table,
           w_in, w_mem_kv, lambda_q1, lambda_k1, lambda_q2, lambda_k2, subln_g, w_o,
           ln1_g, ln1_b, w_gate, w_up, w_down, ln2_g, ln2_b):
    depth, d_model, _ = w_in.shape
    batch, seq, _ = x_prompt.shape
    dec_batch, t_new, _ = x_sample.shape
    n_pages = page_table.shape[1]
    past = n_pages * PAGE_SIZE
    n_mem = mem_prompt.shape[1]
    n_phys = cache_idx_k.shape[1]
    alpha = (2.0 * depth) ** 0.25

    n_main = W_QA + W_KVA + W_QB + W_KVB
    w_in_p = jnp.concatenate(
        [w_in[:, :, :n_main], w_in[:, :, n_main + W_QM:n_main + W_QM + W_QI], w_in[:, :, n_main:n_main + W_QM],
         w_in[:, :, n_main + W_QM + W_QI:],
         jnp.zeros((depth, d_model, N_IN_PAD - w_in.shape[2]), w_in.dtype)], axis=-1).astype(MXU_DTYPE)
    w_mem_c = w_mem_kv.astype(MXU_DTYPE)
    w_o_c, w_gate_c, w_up_c, w_down_c = (w.astype(MXU_DTYPE) for w in (w_o, w_gate, w_up, w_down))
    vec3 = lambda a: a.astype(F32).reshape(depth, 1, a.shape[-1])
    lam_params = tuple(vec3(a) for a in (lambda_q1, lambda_k1, lambda_q2, lambda_k2))
    g_sub = vec3(subln_g)
    ln1g, ln1b, ln2g, ln2b = (vec3(a) for a in (ln1_g, ln1_b, ln2_g, ln2_b))

    pool_kva = cache_kv_a.reshape(-1, LANES)
    pool_idx = jnp.swapaxes(cache_idx_k, 2, 3).reshape(depth * n_phys, D_IDX, PAGE_SIZE)
    pool_kvb = cache_kv_b.reshape(-1, LANES)
    pool_mem = cache_mem_kv.reshape(-1, LANES)

    tm_p = _row_tile(seq, 512)
    m_s = dec_batch * t_new
    tm_s = _row_tile(m_s, 512)
    pos_p = jnp.arange(seq, dtype=I32)
    pos_s = past + (jnp.arange(tm_s, dtype=I32) % t_new)
    tabs_p = _rope_tables(pos_p, DH_A) + _rope_tables(pos_p, DQK_B)
    tabs_s = _rope_tables(pos_s, DH_A) + _rope_tables(pos_s, DQK_B)

    xp = x_prompt.reshape(batch * seq, d_model)
    xs = x_sample.reshape(m_s, d_model)
    mem2d = mem_prompt.reshape(batch * n_mem, d_model)
    tf = _ff_tile(w_gate.shape[-1], 512)
    topk_s = min(INDEX_TOPK_MAX, (past + t_new) // 4)
    idx_group = _largest_divisor(dec_batch, 4)

    outs = [[] for _ in range(3)]
    caches, mem_kv = None, None
    for l in range(depth):
        lam_init = 0.8 - 0.6 * math.exp(-0.3 * l)

        qa, kva, qb, kvb, qi, qm, ki, wi = _project(xp, w_in_p, l, tabs_p, seq // tm_p, tm_p,
                                                    cache_slots=depth, prev_caches=caches)
        caches = (kva, kvb, ki)
        mem_kv = _matmul(mem2d, w_mem_c, l, _row_tile(batch * n_mem, 512), prev=mem_kv)
        mixed = _prompt_attention(qa, kva, qi, wi.T, ki, qb, kvb, qm, mem_kv, lam_params, g_sub,
                                  batch, seq, l, lam_init)
        h = _attn_out(xp, mixed, w_o_c, ln1g, ln1b, l, alpha, tm_p)
        xp = _ffn(h, w_gate_c, w_up_c, w_down_c, ln2g, ln2b, l, alpha, tm_p, tf)

        qa, kva, qb, kvb, qi, qm, ki, wi = _project(xs, w_in_p, l, tabs_s, 1, tm_s)
        wcol = wi.reshape(dec_batch, t_new, H_IDX).transpose(0, 2, 1).reshape(dec_batch * H_IDX * t_new, 1)
        keys = _sample_index(page_table, qi, wcol, ki, pool_idx, dec_batch, t_new, l, n_phys, idx_group)
        bias = _sample_select(keys, topk_s, _row_tile(m_s, 128))
        mixed = _sample_attention(page_table, qa, kva, bias, qb, kvb, qm, pool_kva, pool_kvb, pool_mem,
                                  lam_params, g_sub, dec_batch, t_new, l, lam_init, n_phys, n_mem * N_KVM)
        h = _attn_out(xs, mixed, w_o_c, ln1g, ln1b, l, alpha, tm_s)
        xs = _ffn(h, w_gate_c, w_up_c, w_down_c, ln2g, ln2b, l, alpha, tm_s, tf)
        outs[0].append(kva.reshape(dec_batch, t_new, 2, HKV_A, DH_A))
        outs[1].append(ki.reshape(dec_batch, t_new, D_IDX))
        outs[2].append(kvb.reshape(dec_batch, t_new, 2, H_B, DV_B))

    kva_p, kvb_p, ki_p = caches
    return (xp.reshape(batch, seq, d_model), xs.reshape(dec_batch, t_new, d_model),
            kva_p.reshape(depth, batch, seq, 2, HKV_A, DH_A), ki_p.reshape(depth, batch, seq, D_IDX),
            kvb_p.reshape(depth, batch, seq, 2, H_B, DV_B), mem_kv.reshape(depth, batch, n_mem, 2, H_MEM, DH_MEM)
            ) + tuple(jnp.stack(o) for o in outs)
```

```python
import functools
import math

import jax
import jax.numpy as jnp
from jax import lax
from jax.experimental import pallas as pl
from jax.experimental.pallas import tpu as pltpu

F32 = jnp.float32
I32 = jnp.int32
MXU_DTYPE = jnp.bfloat16

H_A, DH_A, HKV_A = 8, 128, 2
H_IDX, D_IDX = 16, 64
INDEX_TOPK_MAX = 256
H_B, DQK_B, DV_B = 4, 64, 128
H_MEM, DH_MEM = 4, 128
PAGE_SIZE = 128
ROPE_THETA = 500000.0
ROPE_DIV = 4
LN_EPS = 1e-5
RMS_EPS = 1e-5
D_MIX = H_A * DH_A + H_B * DV_B + H_MEM * DH_MEM
N_KVA, N_KVB, N_KVM = 2 * HKV_A, 2 * H_B, 2 * H_MEM

LANES = 128
VMEM_LIMIT_BYTES = 56 * 1024 * 1024

W_QA, W_KVA, W_QB, W_KVB, W_QI, W_QM = 1024, 512, 512, 1024, 1024, 512
C_QA, C_KVA, C_QB, C_KVB, C_QI, C_QM, C_TAIL = 0, 8, 12, 16, 24, 32, 36
N_IN_PAD = 37 * LANES

LOG2_E = math.log2(math.e)
NEG_INF = float("-inf")
NEG_BIG = -1e30
INT_MIN = -(2 ** 31)
NEG_INF_KEY = -2139095041


def _cparams(sem):
    return pltpu.CompilerParams(dimension_semantics=sem, vmem_limit_bytes=VMEM_LIMIT_BYTES)


def _mm(a, b):
    return jnp.dot(a, b, preferred_element_type=F32)


def _mm_nt(a, b):
    return lax.dot_general(a, b, (((1,), (1,)), ((), ())), preferred_element_type=F32)


def _layer_norm(y, g, b):
    mu = jnp.mean(y, axis=-1, keepdims=True)
    d = y - mu
    var = jnp.mean(d * d, axis=-1, keepdims=True)
    return d * lax.rsqrt(var + LN_EPS) * g + b


def _exp_and_sum(s, axis=-1):
    m = jnp.max(s, axis=axis, keepdims=True)
    e = jnp.exp(s - m)
    return e, jnp.sum(e, axis=axis, keepdims=True)


def _ordered_key(x):
    k = pltpu.bitcast(x, I32)
    return jnp.where(k < 0, k ^ jnp.int32(0x7FFFFFFF), k)


def _half_lanes(x, c):
    lane = lax.broadcasted_iota(I32, x.shape, 1)
    return jnp.where((lane >= c * DQK_B) & (lane < (c + 1) * DQK_B), x, 0.0)


def _lambda_value(lq1, lk1, lq2, lk2, lam_init):
    a = jnp.sum(lq1[...] * lk1[...], axis=-1, keepdims=True)
    b = jnp.sum(lq2[...] * lk2[...], axis=-1, keepdims=True)
    return jnp.exp(a) - jnp.exp(b) + lam_init


def _diff_finish(o, g_ref, lam_init):
    o = o * lax.rsqrt(jnp.mean(o * o, axis=-1, keepdims=True) + RMS_EPS)
    return o * g_ref[...] * (1.0 - lam_init)


def _rope_tables(pos, head_dim):
    rot = head_dim // ROPE_DIV
    half = rot // 2
    t = pos.shape[0]
    inv_freq = jnp.float32(ROPE_THETA) ** (-jnp.arange(half, dtype=F32) / half)
    ang = pos.astype(F32)[:, None] * inv_freq[None, :]
    cos, sin = jnp.cos(ang), jnp.sin(ang)
    c = jnp.concatenate([cos, cos, jnp.ones((t, head_dim - rot), F32)], axis=-1)
    s_hi = jnp.concatenate([-sin, jnp.zeros((t, head_dim - half), F32)], axis=-1)
    s_lo = jnp.concatenate([jnp.zeros((t, half), F32), sin, jnp.zeros((t, head_dim - rot), F32)], axis=-1)
    rep = LANES // head_dim
    return tuple(jnp.tile(a, (1, rep)) for a in (c, s_hi, s_lo))


def _proj_kernel(x_ref, w_ref, ca, sa1, sa2, cb, sb1, sb2,
                 qa_o, kva_o, qb_o, kvb_o, qi_o, qm_o, ki_o, wi_o):
    xb = x_ref[...].astype(MXU_DTYPE)
    tm = x_ref.shape[0]

    def rope(z, c, s_hi, s_lo, half):
        return z * c[...] + pltpu.roll(z, LANES - half, 1) * s_hi[...] + pltpu.roll(z, half, 1) * s_lo[...]

    def rope_a(z):
        return rope(z, ca, sa1, sa2, DH_A // ROPE_DIV // 2)

    def rope_b(z):
        return rope(z, cb, sb1, sb2, DQK_B // ROPE_DIV // 2)

    def emit(out_ref, c0, kinds, interleave=False):
        n = len(kinds)
        z = _mm(xb, w_ref[:, c0 * LANES:(c0 + n) * LANES])
        for j, kind in enumerate(kinds):
            zj = z[:, j * LANES:(j + 1) * LANES]
            if kind == "a":
                zj = rope_a(zj)
            elif kind == "b":
                zj = rope_b(zj)
            if interleave:
                out_ref[pl.ds(j, tm, stride=n), :] = zj
            else:
                out_ref[:, j * LANES:(j + 1) * LANES] = zj

    emit(qa_o, C_QA, "a" * 8)
    emit(kva_o, C_KVA, "aa--", interleave=True)
    emit(qb_o, C_QB, "bbbb")
    emit(kvb_o, C_KVB, "bbbb----", interleave=True)
    emit(qi_o, C_QI, "b" * 8)
    emit(qm_o, C_QM, "----")
    z = _mm(xb, w_ref[:, C_TAIL * LANES:(C_TAIL + 1) * LANES])
    lane = lax.broadcasted_iota(I32, z.shape, 1)
    zt = jnp.where(lane < D_IDX, rope_b(z), z)
    ki_o[...] = zt[:, :D_IDX]
    wi_o[...] = zt[:, D_IDX:D_IDX + H_IDX]


CACHE_OUTPUTS = (1, 3, 6)


def _proj_kernel_with_prev(*refs):
    _proj_kernel(*refs[:8], *refs[8 + len(CACHE_OUTPUTS):])


def _project(x2d, w_in_p, layer, tabs, n_tab_blocks, tm, cache_slots=1, prev_caches=None):
    m, d = x2d.shape
    n_blocks = m // tm
    shapes = ((1, W_QA), (N_KVA, LANES), (1, W_QB), (N_KVB, LANES), (1, W_QI), (1, W_QM), (1, D_IDX), (1, H_IDX))
    tab_spec = pl.BlockSpec((tm, LANES), lambda i: (i % n_tab_blocks, 0))
    slot = layer if cache_slots > 1 else 0
    out_specs, out_shape = [], []
    for k, (r, w) in enumerate(shapes):
        slots, base = (cache_slots, slot * n_blocks) if k in CACHE_OUTPUTS else (1, 0)
        out_specs.append(pl.BlockSpec((tm * r, w), lambda i, base=base: (base + i, 0)))
        out_shape.append(jax.ShapeDtypeStruct((slots * m * r, w), F32))
    in_specs = ([pl.BlockSpec((tm, d), lambda i: (i, 0)),
                 pl.BlockSpec((None, d, N_IN_PAD), lambda i: (layer, 0, 0), pipeline_mode=pl.Buffered(1))]
                + [tab_spec] * 6)
    args = [x2d, w_in_p, *tabs]
    aliases = {}
    if prev_caches is not None:
        aliases = {len(args) + n: k for n, k in enumerate(CACHE_OUTPUTS)}
        in_specs += [pl.BlockSpec(memory_space=pl.ANY)] * len(CACHE_OUTPUTS)
        args += list(prev_caches)
    return pl.pallas_call(
        _proj_kernel if prev_caches is None else _proj_kernel_with_prev,
        grid=(n_blocks,),
        in_specs=in_specs,
        out_specs=out_specs,
        out_shape=out_shape,
        input_output_aliases=aliases,
        compiler_params=_cparams(("parallel",)),
        name="proj_rope",
    )(*args)


def _matmul_kernel(x_ref, w_ref, *rest):
    o_ref = rest[-1]
    tm = x_ref.shape[0]
    n = w_ref.shape[1] // LANES
    z = _mm(x_ref[...].astype(MXU_DTYPE), w_ref[...])
    for j in range(n):
        o_ref[pl.ds(j, tm, stride=n), :] = z[:, j * LANES:(j + 1) * LANES]


def _matmul(x2d, w, layer, tm, prev=None):
    m, d = x2d.shape
    depth, _, n = w.shape
    n_blocks = m // tm
    in_specs = [pl.BlockSpec((tm, d), lambda i: (i, 0)),
                pl.BlockSpec((None, d, n), lambda i: (layer, 0, 0), pipeline_mode=pl.Buffered(1))]
    args = [x2d, w]
    aliases = {}
    if prev is not None:
        in_specs.append(pl.BlockSpec(memory_space=pl.ANY))
        aliases = {len(args): 0}
        args.append(prev)
    return pl.pallas_call(
        _matmul_kernel,
        grid=(n_blocks,),
        in_specs=in_specs,
        out_specs=pl.BlockSpec((tm * n // LANES, LANES), lambda i: (layer * n_blocks + i, 0)),
        out_shape=jax.ShapeDtypeStruct((depth * m * n // LANES, LANES), F32),
        input_output_aliases=aliases,
        compiler_params=_cparams(("parallel",)),
        name="mem_kv_proj",
    )(*args)


def _kth_threshold(key_ref, rows, n_chunks, topk):
    ones = jnp.ones((LANES, LANES), MXU_DTYPE)

    def count_ge(cand):
        acc = jnp.zeros((rows, LANES), F32)
        for c in range(n_chunks):
            acc = acc + jnp.where(key_ref[:, c * LANES:(c + 1) * LANES] >= cand, 1.0, 0.0)
        return _mm(acc.astype(MXU_DTYPE), ones)

    def body(it, t):
        cand = t + lax.shift_left(jnp.int32(1), jnp.int32(31) - it)
        return jnp.where(count_ge(cand) >= topk, cand, t)

    t = lax.fori_loop(0, 32, body, jnp.full((rows, LANES), INT_MIN, I32))
    return t, count_ge(t)


def _selection_bias(key_ref, bias_ref, t, n_ge, rows, n_chunks, topk):
    floor_t = jnp.maximum(t, jnp.int32(NEG_INF_KEY + 1))
    tie_rows = jnp.where((n_ge > topk) & (t > NEG_INF_KEY), 1.0, 0.0)
    has_tie = jnp.max(tie_rows) > 0.5

    @pl.when(jnp.logical_not(has_tie))
    def _():
        for c in range(n_chunks):
            sl = slice(c * LANES, (c + 1) * LANES)
            bias_ref[:, sl] = jnp.where(key_ref[:, sl] >= floor_t, 0.0, NEG_INF)

    @pl.when(has_tie)
    def _():
        ones = jnp.ones((LANES, LANES), MXU_DTYPE)
        r_i = lax.broadcasted_iota(I32, (LANES, LANES), 0)
        c_i = lax.broadcasted_iota(I32, (LANES, LANES), 1)
        strict_upper = jnp.where(r_i < c_i, 1.0, 0.0).astype(MXU_DTYPE)
        n_gt = jnp.zeros((rows, LANES), F32)
        for c in range(n_chunks):
            sl = slice(c * LANES, (c + 1) * LANES)
            n_gt = n_gt + _mm(jnp.where(key_ref[:, sl] > t, 1.0, 0.0).astype(MXU_DTYPE), ones)
        need = topk - n_gt
        run = jnp.zeros((rows, LANES), F32)
        for c in range(n_chunks):
            sl = slice(c * LANES, (c + 1) * LANES)
            k = key_ref[:, sl]
            eq = jnp.where(k == t, 1.0, 0.0).astype(MXU_DTYPE)
            before = run + _mm(eq, strict_upper)
            run = run + _mm(eq, ones)
            keep = (k > t) | ((k == t) & (before < need))
            bias_ref[:, sl] = jnp.where(keep & (k > NEG_INF_KEY), 0.0, NEG_INF)


def _count_rows(pred_fn, key_ref, n_keys):
    acc = jnp.zeros((LANES, LANES), F32)
    for c in range(n_keys // LANES):
        acc = acc + jnp.where(pred_fn(key_ref[c * LANES:(c + 1) * LANES, :]), 1.0, 0.0)
    return jnp.sum(acc, axis=0, keepdims=True)


def _kth_threshold_t(key_ref, n_keys, topk):
    def body(it, t):
        cand = t + lax.shift_left(jnp.int32(1), jnp.int32(31) - it)
        return jnp.where(_count_rows(lambda k: k >= cand, key_ref, n_keys) >= topk, cand, t)

    t = lax.fori_loop(0, 32, body, jnp.full((1, LANES), INT_MIN, I32))
    return t, _count_rows(lambda k: k >= t, key_ref, n_keys)


def _selection_bias_t(key_ref, store_bias, t, n_ge, n_keys, topk):
    floor_t = jnp.maximum(t, jnp.int32(NEG_INF_KEY + 1))
    tie_lanes = jnp.where((n_ge > topk) & (t > NEG_INF_KEY), 1.0, 0.0)
    has_tie = jnp.max(tie_lanes) > 0.5

    @pl.when(jnp.logical_not(has_tie))
    def _():
        for c in range(n_keys // LANES):
            sl = slice(c * LANES, (c + 1) * LANES)
            store_bias(sl, jnp.where(key_ref[sl, :] >= floor_t, 0.0, NEG_BIG))

    @pl.when(has_tie)
    def _():
        r_i = lax.broadcasted_iota(I32, (LANES, LANES), 0)
        c_i = lax.broadcasted_iota(I32, (LANES, LANES), 1)
        strict_lower = jnp.where(c_i < r_i, 1.0, 0.0).astype(MXU_DTYPE)
        need = topk - _count_rows(lambda k: k > t, key_ref, n_keys)
        run = jnp.zeros((1, LANES), F32)
        for c in range(n_keys // LANES):
            sl = slice(c * LANES, (c + 1) * LANES)
            k = key_ref[sl, :]
            eq = jnp.where(k == t, 1.0, 0.0)
            before = run + _mm(strict_lower, eq.astype(MXU_DTYPE))
            run = run + jnp.sum(eq, axis=0, keepdims=True)
            keep = (k > t) | ((k == t) & (before < need))
            store_bias(sl, jnp.where(keep & (k > NEG_INF_KEY), 0.0, NEG_BIG))


Q_BLOCK = 128
N_KV_CLASSES = 8


def _prompt_attn_kernel(*refs, topk, lam_init, q0, has_prev):
    (qa_ref, kva_ref, qi_ref, wit_ref, ki_ref, qb_ref, kvb_ref, qm_ref, mem_ref,
     lq1, lk1, lq2, lk2, g_ref) = refs[:14]
    rest = refs[14 + (1 if has_prev else 0):]
    o_ref, kab_s, vat_s, kbb_s, vbt_s, ki_s, km_s, vmt_s, key_ref = rest
    n_keys = ki_ref.shape[0]
    n_mem = km_s.shape[1]
    qi = pl.program_id(1)
    key_lanes = slice(0, LANES)
    mask_lanes = slice(LANES, 2 * LANES)

    @pl.when(qi == 0)
    def _():
        for g in range(HKV_A):
            kab_s[g, :, key_lanes] = kva_ref[pl.ds(g, n_keys, stride=N_KVA), :].astype(MXU_DTYPE)
            vat_s[g] = kva_ref[pl.ds(HKV_A + g, n_keys, stride=N_KVA), :].T.astype(MXU_DTYPE)
        for h in range(H_B):
            kbb_s[h, :, key_lanes] = kvb_ref[pl.ds(h, n_keys, stride=N_KVB), :].astype(MXU_DTYPE)
            vbt_s[h] = kvb_ref[pl.ds(H_B + h, n_keys, stride=N_KVB), :].T.astype(MXU_DTYPE)
        for h in range(H_MEM):
            km_s[h] = mem_ref[pl.ds(h, n_mem, stride=N_KVM), :].astype(MXU_DTYPE)
            vmt_s[h] = mem_ref[pl.ds(H_MEM + h, n_mem, stride=N_KVM), :].T.astype(MXU_DTYPE)
        ki_s[...] = ki_ref[...].astype(MXU_DTYPE)

    q_pos = (q0 + qi) * Q_BLOCK + lax.broadcasted_iota(I32, (n_keys, Q_BLOCK), 1)
    k_pos = lax.broadcasted_iota(I32, (n_keys, Q_BLOCK), 0)
    causal = k_pos <= q_pos

    causal_mask = jnp.where(causal, 0.0, NEG_BIG).astype(MXU_DTYPE)

    def store_selection(rows, block):
        for g in range(HKV_A):
            kab_s[g, rows, mask_lanes] = block.astype(MXU_DTYPE)

    if n_keys <= topk:
        store_selection(slice(0, n_keys), causal_mask)
    else:
        ki = ki_s[...]
        sc = jnp.zeros((n_keys, Q_BLOCK), F32)
        for hp in range(H_IDX // 2):
            q2 = jnp.concatenate([qi_ref[:, (2 * hp + j) * D_IDX:(2 * hp + j + 1) * D_IDX] for j in range(2)],
                                 axis=0).astype(MXU_DTYPE)
            s = _mm_nt(ki, q2)
            pair = (wit_ref[2 * hp:2 * hp + 1, :] * jnp.maximum(s[:, :Q_BLOCK], 0.0)
                    + wit_ref[2 * hp + 1:2 * hp + 2, :] * jnp.maximum(s[:, Q_BLOCK:], 0.0))
            sc = sc + pair
        key_ref[...] = _ordered_key(jnp.where(causal, sc, NEG_INF))
        t, n_ge = _kth_threshold_t(key_ref, n_keys, topk)
        _selection_bias_t(key_ref, store_selection, t, n_ge, n_keys, topk)

    eye = jnp.where(lax.broadcasted_iota(I32, (Q_BLOCK, Q_BLOCK), 0) == lax.broadcasted_iota(I32, (Q_BLOCK, Q_BLOCK), 1),
                    1.0, 0.0)

    def with_mask_selector(q_blocks, log2_scale):
        q = jnp.concatenate([qb_ * log2_scale for qb_ in q_blocks], axis=0)
        return jnp.concatenate([q, jnp.concatenate([eye] * len(q_blocks), axis=0)], axis=1).astype(MXU_DTYPE)

    def softmax_cols(s, n_cols):
        es, ls = [], []
        for r in range(n_cols):
            x = s[:, r * Q_BLOCK:(r + 1) * Q_BLOCK]
            e = jnp.exp2(x - jnp.max(x, axis=0, keepdims=True))
            es.append(e.astype(MXU_DTYPE))
            ls.append(jnp.sum(e, axis=0, keepdims=True))
        return jnp.concatenate(es, axis=1), ls

    rep = H_A // HKV_A
    for g in range(HKV_A):
        q4 = with_mask_selector([qa_ref[:, (g * rep + r) * DH_A:(g * rep + r + 1) * DH_A] for r in range(rep)],
                                DH_A ** -0.5 * LOG2_E)
        e, ls = softmax_cols(_mm_nt(kab_s[g], q4), rep)
        ot = _mm(vat_s[g], e)
        for r in range(rep):
            h = g * rep + r
            o_ref[:, h * DH_A:(h + 1) * DH_A] = (ot[:, r * Q_BLOCK:(r + 1) * Q_BLOCK] / ls[r]).T.astype(o_ref.dtype)

    lam = _lambda_value(lq1, lk1, lq2, lk2, lam_init)
    off_b = H_A * DH_A
    for h in range(H_B):
        kbb_s[h, :, mask_lanes] = causal_mask
        qh = qb_ref[:, h * DV_B:(h + 1) * DV_B]
        q2 = with_mask_selector([_half_lanes(qh, 0), _half_lanes(qh, 1)], DQK_B ** -0.5 * LOG2_E)
        e, ls = softmax_cols(_mm_nt(kbb_s[h], q2), 2)
        ot = _mm(vbt_s[h], e)
        o = (ot[:, :Q_BLOCK] / ls[0] - lam * (ot[:, Q_BLOCK:] / ls[1])).T
        o_ref[:, off_b + h * DV_B:off_b + (h + 1) * DV_B] = _diff_finish(o, g_ref, lam_init).astype(o_ref.dtype)

    off_m = off_b + H_B * DV_B
    for h in range(H_MEM):
        q = (qm_ref[:, h * DH_MEM:(h + 1) * DH_MEM] * (DH_MEM ** -0.5 * LOG2_E)).astype(MXU_DTYPE)
        e, ls = softmax_cols(_mm_nt(km_s[h], q), 1)
        o_ref[:, off_m + h * DH_MEM:off_m + (h + 1) * DH_MEM] = (_mm(vmt_s[h], e) / ls[0]).T.astype(o_ref.dtype)


def _prompt_attention(qa, kva, qi, wit, ki, qb, kvb, qm, mem_kv, lam_params, g, batch, seq, layer, lam_init):
    depth = lam_params[0].shape[0]
    n_q = seq // Q_BLOCK
    n_classes = min(N_KV_CLASSES, n_q)
    q_per_class = n_q // n_classes
    n_mem = mem_kv.shape[0] // (depth * batch * N_KVM)
    topk = min(INDEX_TOPK_MAX, seq // 4)
    kva3 = kva.reshape(depth * batch, seq * N_KVA, LANES)
    kvb3 = kvb.reshape(depth * batch, seq * N_KVB, LANES)
    ki3 = ki.reshape(depth * batch, seq, D_IDX)
    mem3 = mem_kv.reshape(depth * batch, n_mem * N_KVM, LANES)
    vec_spec = pl.BlockSpec((None, 1, DQK_B), lambda b, i: (layer, 0, 0))

    def kv_spec(rows, w):
        return pl.BlockSpec((None, rows, w), lambda b, i: (layer * batch + b, 0, 0))

    mixed = None
    for c in range(n_classes):
        q0 = c * q_per_class
        n_keys = (c + 1) * q_per_class * Q_BLOCK

        def q_spec(w, q0=q0):
            return pl.BlockSpec((Q_BLOCK, w), lambda b, i: (b * n_q + q0 + i, 0))

        in_specs = [q_spec(W_QA), kv_spec(n_keys * N_KVA, LANES), q_spec(W_QI),
                    pl.BlockSpec((H_IDX, Q_BLOCK), lambda b, i, q0=q0: (0, b * n_q + q0 + i)),
                    kv_spec(n_keys, D_IDX), q_spec(W_QB), kv_spec(n_keys * N_KVB, LANES), q_spec(W_QM),
                    kv_spec(n_mem * N_KVM, LANES), vec_spec, vec_spec, vec_spec, vec_spec,
                    pl.BlockSpec((None, 1, DV_B), lambda b, i: (layer, 0, 0))]
        args = [qa, kva3, qi, wit, ki3, qb, kvb3, qm, mem3, *lam_params, g]
        aliases = {}
        if mixed is not None:
            in_specs.append(pl.BlockSpec(memory_space=pl.ANY))
            aliases = {len(args): 0}
            args.append(mixed)
        kernel = functools.partial(_prompt_attn_kernel, topk=topk, lam_init=lam_init, q0=q0,
                                   has_prev=mixed is not None)
        mixed = pl.pallas_call(
            kernel,
            grid=(batch, q_per_class),
            in_specs=in_specs,
            out_specs=q_spec(D_MIX),
            out_shape=jax.ShapeDtypeStruct((batch * seq, D_MIX), MXU_DTYPE),
            scratch_shapes=[pltpu.VMEM((HKV_A, n_keys, 2 * LANES), MXU_DTYPE),
                            pltpu.VMEM((HKV_A, DH_A, n_keys), MXU_DTYPE),
                            pltpu.VMEM((H_B, n_keys, 2 * LANES), MXU_DTYPE), pltpu.VMEM((H_B, DV_B, n_keys), MXU_DTYPE),
                            pltpu.VMEM((n_keys, D_IDX), MXU_DTYPE),
                            pltpu.VMEM((H_MEM, n_mem, DH_MEM), MXU_DTYPE),
                            pltpu.VMEM((H_MEM, DH_MEM, n_mem), MXU_DTYPE),
                            pltpu.VMEM((n_keys, Q_BLOCK), I32)],
            input_output_aliases=aliases,
            compiler_params=_cparams(("parallel", "arbitrary")),
            name=f"prompt_attention_kv{c}",
        )(*args)
    return mixed


def _pad_rows(x, rows):
    return jnp.concatenate([x, jnp.zeros((rows - x.shape[0], x.shape[1]), x.dtype)], axis=0)


def _sample_index_kernel(pt_ref, qi_ref, wcol_ref, kin_ref, *rest, n_pages, group):
    del pt_ref
    idx_pages = rest[:group * n_pages]
    key_o = rest[group * n_pages]
    t_new = qi_ref.shape[0] // group
    past = n_pages * PAGE_SIZE

    lane = lax.broadcasted_iota(I32, (t_new, LANES), 1)
    tok = lax.broadcasted_iota(I32, (t_new, LANES), 0)
    new_ok = lane <= tok

    for b in range(group):
        rows = slice(b * t_new, (b + 1) * t_new)
        q_ht = jnp.concatenate([qi_ref[rows, h * D_IDX:(h + 1) * D_IDX] for h in range(H_IDX)],
                               axis=0).astype(MXU_DTYPE)
        wcol = wcol_ref[b * H_IDX * t_new:(b + 1) * H_IDX * t_new, :]

        def index_scores(qk):
            s = jnp.maximum(qk, 0.0) * wcol
            acc = s[0:t_new]
            for h in range(1, H_IDX):
                acc = acc + s[h * t_new:(h + 1) * t_new]
            return acc

        for j in range(n_pages):
            qk = _mm(q_ht, idx_pages[b * n_pages + j][...].astype(MXU_DTYPE))
            key_o[rows, j * LANES:(j + 1) * LANES] = _ordered_key(index_scores(qk))
        s_new = index_scores(_mm_nt(q_ht, _pad_rows(kin_ref[rows, :], LANES).astype(MXU_DTYPE)))
        key_o[rows, past:past + LANES] = _ordered_key(jnp.where(new_ok, s_new, NEG_INF))


def _sample_index(page_table, qi, wcol, ki_new, pool_idx, batch, t_new, layer, n_phys, group):
    n_pages = page_table.shape[1]
    width = (n_pages + 1) * LANES

    def row_spec(rows, w):
        return pl.BlockSpec((rows, w), lambda s, pt: (s, 0))

    def page_spec(b, j):
        return pl.BlockSpec((None, D_IDX, PAGE_SIZE),
                            lambda s, pt: (layer * n_phys + pt[s * group + b, j], 0, 0))

    in_specs = ([row_spec(group * t_new, W_QI), row_spec(group * H_IDX * t_new, 1), row_spec(group * t_new, D_IDX)]
                + [page_spec(b, j) for b in range(group) for j in range(n_pages)])
    grid_spec = pltpu.PrefetchScalarGridSpec(
        num_scalar_prefetch=1, grid=(batch // group,), in_specs=in_specs,
        out_specs=pl.BlockSpec((group * t_new, width), lambda s, pt: (s, 0)))
    return pl.pallas_call(
        functools.partial(_sample_index_kernel, n_pages=n_pages, group=group),
        grid_spec=grid_spec,
        out_shape=jax.ShapeDtypeStruct((batch * t_new, width), I32),
        compiler_params=_cparams(("parallel",)),
        name="sample_indexer",
    )(page_table, qi, wcol, ki_new, *([pool_idx] * (group * n_pages)))


def _sample_select_kernel(key_ref, bias_ref, *, topk):
    rows, width = key_ref.shape
    n_chunks = width // LANES
    t, n_ge = _kth_threshold(key_ref, rows, n_chunks, topk)
    _selection_bias(key_ref, bias_ref, t, n_ge, rows, n_chunks, topk)


def _sample_select(keys, topk, tr):
    m, width = keys.shape
    spec = pl.BlockSpec((tr, width), lambda i: (i, 0))
    return pl.pallas_call(
        functools.partial(_sample_select_kernel, topk=topk),
        grid=(m // tr,),
        in_specs=[spec],
        out_specs=spec,
        out_shape=jax.ShapeDtypeStruct((m, width), F32),
        compiler_params=_cparams(("parallel",)),
        name="sample_topk_select",
    )(keys)


def _sample_attn_kernel(pt_ref, qa_ref, kvan_ref, bias_ref, qb_ref, kvbn_ref, qm_ref, mem_ref,
                        lq1, lk1, lq2, lk2, g_ref, *rest, n_pages, lam_init):
    del pt_ref
    kva_pages = rest[:n_pages]
    kvb_pages = rest[n_pages:2 * n_pages]
    o_ref, sa_ref, sb_ref = rest[2 * n_pages:]
    t_new = qa_ref.shape[0]
    past = n_pages * PAGE_SIZE

    lane = lax.broadcasted_iota(I32, (t_new, LANES), 1)
    tok = lax.broadcasted_iota(I32, (t_new, LANES), 0)
    new_ok = lane <= tok

    def chunk_rows(ref, j, rows, n):
        return ref[pl.ds(j, rows, stride=n), :].astype(MXU_DTYPE)

    def new_rows(ref, j, n):
        return _pad_rows(ref[pl.ds(j, t_new, stride=n), :], LANES).astype(MXU_DTYPE)

    scale_a = DH_A ** -0.5
    rep = H_A // HKV_A
    bias_a = jnp.concatenate([bias_ref[...]] * rep, axis=0)
    for g in range(HKV_A):
        q = jnp.concatenate([qa_ref[:, (g * rep + r) * DH_A:(g * rep + r + 1) * DH_A] for r in range(rep)],
                            axis=0).astype(MXU_DTYPE)
        for j in range(n_pages):
            sa_ref[:, j * LANES:(j + 1) * LANES] = _mm_nt(q, chunk_rows(kva_pages[j], g, PAGE_SIZE, N_KVA))
        sa_ref[:, past:past + LANES] = _mm_nt(q, new_rows(kvan_ref, g, N_KVA))
        e, l = _exp_and_sum(sa_ref[...] * scale_a + bias_a)
        e = e.astype(MXU_DTYPE)
        acc = _mm(e[:, past:past + LANES], new_rows(kvan_ref, HKV_A + g, N_KVA))
        for j in range(n_pages):
            acc = acc + _mm(e[:, j * LANES:(j + 1) * LANES], chunk_rows(kva_pages[j], HKV_A + g, PAGE_SIZE, N_KVA))
        acc = acc / l
        for r in range(rep):
            h = g * rep + r
            o_ref[:, h * DH_A:(h + 1) * DH_A] = acc[r * t_new:(r + 1) * t_new]

    lam = _lambda_value(lq1, lk1, lq2, lk2, lam_init)
    scale_b = DQK_B ** -0.5
    for h in range(H_B):
        qh = qb_ref[:, h * DV_B:(h + 1) * DV_B]
        q = jnp.concatenate([_half_lanes(qh, 0), _half_lanes(qh, 1)], axis=0).astype(MXU_DTYPE)
        rows = slice(2 * h * t_new, (2 * h + 2) * t_new)
        for j in range(n_pages):
            sb_ref[rows, j * LANES:(j + 1) * LANES] = _mm_nt(q, chunk_rows(kvb_pages[j], h, PAGE_SIZE, N_KVB))
        sb_ref[rows, past:past + LANES] = _mm_nt(q, new_rows(kvbn_ref, h, N_KVB))
    new_bias = jnp.where(new_ok, 0.0, NEG_INF)
    bias_b = jnp.concatenate(
        [jnp.zeros((N_KVB * t_new, past), F32), jnp.concatenate([new_bias] * N_KVB, axis=0)], axis=1)
    e, l = _exp_and_sum(sb_ref[...] * scale_b + bias_b)
    e = e.astype(MXU_DTYPE)
    off_b = H_A * DH_A
    for h in range(H_B):
        rows = slice(2 * h * t_new, (2 * h + 2) * t_new)
        acc = _mm(e[rows, past:past + LANES], new_rows(kvbn_ref, H_B + h, N_KVB))
        for j in range(n_pages):
            acc = acc + _mm(e[rows, j * LANES:(j + 1) * LANES], chunk_rows(kvb_pages[j], H_B + h, PAGE_SIZE, N_KVB))
        acc = acc / l[rows]
        o = acc[0:t_new] - lam * acc[t_new:2 * t_new]
        o_ref[:, off_b + h * DV_B:off_b + (h + 1) * DV_B] = _diff_finish(o, g_ref, lam_init)

    scale_m = DH_MEM ** -0.5
    off_m = off_b + H_B * DV_B
    n_mem = mem_ref.shape[0] // N_KVM
    for h in range(H_MEM):
        q = qm_ref[:, h * DH_MEM:(h + 1) * DH_MEM].astype(MXU_DTYPE)
        e, l = _exp_and_sum(_mm_nt(q, chunk_rows(mem_ref, h, n_mem, N_KVM)) * scale_m)
        o_ref[:, off_m + h * DH_MEM:off_m + (h + 1) * DH_MEM] = (
            _mm(e.astype(MXU_DTYPE), chunk_rows(mem_ref, H_MEM + h, n_mem, N_KVM)) / l)


def _sample_attention(page_table, qa, kva_new, bias, qb, kvb_new, qm, pool_kva, pool_kvb, pool_mem,
                      lam_params, g, batch, t_new, layer, lam_init, n_phys, mem_rows):
    n_pages = page_table.shape[1]
    width = (n_pages + 1) * LANES

    def row_spec(rows, w):
        return pl.BlockSpec((rows, w), lambda b, pt: (b, 0))

    def page_spec(j, n):
        return pl.BlockSpec((PAGE_SIZE * n, LANES), lambda b, pt: (layer * n_phys + pt[b, j], 0))

    vec_spec = pl.BlockSpec((None, 1, DQK_B), lambda b, pt: (layer, 0, 0))
    in_specs = ([row_spec(t_new, W_QA), row_spec(t_new * N_KVA, LANES), row_spec(t_new, width),
                 row_spec(t_new, W_QB), row_spec(t_new * N_KVB, LANES), row_spec(t_new, W_QM),
                 pl.BlockSpec((mem_rows, LANES), lambda b, pt: (layer * batch + b, 0)),
                 vec_spec, vec_spec, vec_spec, vec_spec,
                 pl.BlockSpec((None, 1, DV_B), lambda b, pt: (layer, 0, 0))]
                + [page_spec(j, N_KVA) for j in range(n_pages)]
                + [page_spec(j, N_KVB) for j in range(n_pages)])
    grid_spec = pltpu.PrefetchScalarGridSpec(
        num_scalar_prefetch=1,
        grid=(batch,),
        in_specs=in_specs,
        out_specs=pl.BlockSpec((t_new, D_MIX), lambda b, pt: (b, 0)),
        scratch_shapes=[pltpu.VMEM((H_A // HKV_A * t_new, width), F32),
                        pltpu.VMEM((N_KVB * t_new, width), F32)],
    )
    return pl.pallas_call(
        functools.partial(_sample_attn_kernel, n_pages=n_pages, lam_init=lam_init),
        grid_spec=grid_spec,
        out_shape=jax.ShapeDtypeStruct((batch * t_new, D_MIX), F32),
        compiler_params=_cparams(("parallel",)),
        name="sample_attention",
    )(page_table, qa, kva_new, bias, qb, kvb_new, qm, pool_mem, *lam_params, g,
      *([pool_kva] * n_pages), *([pool_kvb] * n_pages))


def _attn_out_kernel(x_ref, mix_ref, wo_ref, g_ref, b_ref, h_ref, *, alpha):
    a = _mm(mix_ref[...].astype(MXU_DTYPE), wo_ref[...])
    h_ref[...] = _layer_norm(alpha * x_ref[...] + a, g_ref[...], b_ref[...])


def _attn_out(x2d, mixed, w_o, ln_g, ln_b, layer, alpha, tm):
    m, d = x2d.shape
    row = pl.BlockSpec((tm, d), lambda i: (i, 0))
    vec = pl.BlockSpec((None, 1, d), lambda i: (layer, 0, 0))
    return pl.pallas_call(
        functools.partial(_attn_out_kernel, alpha=alpha),
        grid=(m // tm,),
        in_specs=[row, pl.BlockSpec((tm, mixed.shape[1]), lambda i: (i, 0)),
                  pl.BlockSpec((None, mixed.shape[1], d), lambda i: (layer, 0, 0), pipeline_mode=pl.Buffered(1)),
                  vec, vec],
        out_specs=row,
        out_shape=jax.ShapeDtypeStruct((m, d), F32),
        compiler_params=_cparams(("parallel",)),
        name="attn_out_ln",
    )(x2d, mixed, w_o, ln_g, ln_b)


def _ffn_kernel(h_ref, wg_ref, wu_ref, wd_ref, g_ref, b_ref, o_ref, hb_ref, acc_ref, *, alpha):
    j = pl.program_id(1)

    @pl.when(j == 0)
    def _():
        hb_ref[...] = h_ref[...].astype(MXU_DTYPE)
        acc_ref[...] = jnp.zeros_like(acc_ref)

    hb = hb_ref[...]
    gate = _mm(hb, wg_ref[...])
    up = _mm(hb, wu_ref[...])
    act = gate * jax.nn.sigmoid(gate) * up
    acc_ref[...] += _mm(act.astype(MXU_DTYPE), wd_ref[...])

    @pl.when(j == pl.num_programs(1) - 1)
    def _():
        o_ref[...] = _layer_norm(alpha * h_ref[...] + acc_ref[...], g_ref[...], b_ref[...])


def _ffn(h2d, w_gate, w_up, w_down, ln_g, ln_b, layer, alpha, tm, tf):
    m, d = h2d.shape
    f = w_gate.shape[-1]
    row = pl.BlockSpec((tm, d), lambda i, j: (i, 0))
    vec = pl.BlockSpec((None, 1, d), lambda i, j: (layer, 0, 0))
    return pl.pallas_call(
        functools.partial(_ffn_kernel, alpha=alpha),
        grid=(m // tm, f // tf),
        in_specs=[row,
                  pl.BlockSpec((None, d, tf), lambda i, j: (layer, 0, j)),
                  pl.BlockSpec((None, d, tf), lambda i, j: (layer, 0, j)),
                  pl.BlockSpec((None, tf, d), lambda i, j: (layer, j, 0)),
                  vec, vec],
        out_specs=row,
        out_shape=jax.ShapeDtypeStruct((m, d), F32),
        scratch_shapes=[pltpu.VMEM((tm, d), MXU_DTYPE), pltpu.VMEM((tm, d), F32)],
        compiler_params=_cparams(("parallel", "arbitrary")),
        name="swiglu_ln",
    )(h2d, w_gate, w_up, w_down, ln_g, ln_b)


def _row_tile(m, cap):
    t = min(m, cap)
    while m % t:
        t //= 2
    return t


def _ff_tile(f, cap):
    best = LANES
    for t in range(LANES, cap + 1, LANES):
        if f % t == 0:
            best = t
    return best


def _largest_divisor(n, cap):
    return max(d for d in range(1, cap + 1) if n % d == 0)


def kernel(x_prompt, x_sample, mem_prompt, cache_kv_a, cache_idx_k, cache_kv_b, cache_mem_kv, page_table,
           w_in, w_mem_kv, lambda_q1, lambda_k1, lambda_q2, lambda_k2, subln_g, w_o,
           ln1_g, ln1_b, w_gate, w_up, w_down, ln2_g, ln2_b):
    depth, d_model, _ = w_in.shape
    batch, seq, _ = x_prompt.shape
    dec_batch, t_new, _ = x_sample.shape
    n_pages = page_table.shape[1]
    past = n_pages * PAGE_SIZE
    n_mem = mem_prompt.shape[1]
    n_phys = cache_idx_k.shape[1]
    alpha = (2.0 * depth) ** 0.25

    n_main = W_QA + W_KVA + W_QB + W_KVB
    w_in_p = jnp.concatenate(
        [w_in[:, :, :n_main], w_in[:, :, n_main + W_QM:n_main + W_QM + W_QI], w_in[:, :, n_main:n_main + W_QM],
         w_in[:, :, n_main + W_QM + W_QI:],
         jnp.zeros((depth, d_model, N_IN_PAD - w_in.shape[2]), w_in.dtype)], axis=-1).astype(MXU_DTYPE)
    w_mem_c = w_mem_kv.astype(MXU_DTYPE)
    w_o_c, w_gate_c, w_up_c, w_down_c = (w.astype(MXU_DTYPE) for w in (w_o, w_gate, w_up, w_down))
    vec3 = lambda a: a.astype(F32).reshape(depth, 1, a.shape[-1])
    lam_params = tuple(vec3(a) for a in (lambda_q1, lambda_k1, lambda_q2, lambda_k2))
    g_sub = vec3(subln_g)
    ln1g, ln1b, ln2g, ln2b = (vec3(a) for a in (ln1_g, ln1_b, ln2_g, ln2_b))

    pool_kva = cache_kv_a.reshape(-1, LANES)
    pool_idx = jnp.swapaxes(cache_idx_k, 2, 3).reshape(depth * n_phys, D_IDX, PAGE_SIZE)
    pool_kvb = cache_kv_b.reshape(-1, LANES)
    pool_mem = cache_mem_kv.reshape(-1, LANES)

    tm_p = _row_tile(seq, 512)
    m_s = dec_batch * t_new
    tm_s = _row_tile(m_s, 512)
    pos_p = jnp.arange(seq, dtype=I32)
    pos_s = past + (jnp.arange(tm_s, dtype=I32) % t_new)
    tabs_p = _rope_tables(pos_p, DH_A) + _rope_tables(pos_p, DQK_B)
    tabs_s = _rope_tables(pos_s, DH_A) + _rope_tables(pos_s, DQK_B)

    xp = x_prompt.reshape(batch * seq, d_model)
    xs = x_sample.reshape(m_s, d_model)
    mem2d = mem_prompt.reshape(batch * n_mem, d_model)
    tf = _ff_tile(w_gate.shape[-1], 512)
    topk_s = min(INDEX_TOPK_MAX, (past + t_new) // 4)
    idx_group = _largest_divisor(dec_batch, 4)

    outs = [[] for _ in range(3)]
    caches, mem_kv = None, None
    for l in range(depth):
        lam_init = 0.8 - 0.6 * math.exp(-0.3 * l)

        qa, kva, qb, kvb, qi, qm, ki, wi = _project(xp, w_in_p, l, tabs_p, seq // tm_p, tm_p,
                                                    cache_slots=depth, prev_caches=caches)
        caches = (kva, kvb, ki)
        mem_kv = _matmul(mem2d, w_mem_c, l, _row_tile(batch * n_mem, 512), prev=mem_kv)
        mixed = _prompt_attention(qa, kva, qi, wi.T, ki, qb, kvb, qm, mem_kv, lam_params, g_sub,
                                  batch, seq, l, lam_init)
        h = _attn_out(xp, mixed, w_o_c, ln1g, ln1b, l, alpha, tm_p)
        xp = _ffn(h, w_gate_c, w_up_c, w_down_c, ln2g, ln2b, l, alpha, tm_p, tf)

        qa, kva, qb, kvb, qi, qm, ki, wi = _project(xs, w_in_p, l, tabs_s, 1, tm_s)
        wcol = wi.reshape(dec_batch, t_new, H_IDX).transpose(0, 2, 1).reshape(dec_batch * H_IDX * t_new, 1)
        keys = _sample_index(page_table, qi, wcol, ki, pool_idx, dec_batch, t_new, l, n_phys, idx_group)
        bias = _sample_select(keys, topk_s, _row_tile(m_s, 128))
        mixed = _sample_attention(page_table, qa, kva, bias, qb, kvb, qm, pool_kva, pool_kvb, pool_mem,
                                  lam_params, g_sub, dec_batch, t_new, l, lam_init, n_phys, n_mem * N_KVM)
        h = _attn_out(xs, mixed, w_o_c, ln1g, ln1b, l, alpha, tm_s)
        xs = _ffn(h, w_gate_c, w_up_c, w_down_c, ln2g, ln2b, l, alpha, tm_s, tf)
        outs[0].append(kva.reshape(dec_batch, t_new, 2, HKV_A, DH_A))
        outs[1].append(ki.reshape(dec_batch, t_new, D_IDX))
        outs[2].append(kvb.reshape(dec_batch, t_new, 2, H_B, DV_B))

    kva_p, kvb_p, ki_p = caches
    return (xp.reshape(batch, seq, d_model), xs.reshape(dec_batch, t_new, d_model),
            kva_p.reshape(depth, batch, seq, 2, HKV_A, DH_A), ki_p.reshape(depth, batch, seq, D_IDX),
            kvb_p.reshape(depth, batch, seq, 2, H_B, DV_B), mem_kv.reshape(depth, batch, n_mem, 2, H_MEM, DH_MEM)
            ) + tuple(jnp.stack(o) for o in outs)
```

```python
import functools
import math

import jax
import jax.numpy as jnp
from jax import lax
from jax.experimental import pallas as pl
from jax.experimental.pallas import tpu as pltpu

F32 = jnp.float32
I32 = jnp.int32
MXU_DTYPE = jnp.bfloat16

H_A, DH_A, HKV_A = 8, 128, 2
H_IDX, D_IDX = 16, 64
INDEX_TOPK_MAX = 256
H_B, DQK_B, DV_B = 4, 64, 128
H_MEM, DH_MEM = 4, 128
PAGE_SIZE = 128
ROPE_THETA = 500000.0
ROPE_DIV = 4
LN_EPS = 1e-5
RMS_EPS = 1e-5
D_MIX = H_A * DH_A + H_B * DV_B + H_MEM * DH_MEM
N_KVA, N_KVB, N_KVM = 2 * HKV_A, 2 * H_B, 2 * H_MEM

LANES = 128
VMEM_LIMIT_BYTES = 56 * 1024 * 1024

W_QA, W_KVA, W_QB, W_KVB, W_QI, W_QM = 1024, 512, 512, 1024, 1024, 512
C_QA, C_KVA, C_QB, C_KVB, C_QI, C_QM, C_TAIL = 0, 8, 12, 16, 24, 32, 36
N_IN_PAD = 37 * LANES

LOG2_E = math.log2(math.e)
NEG_INF = float("-inf")
NEG_BIG = -1e30
INT_MIN = -(2 ** 31)
NEG_INF_KEY = -2139095041


def _cparams(sem):
    return pltpu.CompilerParams(dimension_semantics=sem, vmem_limit_bytes=VMEM_LIMIT_BYTES)


def _mm(a, b):
    return jnp.dot(a, b, preferred_element_type=F32)


def _mm_nt(a, b):
    return lax.dot_general(a, b, (((1,), (1,)), ((), ())), preferred_element_type=F32)


def _layer_norm(y, g, b):
    mu = jnp.mean(y, axis=-1, keepdims=True)
    d = y - mu
    var = jnp.mean(d * d, axis=-1, keepdims=True)
    return d * lax.rsqrt(var + LN_EPS) * g + b


def _exp_and_sum(s, axis=-1):
    m = jnp.max(s, axis=axis, keepdims=True)
    e = jnp.exp(s - m)
    return e, jnp.sum(e, axis=axis, keepdims=True)


def _ordered_key(x):
    k = pltpu.bitcast(x, I32)
    return jnp.where(k < 0, k ^ jnp.int32(0x7FFFFFFF), k)


def _half_lanes(x, c):
    lane = lax.broadcasted_iota(I32, x.shape, 1)
    return jnp.where((lane >= c * DQK_B) & (lane < (c + 1) * DQK_B), x, 0.0)


def _lambda_value(lq1, lk1, lq2, lk2, lam_init):
    a = jnp.sum(lq1[...] * lk1[...], axis=-1, keepdims=True)
    b = jnp.sum(lq2[...] * lk2[...], axis=-1, keepdims=True)
    return jnp.exp(a) - jnp.exp(b) + lam_init


def _diff_finish(o, g_ref, lam_init):
    o = o * lax.rsqrt(jnp.mean(o * o, axis=-1, keepdims=True) + RMS_EPS)
    return o * g_ref[...] * (1.0 - lam_init)


def _rope_tables(pos, head_dim):
    rot = head_dim // ROPE_DIV
    half = rot // 2
    t = pos.shape[0]
    inv_freq = jnp.float32(ROPE_THETA) ** (-jnp.arange(half, dtype=F32) / half)
    ang = pos.astype(F32)[:, None] * inv_freq[None, :]
    cos, sin = jnp.cos(ang), jnp.sin(ang)
    c = jnp.concatenate([cos, cos, jnp.ones((t, head_dim - rot), F32)], axis=-1)
    s_hi = jnp.concatenate([-sin, jnp.zeros((t, head_dim - half), F32)], axis=-1)
    s_lo = jnp.concatenate([jnp.zeros((t, half), F32), sin, jnp.zeros((t, head_dim - rot), F32)], axis=-1)
    rep = LANES // head_dim
    return tuple(jnp.tile(a, (1, rep)) for a in (c, s_hi, s_lo))


def _proj_kernel(x_ref, w_ref, ca, sa1, sa2, cb, sb1, sb2, *rest, n_prev, attn_layout):
    qa_o, kva_o, qb_o, kvb_o, qi_o, qm_o, ki_o, wi_o = rest[n_prev:n_prev + 8]
    ka_t, vat_t, kb_t, vbt_t, ki_t = rest[n_prev + 8:] if attn_layout else (None,) * 5
    xb = x_ref[...].astype(MXU_DTYPE)
    tm = x_ref.shape[0]

    def rope(z, c, s_hi, s_lo, half):
        return z * c[...] + pltpu.roll(z, LANES - half, 1) * s_hi[...] + pltpu.roll(z, half, 1) * s_lo[...]

    def rope_a(z):
        return rope(z, ca, sa1, sa2, DH_A // ROPE_DIV // 2)

    def rope_b(z):
        return rope(z, cb, sb1, sb2, DQK_B // ROPE_DIV // 2)

    def emit(out_ref, c0, kinds, interleave=False, key_t=None, value_t=None):
        n = len(kinds)
        z = _mm(xb, w_ref[:, c0 * LANES:(c0 + n) * LANES])
        for j, kind in enumerate(kinds):
            zj = z[:, j * LANES:(j + 1) * LANES]
            if kind == "a":
                zj = rope_a(zj)
            elif kind == "b":
                zj = rope_b(zj)
            if interleave:
                out_ref[pl.ds(j, tm, stride=n), :] = zj
                if key_t is not None:
                    if j < n // 2:
                        key_t[j] = zj.astype(MXU_DTYPE)
                    else:
                        value_t[j - n // 2] = zj.T.astype(MXU_DTYPE)
            else:
                out_ref[:, j * LANES:(j + 1) * LANES] = zj

    emit(qa_o, C_QA, "a" * 8)
    emit(kva_o, C_KVA, "aa--", interleave=True, key_t=ka_t, value_t=vat_t)
    emit(qb_o, C_QB, "bbbb")
    emit(kvb_o, C_KVB, "bbbb----", interleave=True, key_t=kb_t, value_t=vbt_t)
    emit(qi_o, C_QI, "b" * 8)
    emit(qm_o, C_QM, "----")
    z = _mm(xb, w_ref[:, C_TAIL * LANES:(C_TAIL + 1) * LANES])
    lane = lax.broadcasted_iota(I32, z.shape, 1)
    zt = jnp.where(lane < D_IDX, rope_b(z), z)
    ki_o[...] = zt[:, :D_IDX]
    wi_o[...] = zt[:, D_IDX:D_IDX + H_IDX]
    if ki_t is not None:
        ki_t[...] = zt[:, :D_IDX].astype(MXU_DTYPE)


CACHE_OUTPUTS = (1, 3, 6)


def _project(x2d, w_in_p, layer, tabs, n_tab_blocks, tm, cache_slots=1, prev_caches=None, attn_batch=None):
    m, d = x2d.shape
    n_blocks = m // tm
    shapes = ((1, W_QA), (N_KVA, LANES), (1, W_QB), (N_KVB, LANES), (1, W_QI), (1, W_QM), (1, D_IDX), (1, H_IDX))
    tab_spec = pl.BlockSpec((tm, LANES), lambda i: (i % n_tab_blocks, 0))
    slot = layer if cache_slots > 1 else 0
    out_specs, out_shape = [], []
    for k, (r, w) in enumerate(shapes):
        slots, base = (cache_slots, slot * n_blocks) if k in CACHE_OUTPUTS else (1, 0)
        out_specs.append(pl.BlockSpec((tm * r, w), lambda i, base=base: (base + i, 0)))
        out_shape.append(jax.ShapeDtypeStruct((slots * m * r, w), F32))
    if attn_batch is not None:
        seq = m // attn_batch
        per_seq = seq // tm
        for heads in (HKV_A, H_B):
            out_specs.append(pl.BlockSpec((None, heads, tm, LANES), lambda i: (i // per_seq, 0, i % per_seq, 0)))
            out_shape.append(jax.ShapeDtypeStruct((attn_batch, heads, seq, LANES), MXU_DTYPE))
            out_specs.append(pl.BlockSpec((None, heads, LANES, tm), lambda i: (i // per_seq, 0, 0, i % per_seq)))
            out_shape.append(jax.ShapeDtypeStruct((attn_batch, heads, LANES, seq), MXU_DTYPE))
        out_specs.append(pl.BlockSpec((None, tm, D_IDX), lambda i: (i // per_seq, i % per_seq, 0)))
        out_shape.append(jax.ShapeDtypeStruct((attn_batch, seq, D_IDX), MXU_DTYPE))
    in_specs = ([pl.BlockSpec((tm, d), lambda i: (i, 0)),
                 pl.BlockSpec((None, d, N_IN_PAD), lambda i: (layer, 0, 0), pipeline_mode=pl.Buffered(1))]
                + [tab_spec] * 6)
    args = [x2d, w_in_p, *tabs]
    aliases = {}
    n_prev = 0
    if prev_caches is not None:
        n_prev = len(CACHE_OUTPUTS)
        aliases = {len(args) + n: k for n, k in enumerate(CACHE_OUTPUTS)}
        in_specs += [pl.BlockSpec(memory_space=pl.ANY)] * n_prev
        args += list(prev_caches)
    return pl.pallas_call(
        functools.partial(_proj_kernel, n_prev=n_prev, attn_layout=attn_batch is not None),
        grid=(n_blocks,),
        in_specs=in_specs,
        out_specs=out_specs,
        out_shape=out_shape,
        input_output_aliases=aliases,
        compiler_params=_cparams(("parallel",)),
        name="proj_rope",
    )(*args)


def _matmul_kernel(x_ref, w_ref, *rest):
    o_ref = rest[-1]
    tm = x_ref.shape[0]
    n = w_ref.shape[1] // LANES
    z = _mm(x_ref[...].astype(MXU_DTYPE), w_ref[...])
    for j in range(n):
        o_ref[pl.ds(j, tm, stride=n), :] = z[:, j * LANES:(j + 1) * LANES]


def _matmul(x2d, w, layer, tm, prev=None):
    m, d = x2d.shape
    depth, _, n = w.shape
    n_blocks = m // tm
    in_specs = [pl.BlockSpec((tm, d), lambda i: (i, 0)),
                pl.BlockSpec((None, d, n), lambda i: (layer, 0, 0), pipeline_mode=pl.Buffered(1))]
    args = [x2d, w]
    aliases = {}
    if prev is not None:
        in_specs.append(pl.BlockSpec(memory_space=pl.ANY))
        aliases = {len(args): 0}
        args.append(prev)
    return pl.pallas_call(
        _matmul_kernel,
        grid=(n_blocks,),
        in_specs=in_specs,
        out_specs=pl.BlockSpec((tm * n // LANES, LANES), lambda i: (layer * n_blocks + i, 0)),
        out_shape=jax.ShapeDtypeStruct((depth * m * n // LANES, LANES), F32),
        input_output_aliases=aliases,
        compiler_params=_cparams(("parallel",)),
        name="mem_kv_proj",
    )(*args)


def _kth_threshold(key_ref, rows, n_chunks, topk, groups=1):
    ones = jnp.ones((LANES, LANES), MXU_DTYPE)
    gr = rows // groups

    def count_ge(cand, g):
        acc = jnp.zeros((gr, LANES), F32)
        for c in range(n_chunks):
            acc = acc + jnp.where(key_ref[g * gr:(g + 1) * gr, c * LANES:(c + 1) * LANES] >= cand, 1.0, 0.0)
        return _mm(acc.astype(MXU_DTYPE), ones)

    def body(it, ts):
        bit = lax.shift_left(jnp.int32(1), jnp.int32(31) - it)
        return tuple(jnp.where(count_ge(t + bit, g) >= topk, t + bit, t) for g, t in enumerate(ts))

    ts = lax.fori_loop(0, 32, body, tuple(jnp.full((gr, LANES), INT_MIN, I32) for _ in range(groups)))
    n_ge = [count_ge(t, g) for g, t in enumerate(ts)]
    return jnp.concatenate(ts, axis=0), jnp.concatenate(n_ge, axis=0)


def _selection_bias(key_ref, bias_ref, t, n_ge, rows, n_chunks, topk):
    floor_t = jnp.maximum(t, jnp.int32(NEG_INF_KEY + 1))
    tie_rows = jnp.where((n_ge > topk) & (t > NEG_INF_KEY), 1.0, 0.0)
    has_tie = jnp.max(tie_rows) > 0.5

    @pl.when(jnp.logical_not(has_tie))
    def _():
        for c in range(n_chunks):
            sl = slice(c * LANES, (c + 1) * LANES)
            bias_ref[:, sl] = jnp.where(key_ref[:, sl] >= floor_t, 0.0, NEG_INF)

    @pl.when(has_tie)
    def _():
        ones = jnp.ones((LANES, LANES), MXU_DTYPE)
        r_i = lax.broadcasted_iota(I32, (LANES, LANES), 0)
        c_i = lax.broadcasted_iota(I32, (LANES, LANES), 1)
        strict_upper = jnp.where(r_i < c_i, 1.0, 0.0).astype(MXU_DTYPE)
        n_gt = jnp.zeros((rows, LANES), F32)
        for c in range(n_chunks):
            sl = slice(c * LANES, (c + 1) * LANES)
            n_gt = n_gt + _mm(jnp.where(key_ref[:, sl] > t, 1.0, 0.0).astype(MXU_DTYPE), ones)
        need = topk - n_gt
        run = jnp.zeros((rows, LANES), F32)
        for c in range(n_chunks):
            sl = slice(c * LANES, (c + 1) * LANES)
            k = key_ref[:, sl]
            eq = jnp.where(k == t, 1.0, 0.0).astype(MXU_DTYPE)
            before = run + _mm(eq, strict_upper)
            run = run + _mm(eq, ones)
            keep = (k > t) | ((k == t) & (before < need))
            bias_ref[:, sl] = jnp.where(keep & (k > NEG_INF_KEY), 0.0, NEG_INF)


def _count_rows(pred_fn, key_ref, n_keys):
    acc = jnp.zeros((LANES, LANES), F32)
    for c in range(n_keys // LANES):
        acc = acc + jnp.where(pred_fn(key_ref[c * LANES:(c + 1) * LANES, :]), 1.0, 0.0)
    return jnp.sum(acc, axis=0, keepdims=True)


def _kth_threshold_t(key_ref, n_keys, topk):
    def body(it, t):
        cand = t + lax.shift_left(jnp.int32(1), jnp.int32(31) - it)
        return jnp.where(_count_rows(lambda k: k >= cand, key_ref, n_keys) >= topk, cand, t)

    t = lax.fori_loop(0, 32, body, jnp.full((1, LANES), INT_MIN, I32))
    return t, _count_rows(lambda k: k >= t, key_ref, n_keys)


def _selection_bias_t(key_ref, store_bias, t, n_ge, n_keys, topk):
    floor_t = jnp.maximum(t, jnp.int32(NEG_INF_KEY + 1))
    tie_lanes = jnp.where((n_ge > topk) & (t > NEG_INF_KEY), 1.0, 0.0)
    has_tie = jnp.max(tie_lanes) > 0.5

    @pl.when(jnp.logical_not(has_tie))
    def _():
        for c in range(n_keys // LANES):
            sl = slice(c * LANES, (c + 1) * LANES)
            store_bias(sl, jnp.where(key_ref[sl, :] >= floor_t, 0.0, NEG_BIG))

    @pl.when(has_tie)
    def _():
        r_i = lax.broadcasted_iota(I32, (LANES, LANES), 0)
        c_i = lax.broadcasted_iota(I32, (LANES, LANES), 1)
        strict_lower = jnp.where(c_i < r_i, 1.0, 0.0).astype(MXU_DTYPE)
        need = topk - _count_rows(lambda k: k > t, key_ref, n_keys)
        run = jnp.zeros((1, LANES), F32)
        for c in range(n_keys // LANES):
            sl = slice(c * LANES, (c + 1) * LANES)
            k = key_ref[sl, :]
            eq = jnp.where(k == t, 1.0, 0.0)
            before = run + _mm(strict_lower, eq.astype(MXU_DTYPE))
            run = run + jnp.sum(eq, axis=0, keepdims=True)
            keep = (k > t) | ((k == t) & (before < need))
            store_bias(sl, jnp.where(keep & (k > NEG_INF_KEY), 0.0, NEG_BIG))


Q_BLOCK = 128
N_KV_CLASSES = 8


def _prompt_attn_kernel(*refs, topk, lam_init, q0):
    (qa_ref, ka_ref, vat_ref, qi_ref, wit_ref, ki_ref, qb_ref, kb_ref, vbt_ref, qm_ref, mem_ref,
     lq1, lk1, lq2, lk2, g_ref, _, o_ref, kab_s, kbb_s, km_s, vmt_s, key_ref) = refs
    n_keys = ki_ref.shape[0]
    n_mem = km_s.shape[1]
    qi = pl.program_id(1)
    key_lanes = slice(0, LANES)
    mask_lanes = slice(LANES, 2 * LANES)

    @pl.when(qi == 0)
    def _():
        for g in range(HKV_A):
            kab_s[g, :, key_lanes] = ka_ref[g]
        for h in range(H_B):
            kbb_s[h, :, key_lanes] = kb_ref[h]
        for h in range(H_MEM):
            km_s[h] = mem_ref[pl.ds(h, n_mem, stride=N_KVM), :].astype(MXU_DTYPE)
            vmt_s[h] = mem_ref[pl.ds(H_MEM + h, n_mem, stride=N_KVM), :].T.astype(MXU_DTYPE)

    q_pos = (q0 + qi) * Q_BLOCK + lax.broadcasted_iota(I32, (n_keys, Q_BLOCK), 1)
    k_pos = lax.broadcasted_iota(I32, (n_keys, Q_BLOCK), 0)
    causal = k_pos <= q_pos

    causal_mask = jnp.where(causal, 0.0, NEG_BIG).astype(MXU_DTYPE)

    def store_selection(rows, block):
        for g in range(HKV_A):
            kab_s[g, rows, mask_lanes] = block.astype(MXU_DTYPE)

    if n_keys <= topk:
        store_selection(slice(0, n_keys), causal_mask)
    else:
        ki = ki_ref[...]
        sc = jnp.zeros((n_keys, Q_BLOCK), F32)
        for hp in range(H_IDX // 2):
            q2 = jnp.concatenate([qi_ref[:, (2 * hp + j) * D_IDX:(2 * hp + j + 1) * D_IDX] for j in range(2)],
                                 axis=0).astype(MXU_DTYPE)
            s = _mm_nt(ki, q2)
            pair = (wit_ref[2 * hp:2 * hp + 1, :] * jnp.maximum(s[:, :Q_BLOCK], 0.0)
                    + wit_ref[2 * hp + 1:2 * hp + 2, :] * jnp.maximum(s[:, Q_BLOCK:], 0.0))
            sc = sc + pair
        key_ref[...] = _ordered_key(jnp.where(causal, sc, NEG_INF))
        t, n_ge = _kth_threshold_t(key_ref, n_keys, topk)
        _selection_bias_t(key_ref, store_selection, t, n_ge, n_keys, topk)

    eye = jnp.where(lax.broadcasted_iota(I32, (Q_BLOCK, Q_BLOCK), 0) == lax.broadcasted_iota(I32, (Q_BLOCK, Q_BLOCK), 1),
                    1.0, 0.0)

    def with_mask_selector(q_blocks, log2_scale):
        q = jnp.concatenate([qb_ * log2_scale for qb_ in q_blocks], axis=0)
        return jnp.concatenate([q, jnp.concatenate([eye] * len(q_blocks), axis=0)], axis=1).astype(MXU_DTYPE)

    def softmax_cols(s, n_cols):
        es, ls = [], []
        for r in range(n_cols):
            x = s[:, r * Q_BLOCK:(r + 1) * Q_BLOCK]
            e = jnp.exp2(x - jnp.max(x, axis=0, keepdims=True))
            es.append(e.astype(MXU_DTYPE))
            ls.append(jnp.sum(e, axis=0, keepdims=True))
        return jnp.concatenate(es, axis=1), ls

    rep = H_A // HKV_A
    for g in range(HKV_A):
        q4 = with_mask_selector([qa_ref[:, (g * rep + r) * DH_A:(g * rep + r + 1) * DH_A] for r in range(rep)],
                                DH_A ** -0.5 * LOG2_E)
        e, ls = softmax_cols(_mm_nt(kab_s[g], q4), rep)
        ot = _mm(vat_ref[g], e)
        for r in range(rep):
            h = g * rep + r
            o_ref[:, h * DH_A:(h + 1) * DH_A] = (ot[:, r * Q_BLOCK:(r + 1) * Q_BLOCK] / ls[r]).T.astype(o_ref.dtype)

    lam = _lambda_value(lq1, lk1, lq2, lk2, lam_init)
    off_b = H_A * DH_A
    for h in range(H_B):
        kbb_s[h, :, mask_lanes] = causal_mask
        qh = qb_ref[:, h * DV_B:(h + 1) * DV_B]
        q2 = with_mask_selector([_half_lanes(qh, 0), _half_lanes(qh, 1)], DQK_B ** -0.5 * LOG2_E)
        e, ls = softmax_cols(_mm_nt(kbb_s[h], q2), 2)
        ot = _mm(vbt_ref[h], e)
        o = (ot[:, :Q_BLOCK] / ls[0] - lam * (ot[:, Q_BLOCK:] / ls[1])).T
        o_ref[:, off_b + h * DV_B:off_b + (h + 1) * DV_B] = _diff_finish(o, g_ref, lam_init).astype(o_ref.dtype)

    off_m = off_b + H_B * DV_B
    for h in range(H_MEM):
        q = (qm_ref[:, h * DH_MEM:(h + 1) * DH_MEM] * (DH_MEM ** -0.5 * LOG2_E)).astype(MXU_DTYPE)
        e, ls = softmax_cols(_mm_nt(km_s[h], q), 1)
        o_ref[:, off_m + h * DH_MEM:off_m + (h + 1) * DH_MEM] = (_mm(vmt_s[h], e) / ls[0]).T.astype(o_ref.dtype)


def _prompt_attention(qa, ka, vat, qi, wit, ki, qb, kb, vbt, qm, mem_kv, lam_params, g, batch, seq, layer, lam_init):
    depth = lam_params[0].shape[0]
    n_q = seq // Q_BLOCK
    n_classes = min(N_KV_CLASSES, n_q)
    q_per_class = n_q // n_classes
    n_mem = mem_kv.shape[0] // (depth * batch * N_KVM)
    topk = min(INDEX_TOPK_MAX, seq // 4)
    mem3 = mem_kv.reshape(depth * batch, n_mem * N_KVM, LANES)
    vec_spec = pl.BlockSpec((None, 1, DQK_B), lambda b, i: (layer, 0, 0))

    mixed = jnp.zeros((batch * seq, D_MIX), MXU_DTYPE)
    for c in range(n_classes):
        q0 = c * q_per_class
        n_keys = (c + 1) * q_per_class * Q_BLOCK

        def q_spec(w, q0=q0):
            return pl.BlockSpec((Q_BLOCK, w), lambda b, i: (b * n_q + q0 + i, 0))

        def k_spec(heads):
            return pl.BlockSpec((None, heads, n_keys, LANES), lambda b, i: (b, 0, 0, 0))

        def vt_spec(heads):
            return pl.BlockSpec((None, heads, LANES, n_keys), lambda b, i: (b, 0, 0, 0))

        in_specs = [q_spec(W_QA), k_spec(HKV_A), vt_spec(HKV_A), q_spec(W_QI),
                    pl.BlockSpec((H_IDX, Q_BLOCK), lambda b, i, q0=q0: (0, b * n_q + q0 + i)),
                    pl.BlockSpec((None, n_keys, D_IDX), lambda b, i: (b, 0, 0)),
                    q_spec(W_QB), k_spec(H_B), vt_spec(H_B), q_spec(W_QM),
                    pl.BlockSpec((None, n_mem * N_KVM, LANES), lambda b, i: (layer * batch + b, 0, 0)),
                    vec_spec, vec_spec, vec_spec, vec_spec,
                    pl.BlockSpec((None, 1, DV_B), lambda b, i: (layer, 0, 0)),
                    pl.BlockSpec(memory_space=pl.ANY)]
        args = [qa, ka, vat, qi, wit, ki, qb, kb, vbt, qm, mem3, *lam_params, g, mixed]
        mixed = pl.pallas_call(
            functools.partial(_prompt_attn_kernel, topk=topk, lam_init=lam_init, q0=q0),
            grid=(batch, q_per_class),
            in_specs=in_specs,
            out_specs=q_spec(D_MIX),
            out_shape=jax.ShapeDtypeStruct((batch * seq, D_MIX), MXU_DTYPE),
            scratch_shapes=[pltpu.VMEM((HKV_A, n_keys, 2 * LANES), MXU_DTYPE),
                            pltpu.VMEM((H_B, n_keys, 2 * LANES), MXU_DTYPE),
                            pltpu.VMEM((H_MEM, n_mem, DH_MEM), MXU_DTYPE),
                            pltpu.VMEM((H_MEM, DH_MEM, n_mem), MXU_DTYPE),
                            pltpu.VMEM((n_keys, Q_BLOCK), I32)],
            input_output_aliases={len(args) - 1: 0},
            compiler_params=_cparams(("parallel", "arbitrary")),
            name=f"prompt_attention_kv{c}",
        )(*args)
    return mixed


def _pad_rows(x, rows):
    return jnp.concatenate([x, jnp.zeros((rows - x.shape[0], x.shape[1]), x.dtype)], axis=0)


def _sample_index_kernel(pt_ref, qi_ref, wcol_ref, kin_ref, *rest, n_pages, group):
    del pt_ref
    idx_pages = rest[:group * n_pages]
    key_o = rest[group * n_pages]
    t_new = qi_ref.shape[0] // group
    past = n_pages * PAGE_SIZE

    lane = lax.broadcasted_iota(I32, (t_new, LANES), 1)
    tok = lax.broadcasted_iota(I32, (t_new, LANES), 0)
    new_ok = lane <= tok

    for b in range(group):
        rows = slice(b * t_new, (b + 1) * t_new)
        q_ht = jnp.concatenate([qi_ref[rows, h * D_IDX:(h + 1) * D_IDX] for h in range(H_IDX)],
                               axis=0).astype(MXU_DTYPE)
        wcol = wcol_ref[b * H_IDX * t_new:(b + 1) * H_IDX * t_new, :]

        def index_scores(qk):
            s = jnp.maximum(qk, 0.0) * wcol
            acc = s[0:t_new]
            for h in range(1, H_IDX):
                acc = acc + s[h * t_new:(h + 1) * t_new]
            return acc

        for j in range(n_pages):
            qk = _mm(q_ht, idx_pages[b * n_pages + j][...].astype(MXU_DTYPE))
            key_o[rows, j * LANES:(j + 1) * LANES] = _ordered_key(index_scores(qk))
        s_new = index_scores(_mm_nt(q_ht, _pad_rows(kin_ref[rows, :], LANES).astype(MXU_DTYPE)))
        key_o[rows, past:past + LANES] = _ordered_key(jnp.where(new_ok, s_new, NEG_INF))


def _sample_index(page_table, qi, wcol, ki_new, pool_idx, batch, t_new, layer, n_phys, group):
    n_pages = page_table.shape[1]
    width = (n_pages + 1) * LANES

    def row_spec(rows, w):
        return pl.BlockSpec((rows, w), lambda s, pt: (s, 0))

    def page_spec(b, j):
        return pl.BlockSpec((None, D_IDX, PAGE_SIZE),
                            lambda s, pt: (layer * n_phys + pt[s * group + b, j], 0, 0))

    in_specs = ([row_spec(group * t_new, W_QI), row_spec(group * H_IDX * t_new, 1), row_spec(group * t_new, D_IDX)]
                + [page_spec(b, j) for b in range(group) for j in range(n_pages)])
    grid_spec = pltpu.PrefetchScalarGridSpec(
        num_scalar_prefetch=1, grid=(batch // group,), in_specs=in_specs,
        out_specs=pl.BlockSpec((group * t_new, width), lambda s, pt: (s, 0)))
    return pl.pallas_call(
        functools.partial(_sample_index_kernel, n_pages=n_pages, group=group),
        grid_spec=grid_spec,
        out_shape=jax.ShapeDtypeStruct((batch * t_new, width), I32),
        compiler_params=_cparams(("parallel",)),
        name="sample_indexer",
    )(page_table, qi, wcol, ki_new, *([pool_idx] * (group * n_pages)))


def _sample_select_kernel(key_ref, bias_ref, *, topk):
    rows, width = key_ref.shape
    n_chunks = width // LANES
    t, n_ge = _kth_threshold(key_ref, rows, n_chunks, topk, groups=4 if rows % 64 == 0 else 1)
    _selection_bias(key_ref, bias_ref, t, n_ge, rows, n_chunks, topk)


def _sample_select(keys, topk, tr):
    m, width = keys.shape
    spec = pl.BlockSpec((tr, width), lambda i: (i, 0))
    return pl.pallas_call(
        functools.partial(_sample_select_kernel, topk=topk),
        grid=(m // tr,),
        in_specs=[spec],
        out_specs=spec,
        out_shape=jax.ShapeDtypeStruct((m, width), F32),
        compiler_params=_cparams(("parallel",)),
        name="sample_topk_select",
    )(keys)


def _sample_attn_kernel(pt_ref, qa_ref, kvan_ref, bias_ref, qb_ref, kvbn_ref, qm_ref, mem_ref,
                        lq1, lk1, lq2, lk2, g_ref, *rest, n_pages, lam_init):
    del pt_ref
    kva_pages = rest[:n_pages]
    kvb_pages = rest[n_pages:2 * n_pages]
    o_ref, sa_ref, sb_ref = rest[2 * n_pages:]
    t_new = qa_ref.shape[0]
    past = n_pages * PAGE_SIZE

    lane = lax.broadcasted_iota(I32, (t_new, LANES), 1)
    tok = lax.broadcasted_iota(I32, (t_new, LANES), 0)
    new_ok = lane <= tok

    def chunk_rows(ref, j, rows, n):
        return ref[pl.ds(j, rows, stride=n), :].astype(MXU_DTYPE)

    def new_rows(ref, j, n):
        return _pad_rows(ref[pl.ds(j, t_new, stride=n), :], LANES).astype(MXU_DTYPE)

    scale_a = DH_A ** -0.5
    rep = H_A // HKV_A
    bias_a = jnp.concatenate([bias_ref[...]] * rep, axis=0)
    for g in range(HKV_A):
        q = jnp.concatenate([qa_ref[:, (g * rep + r) * DH_A:(g * rep + r + 1) * DH_A] for r in range(rep)],
                            axis=0).astype(MXU_DTYPE)
        for j in range(n_pages):
            sa_ref[:, j * LANES:(j + 1) * LANES] = _mm_nt(q, chunk_rows(kva_pages[j], g, PAGE_SIZE, N_KVA))
        sa_ref[:, past:past + LANES] = _mm_nt(q, new_rows(kvan_ref, g, N_KVA))
        e, l = _exp_and_sum(sa_ref[...] * scale_a + bias_a)
        e = e.astype(MXU_DTYPE)
        acc = _mm(e[:, past:past + LANES], new_rows(kvan_ref, HKV_A + g, N_KVA))
        for j in range(n_pages):
            acc = acc + _mm(e[:, j * LANES:(j + 1) * LANES], chunk_rows(kva_pages[j], HKV_A + g, PAGE_SIZE, N_KVA))
        acc = acc / l
        for r in range(rep):
            h = g * rep + r
            o_ref[:, h * DH_A:(h + 1) * DH_A] = acc[r * t_new:(r + 1) * t_new]

    lam = _lambda_value(lq1, lk1, lq2, lk2, lam_init)
    scale_b = DQK_B ** -0.5
    for h in range(H_B):
        qh = qb_ref[:, h * DV_B:(h + 1) * DV_B]
        q = jnp.concatenate([_half_lanes(qh, 0), _half_lanes(qh, 1)], axis=0).astype(MXU_DTYPE)
        rows = slice(2 * h * t_new, (2 * h + 2) * t_new)
        for j in range(n_pages):
            sb_ref[rows, j * LANES:(j + 1) * LANES] = _mm_nt(q, chunk_rows(kvb_pages[j], h, PAGE_SIZE, N_KVB))
        sb_ref[rows, past:past + LANES] = _mm_nt(q, new_rows(kvbn_ref, h, N_KVB))
    new_bias = jnp.where(new_ok, 0.0, NEG_INF)
    bias_b = jnp.concatenate(
        [jnp.zeros((N_KVB * t_new, past), F32), jnp.concatenate([new_bias] * N_KVB, axis=0)], axis=1)
    e, l = _exp_and_sum(sb_ref[...] * scale_b + bias_b)
    e = e.astype(MXU_DTYPE)
    off_b = H_A * DH_A
    for h in range(H_B):
        rows = slice(2 * h * t_new, (2 * h + 2) * t_new)
        acc = _mm(e[rows, past:past + LANES], new_rows(kvbn_ref, H_B + h, N_KVB))
        for j in range(n_pages):
            acc = acc + _mm(e[rows, j * LANES:(j + 1) * LANES], chunk_rows(kvb_pages[j], H_B + h, PAGE_SIZE, N_KVB))
        acc = acc / l[rows]
        o = acc[0:t_new] - lam * acc[t_new:2 * t_new]
        o_ref[:, off_b + h * DV_B:off_b + (h + 1) * DV_B] = _diff_finish(o, g_ref, lam_init)

    scale_m = DH_MEM ** -0.5
    off_m = off_b + H_B * DV_B
    n_mem = mem_ref.shape[0] // N_KVM
    for h in range(H_MEM):
        q = qm_ref[:, h * DH_MEM:(h + 1) * DH_MEM].astype(MXU_DTYPE)
        e, l = _exp_and_sum(_mm_nt(q, chunk_rows(mem_ref, h, n_mem, N_KVM)) * scale_m)
        o_ref[:, off_m + h * DH_MEM:off_m + (h + 1) * DH_MEM] = (
            _mm(e.astype(MXU_DTYPE), chunk_rows(mem_ref, H_MEM + h, n_mem, N_KVM)) / l)


def _sample_attention(page_table, qa, kva_new, bias, qb, kvb_new, qm, pool_kva, pool_kvb, pool_mem,
                      lam_params, g, batch, t_new, layer, lam_init, n_phys, mem_rows):
    n_pages = page_table.shape[1]
    width = (n_pages + 1) * LANES

    def row_spec(rows, w):
        return pl.BlockSpec((rows, w), lambda b, pt: (b, 0))

    def page_spec(j, n):
        return pl.BlockSpec((PAGE_SIZE * n, LANES), lambda b, pt: (layer * n_phys + pt[b, j], 0))

    vec_spec = pl.BlockSpec((None, 1, DQK_B), lambda b, pt: (layer, 0, 0))
    in_specs = ([row_spec(t_new, W_QA), row_spec(t_new * N_KVA, LANES), row_spec(t_new, width),
                 row_spec(t_new, W_QB), row_spec(t_new * N_KVB, LANES), row_spec(t_new, W_QM),
                 pl.BlockSpec((mem_rows, LANES), lambda b, pt: (layer * batch + b, 0)),
                 vec_spec, vec_spec, vec_spec, vec_spec,
                 pl.BlockSpec((None, 1, DV_B), lambda b, pt: (layer, 0, 0))]
                + [page_spec(j, N_KVA) for j in range(n_pages)]
                + [page_spec(j, N_KVB) for j in range(n_pages)])
    grid_spec = pltpu.PrefetchScalarGridSpec(
        num_scalar_prefetch=1,
        grid=(batch,),
        in_specs=in_specs,
        out_specs=pl.BlockSpec((t_new, D_MIX), lambda b, pt: (b, 0)),
        scratch_shapes=[pltpu.VMEM((H_A // HKV_A * t_new, width), F32),
                        pltpu.VMEM((N_KVB * t_new, width), F32)],
    )
    return pl.pallas_call(
        functools.partial(_sample_attn_kernel, n_pages=n_pages, lam_init=lam_init),
        grid_spec=grid_spec,
        out_shape=jax.ShapeDtypeStruct((batch * t_new, D_MIX), F32),
        compiler_params=_cparams(("parallel",)),
        name="sample_attention",
    )(page_table, qa, kva_new, bias, qb, kvb_new, qm, pool_mem, *lam_params, g,
      *([pool_kva] * n_pages), *([pool_kvb] * n_pages))


def _attn_out_kernel(x_ref, mix_ref, wo_ref, g_ref, b_ref, h_ref, *, alpha):
    half = x_ref.shape[0] // 2
    for rows in (slice(0, half), slice(half, 2 * half)):
        a = _mm(mix_ref[rows, :].astype(MXU_DTYPE), wo_ref[...])
        h_ref[rows, :] = _layer_norm(alpha * x_ref[rows, :] + a, g_ref[...], b_ref[...])


def _attn_out(x2d, mixed, w_o, ln_g, ln_b, layer, alpha, tm):
    m, d = x2d.shape
    row = pl.BlockSpec((tm, d), lambda i: (i, 0))
    vec = pl.BlockSpec((None, 1, d), lambda i: (layer, 0, 0))
    return pl.pallas_call(
        functools.partial(_attn_out_kernel, alpha=alpha),
        grid=(m // tm,),
        in_specs=[row, pl.BlockSpec((tm, mixed.shape[1]), lambda i: (i, 0)),
                  pl.BlockSpec((None, mixed.shape[1], d), lambda i: (layer, 0, 0), pipeline_mode=pl.Buffered(1)),
                  vec, vec],
        out_specs=row,
        out_shape=jax.ShapeDtypeStruct((m, d), F32),
        compiler_params=_cparams(("parallel",)),
        name="attn_out_ln",
    )(x2d, mixed, w_o, ln_g, ln_b)


def _ffn_kernel(h_ref, wg_ref, wu_ref, wd_ref, g_ref, b_ref, o_ref, hb_ref, acc_ref, *, alpha):
    j = pl.program_id(1)

    @pl.when(j == 0)
    def _():
        hb_ref[...] = h_ref[...].astype(MXU_DTYPE)
        acc_ref[...] = jnp.zeros_like(acc_ref)

    hb = hb_ref[...]
    gate = _mm(hb, wg_ref[...])
    up = _mm(hb, wu_ref[...])
    act = gate * jax.nn.sigmoid(gate) * up
    acc_ref[...] += _mm(act.astype(MXU_DTYPE), wd_ref[...])

    @pl.when(j == pl.num_programs(1) - 1)
    def _():
        o_ref[...] = _layer_norm(alpha * h_ref[...] + acc_ref[...], g_ref[...], b_ref[...])


def _ffn(h2d, w_gate, w_up, w_down, ln_g, ln_b, layer, alpha, tm, tf):
    m, d = h2d.shape
    f = w_gate.shape[-1]
    row = pl.BlockSpec((tm, d), lambda i, j: (i, 0))
    vec = pl.BlockSpec((None, 1, d), lambda i, j: (layer, 0, 0))
    return pl.pallas_call(
        functools.partial(_ffn_kernel, alpha=alpha),
        grid=(m // tm, f // tf),
        in_specs=[row,
                  pl.BlockSpec((None, d, tf), lambda i, j: (layer, 0, j)),
                  pl.BlockSpec((None, d, tf), lambda i, j: (layer, 0, j)),
                  pl.BlockSpec((None, tf, d), lambda i, j: (layer, j, 0)),
                  vec, vec],
        out_specs=row,
        out_shape=jax.ShapeDtypeStruct((m, d), F32),
        scratch_shapes=[pltpu.VMEM((tm, d), MXU_DTYPE), pltpu.VMEM((tm, d), F32)],
        compiler_params=_cparams(("parallel", "arbitrary")),
        name="swiglu_ln",
    )(h2d, w_gate, w_up, w_down, ln_g, ln_b)


def _row_tile(m, cap):
    t = min(m, cap)
    while m % t:
        t //= 2
    return t


def _ff_tile(f, cap):
    best = LANES
    for t in range(LANES, cap + 1, LANES):
        if f % t == 0:
            best = t
    return best


def _largest_divisor(n, cap):
    return max(d for d in range(1, cap + 1) if n % d == 0)


def kernel(x_prompt, x_sample, mem_prompt, cache_kv_a, cache_idx_k, cache_kv_b, cache_mem_kv, page_table,
           w_in, w_mem_kv, lambda_q1, lambda_k1, lambda_q2, lambda_k2, subln_g, w_o,
           ln1_g, ln1_b, w_gate, w_up, w_down, ln2_g, ln2_b):
    depth, d_model, _ = w_in.shape
    batch, seq, _ = x_prompt.shape
    dec_batch, t_new, _ = x_sample.shape
    n_pages = page_table.shape[1]
    past = n_pages * PAGE_SIZE
    n_mem = mem_prompt.shape[1]
    n_phys = cache_idx_k.shape[1]
    alpha = (2.0 * depth) ** 0.25

    n_main = W_QA + W_KVA + W_QB + W_KVB
    w_in_p = jnp.concatenate(
        [w_in[:, :, :n_main], w_in[:, :, n_main + W_QM:n_main + W_QM + W_QI], w_in[:, :, n_main:n_main + W_QM],
         w_in[:, :, n_main + W_QM + W_QI:],
         jnp.zeros((depth, d_model, N_IN_PAD - w_in.shape[2]), w_in.dtype)], axis=-1).astype(MXU_DTYPE)
    w_mem_c = w_mem_kv.astype(MXU_DTYPE)
    w_o_c, w_gate_c, w_up_c, w_down_c = (w.astype(MXU_DTYPE) for w in (w_o, w_gate, w_up, w_down))
    vec3 = lambda a: a.astype(F32).reshape(depth, 1, a.shape[-1])
    lam_params = tuple(vec3(a) for a in (lambda_q1, lambda_k1, lambda_q2, lambda_k2))
    g_sub = vec3(subln_g)
    ln1g, ln1b, ln2g, ln2b = (vec3(a) for a in (ln1_g, ln1_b, ln2_g, ln2_b))

    pool_kva = cache_kv_a.reshape(-1, LANES)
    pool_idx = jnp.swapaxes(cache_idx_k, 2, 3).reshape(depth * n_phys, D_IDX, PAGE_SIZE)
    pool_kvb = cache_kv_b.reshape(-1, LANES)
    pool_mem = cache_mem_kv.reshape(-1, LANES)

    tm_p = _row_tile(seq, 512)
    m_s = dec_batch * t_new
    tm_s = _row_tile(m_s, 512)
    pos_p = jnp.arange(seq, dtype=I32)
    pos_s = past + (jnp.arange(tm_s, dtype=I32) % t_new)
    tabs_p = _rope_tables(pos_p, DH_A) + _rope_tables(pos_p, DQK_B)
    tabs_s = _rope_tables(pos_s, DH_A) + _rope_tables(pos_s, DQK_B)

    xp = x_prompt.reshape(batch * seq, d_model)
    xs = x_sample.reshape(m_s, d_model)
    mem2d = mem_prompt.reshape(batch * n_mem, d_model)
    tf = _ff_tile(w_gate.shape[-1], 512)
    topk_s = min(INDEX_TOPK_MAX, (past + t_new) // 4)
    idx_group = _largest_divisor(dec_batch, 4)

    outs = [[] for _ in range(3)]
    m_p = batch * seq
    caches = tuple(jnp.zeros((depth * m_p * r, w), F32) for r, w in ((N_KVA, LANES), (N_KVB, LANES), (1, D_IDX)))
    mem_kv = jnp.zeros((depth * batch * n_mem * N_KVM, LANES), F32)
    for l in range(depth):
        lam_init = 0.8 - 0.6 * math.exp(-0.3 * l)

        qa, kva, qb, kvb, qi, qm, ki, wi, ka_t, vat_t, kb_t, vbt_t, ki_t = _project(
            xp, w_in_p, l, tabs_p, seq // tm_p, tm_p, cache_slots=depth, prev_caches=caches, attn_batch=batch)
        caches = (kva, kvb, ki)
        mem_kv = _matmul(mem2d, w_mem_c, l, _row_tile(batch * n_mem, 512), prev=mem_kv)
        mixed = _prompt_attention(qa, ka_t, vat_t, qi, wi.T, ki_t, qb, kb_t, vbt_t, qm, mem_kv, lam_params, g_sub,
                                  batch, seq, l, lam_init)
        h = _attn_out(xp, mixed, w_o_c, ln1g, ln1b, l, alpha, tm_p)
        xp = _ffn(h, w_gate_c, w_up_c, w_down_c, ln2g, ln2b, l, alpha, tm_p, tf)

        qa, kva, qb, kvb, qi, qm, ki, wi = _project(xs, w_in_p, l, tabs_s, 1, tm_s)
        wcol = wi.reshape(dec_batch, t_new, H_IDX).transpose(0, 2, 1).reshape(dec_batch * H_IDX * t_new, 1)
        keys = _sample_index(page_table, qi, wcol, ki, pool_idx, dec_batch, t_new, l, n_phys, idx_group)
        bias = _sample_select(keys, topk_s, _row_tile(m_s, 256))
        mixed = _sample_attention(page_table, qa, kva, bias, qb, kvb, qm, pool_kva, pool_kvb, pool_mem,
                                  lam_params, g_sub, dec_batch, t_new, l, lam_init, n_phys, n_mem * N_KVM)
        h = _attn_out(xs, mixed, w_o_c, ln1g, ln1b, l, alpha, tm_s)
        xs = _ffn(h, w_gate_c, w_up_c, w_down_c, ln2g, ln2b, l, alpha, tm_s, tf)
        outs[0].append(kva.reshape(dec_batch, t_new, 2, HKV_A, DH_A))
        outs[1].append(ki.reshape(dec_batch, t_new, D_IDX))
        outs[2].append(kvb.reshape(dec_batch, t_new, 2, H_B, DV_B))

    kva_p, kvb_p, ki_p = caches
    return (xp.reshape(batch, seq, d_model), xs.reshape(dec_batch, t_new, d_model),
            kva_p.reshape(depth, batch, seq, 2, HKV_A, DH_A), ki_p.reshape(depth, batch, seq, D_IDX),
            kvb_p.reshape(depth, batch, seq, 2, H_B, DV_B), mem_kv.reshape(depth, batch, n_mem, 2, H_MEM, DH_MEM)
            ) + tuple(jnp.stack(o) for o in outs)
```

```python
import functools
import math

import jax
import jax.numpy as jnp
from jax import lax
from jax.experimental import pallas as pl
from jax.experimental.pallas import tpu as pltpu

F32 = jnp.float32
I32 = jnp.int32
MXU_DTYPE = jnp.bfloat16

H_A, DH_A, HKV_A = 8, 128, 2
H_IDX, D_IDX = 16, 64
INDEX_TOPK_MAX = 256
H_B, DQK_B, DV_B = 4, 64, 128
H_MEM, DH_MEM = 4, 128
PAGE_SIZE = 128
ROPE_THETA = 500000.0
ROPE_DIV = 4
LN_EPS = 1e-5
RMS_EPS = 1e-5
D_MIX = H_A * DH_A + H_B * DV_B + H_MEM * DH_MEM
N_KVA, N_KVB, N_KVM = 2 * HKV_A, 2 * H_B, 2 * H_MEM

LANES = 128
VMEM_LIMIT_BYTES = 56 * 1024 * 1024

W_QA, W_KVA, W_QB, W_KVB, W_QI, W_QM = 1024, 512, 512, 1024, 1024, 512
C_QA, C_KVA, C_QB, C_KVB, C_QI, C_QM, C_TAIL = 0, 8, 12, 16, 24, 32, 36
N_IN_PAD = 37 * LANES

LOG2_E = math.log2(math.e)
NEG_INF = float("-inf")
NEG_BIG = -1e30
INT_MIN = -(2 ** 31)
NEG_INF_KEY = -2139095041


def _cparams(sem):
    return pltpu.CompilerParams(dimension_semantics=sem, vmem_limit_bytes=VMEM_LIMIT_BYTES)


def _mm(a, b):
    return jnp.dot(a, b, preferred_element_type=F32)


def _mm_nt(a, b):
    return lax.dot_general(a, b, (((1,), (1,)), ((), ())), preferred_element_type=F32)


def _layer_norm(y, g, b):
    mu = jnp.mean(y, axis=-1, keepdims=True)
    d = y - mu
    var = jnp.mean(d * d, axis=-1, keepdims=True)
    return d * lax.rsqrt(var + LN_EPS) * g + b


def _exp_and_sum(s, axis=-1):
    m = jnp.max(s, axis=axis, keepdims=True)
    e = jnp.exp(s - m)
    return e, jnp.sum(e, axis=axis, keepdims=True)


def _ordered_key(x):
    k = pltpu.bitcast(x, I32)
    return jnp.where(k < 0, k ^ jnp.int32(0x7FFFFFFF), k)


def _half_lanes(x, c):
    lane = lax.broadcasted_iota(I32, x.shape, 1)
    return jnp.where((lane >= c * DQK_B) & (lane < (c + 1) * DQK_B), x, 0.0)


def _lambda_value(lq1, lk1, lq2, lk2, lam_init):
    a = jnp.sum(lq1[...] * lk1[...], axis=-1, keepdims=True)
    b = jnp.sum(lq2[...] * lk2[...], axis=-1, keepdims=True)
    return jnp.exp(a) - jnp.exp(b) + lam_init


def _diff_finish(o, g_ref, lam_init):
    o = o * lax.rsqrt(jnp.mean(o * o, axis=-1, keepdims=True) + RMS_EPS)
    return o * g_ref[...] * (1.0 - lam_init)


def _rope_tables(pos, head_dim):
    rot = head_dim // ROPE_DIV
    half = rot // 2
    t = pos.shape[0]
    inv_freq = jnp.float32(ROPE_THETA) ** (-jnp.arange(half, dtype=F32) / half)
    ang = pos.astype(F32)[:, None] * inv_freq[None, :]
    cos, sin = jnp.cos(ang), jnp.sin(ang)
    c = jnp.concatenate([cos, cos, jnp.ones((t, head_dim - rot), F32)], axis=-1)
    s_hi = jnp.concatenate([-sin, jnp.zeros((t, head_dim - half), F32)], axis=-1)
    s_lo = jnp.concatenate([jnp.zeros((t, half), F32), sin, jnp.zeros((t, head_dim - rot), F32)], axis=-1)
    rep = LANES // head_dim
    return tuple(jnp.tile(a, (1, rep)) for a in (c, s_hi, s_lo))


def _proj_kernel(x_ref, w_ref, ca, sa1, sa2, cb, sb1, sb2, *rest, n_prev, attn_layout):
    qa_o, kva_o, qb_o, kvb_o, qi_o, qm_o, ki_o, wi_o = rest[n_prev:n_prev + 8]
    ka_t, vat_t, kb_t, vbt_t, ki_t = rest[n_prev + 8:] if attn_layout else (None,) * 5
    xb = x_ref[...].astype(MXU_DTYPE)
    tm = x_ref.shape[0]

    def rope(z, c, s_hi, s_lo, half):
        return z * c[...] + pltpu.roll(z, LANES - half, 1) * s_hi[...] + pltpu.roll(z, half, 1) * s_lo[...]

    def rope_a(z):
        return rope(z, ca, sa1, sa2, DH_A // ROPE_DIV // 2)

    def rope_b(z):
        return rope(z, cb, sb1, sb2, DQK_B // ROPE_DIV // 2)

    def emit(out_ref, c0, kinds, interleave=False, key_t=None, value_t=None):
        n = len(kinds)
        z = _mm_nt(xb, w_ref[c0 * LANES:(c0 + n) * LANES, :])
        for j, kind in enumerate(kinds):
            zj = z[:, j * LANES:(j + 1) * LANES]
            if kind == "a":
                zj = rope_a(zj)
            elif kind == "b":
                zj = rope_b(zj)
            if interleave:
                out_ref[pl.ds(j, tm, stride=n), :] = zj
                if key_t is not None:
                    if j < n // 2:
                        key_t[j] = zj.astype(MXU_DTYPE)
                    else:
                        value_t[j - n // 2] = zj.T.astype(MXU_DTYPE)
            else:
                out_ref[:, j * LANES:(j + 1) * LANES] = zj

    emit(qa_o, C_QA, "a" * 8)
    emit(kva_o, C_KVA, "aa--", interleave=True, key_t=ka_t, value_t=vat_t)
    emit(qb_o, C_QB, "bbbb")
    emit(kvb_o, C_KVB, "bbbb----", interleave=True, key_t=kb_t, value_t=vbt_t)
    emit(qi_o, C_QI, "b" * 8)
    emit(qm_o, C_QM, "----")
    z = _mm_nt(xb, w_ref[C_TAIL * LANES:(C_TAIL + 1) * LANES, :])
    lane = lax.broadcasted_iota(I32, z.shape, 1)
    zt = jnp.where(lane < D_IDX, rope_b(z), z)
    ki_o[...] = zt[:, :D_IDX]
    wi_o[...] = zt[:, D_IDX:D_IDX + H_IDX]
    if ki_t is not None:
        ki_t[...] = zt[:, :D_IDX].astype(MXU_DTYPE)


CACHE_OUTPUTS = (1, 3, 6)


def _project(x2d, w_in_p, layer, tabs, n_tab_blocks, tm, cache_slots=1, prev_caches=None, attn_batch=None):
    m, d = x2d.shape
    n_blocks = m // tm
    shapes = ((1, W_QA), (N_KVA, LANES), (1, W_QB), (N_KVB, LANES), (1, W_QI), (1, W_QM), (1, D_IDX), (1, H_IDX))
    tab_spec = pl.BlockSpec((tm, LANES), lambda i: (i % n_tab_blocks, 0))
    slot = layer if cache_slots > 1 else 0
    out_specs, out_shape = [], []
    for k, (r, w) in enumerate(shapes):
        slots, base = (cache_slots, slot * n_blocks) if k in CACHE_OUTPUTS else (1, 0)
        out_specs.append(pl.BlockSpec((tm * r, w), lambda i, base=base: (base + i, 0)))
        out_shape.append(jax.ShapeDtypeStruct((slots * m * r, w), F32))
    if attn_batch is not None:
        seq = m // attn_batch
        per_seq = seq // tm
        for heads in (HKV_A, H_B):
            out_specs.append(pl.BlockSpec((None, heads, tm, LANES), lambda i: (i // per_seq, 0, i % per_seq, 0)))
            out_shape.append(jax.ShapeDtypeStruct((attn_batch, heads, seq, LANES), MXU_DTYPE))
            out_specs.append(pl.BlockSpec((None, heads, LANES, tm), lambda i: (i // per_seq, 0, 0, i % per_seq)))
            out_shape.append(jax.ShapeDtypeStruct((attn_batch, heads, LANES, seq), MXU_DTYPE))
        out_specs.append(pl.BlockSpec((None, tm, D_IDX), lambda i: (i // per_seq, i % per_seq, 0)))
        out_shape.append(jax.ShapeDtypeStruct((attn_batch, seq, D_IDX), MXU_DTYPE))
    in_specs = ([pl.BlockSpec((tm, d), lambda i: (i, 0)),
                 pl.BlockSpec((None, N_IN_PAD, d), lambda i: (layer, 0, 0), pipeline_mode=pl.Buffered(1))]
                + [tab_spec] * 6)
    args = [x2d, w_in_p, *tabs]
    aliases = {}
    n_prev = 0
    if prev_caches is not None:
        n_prev = len(CACHE_OUTPUTS)
        aliases = {len(args) + n: k for n, k in enumerate(CACHE_OUTPUTS)}
        in_specs += [pl.BlockSpec(memory_space=pl.ANY)] * n_prev
        args += list(prev_caches)
    return pl.pallas_call(
        functools.partial(_proj_kernel, n_prev=n_prev, attn_layout=attn_batch is not None),
        grid=(n_blocks,),
        in_specs=in_specs,
        out_specs=out_specs,
        out_shape=out_shape,
        input_output_aliases=aliases,
        compiler_params=_cparams(("parallel",)),
        name="proj_rope",
    )(*args)


def _matmul_kernel(x_ref, w_ref, *rest):
    o_ref = rest[-1]
    tm = x_ref.shape[0]
    n = w_ref.shape[1] // LANES
    z = _mm(x_ref[...].astype(MXU_DTYPE), w_ref[...])
    for j in range(n):
        o_ref[pl.ds(j, tm, stride=n), :] = z[:, j * LANES:(j + 1) * LANES]


def _matmul(x2d, w, layer, tm, prev=None):
    m, d = x2d.shape
    depth, _, n = w.shape
    n_blocks = m // tm
    in_specs = [pl.BlockSpec((tm, d), lambda i: (i, 0)),
                pl.BlockSpec((None, d, n), lambda i: (layer, 0, 0), pipeline_mode=pl.Buffered(1))]
    args = [x2d, w]
    aliases = {}
    if prev is not None:
        in_specs.append(pl.BlockSpec(memory_space=pl.ANY))
        aliases = {len(args): 0}
        args.append(prev)
    return pl.pallas_call(
        _matmul_kernel,
        grid=(n_blocks,),
        in_specs=in_specs,
        out_specs=pl.BlockSpec((tm * n // LANES, LANES), lambda i: (layer * n_blocks + i, 0)),
        out_shape=jax.ShapeDtypeStruct((depth * m * n // LANES, LANES), F32),
        input_output_aliases=aliases,
        compiler_params=_cparams(("parallel",)),
        name="mem_kv_proj",
    )(*args)


def _kth_threshold(key_ref, rows, n_chunks, topk, groups=1):
    ones = jnp.ones((LANES, LANES), MXU_DTYPE)
    gr = rows // groups

    def count_ge(cand, g):
        acc = jnp.zeros((gr, LANES), F32)
        for c in range(n_chunks):
            acc = acc + jnp.where(key_ref[g * gr:(g + 1) * gr, c * LANES:(c + 1) * LANES] >= cand, 1.0, 0.0)
        return _mm(acc.astype(MXU_DTYPE), ones)

    def body(it, ts):
        bit = lax.shift_left(jnp.int32(1), jnp.int32(31) - it)
        return tuple(jnp.where(count_ge(t + bit, g) >= topk, t + bit, t) for g, t in enumerate(ts))

    ts = lax.fori_loop(0, 32, body, tuple(jnp.full((gr, LANES), INT_MIN, I32) for _ in range(groups)))
    n_ge = [count_ge(t, g) for g, t in enumerate(ts)]
    return jnp.concatenate(ts, axis=0), jnp.concatenate(n_ge, axis=0)


def _selection_bias(key_ref, bias_ref, t, n_ge, rows, n_chunks, topk):
    floor_t = jnp.maximum(t, jnp.int32(NEG_INF_KEY + 1))
    tie_rows = jnp.where((n_ge > topk) & (t > NEG_INF_KEY), 1.0, 0.0)
    has_tie = jnp.max(tie_rows) > 0.5

    @pl.when(jnp.logical_not(has_tie))
    def _():
        for c in range(n_chunks):
            sl = slice(c * LANES, (c + 1) * LANES)
            bias_ref[:, sl] = jnp.where(key_ref[:, sl] >= floor_t, 0.0, NEG_INF)

    @pl.when(has_tie)
    def _():
        ones = jnp.ones((LANES, LANES), MXU_DTYPE)
        r_i = lax.broadcasted_iota(I32, (LANES, LANES), 0)
        c_i = lax.broadcasted_iota(I32, (LANES, LANES), 1)
        strict_upper = jnp.where(r_i < c_i, 1.0, 0.0).astype(MXU_DTYPE)
        n_gt = jnp.zeros((rows, LANES), F32)
        for c in range(n_chunks):
            sl = slice(c * LANES, (c + 1) * LANES)
            n_gt = n_gt + _mm(jnp.where(key_ref[:, sl] > t, 1.0, 0.0).astype(MXU_DTYPE), ones)
        need = topk - n_gt
        run = jnp.zeros((rows, LANES), F32)
        for c in range(n_chunks):
            sl = slice(c * LANES, (c + 1) * LANES)
            k = key_ref[:, sl]
            eq = jnp.where(k == t, 1.0, 0.0).astype(MXU_DTYPE)
            before = run + _mm(eq, strict_upper)
            run = run + _mm(eq, ones)
            keep = (k > t) | ((k == t) & (before < need))
            bias_ref[:, sl] = jnp.where(keep & (k > NEG_INF_KEY), 0.0, NEG_INF)


def _count_rows(pred_fn, key_ref, n_keys):
    acc = jnp.zeros((LANES, LANES), F32)
    for c in range(n_keys // LANES):
        acc = acc + jnp.where(pred_fn(key_ref[c * LANES:(c + 1) * LANES, :]), 1.0, 0.0)
    return jnp.sum(acc, axis=0, keepdims=True)


def _kth_threshold_t(key_ref, n_keys, topk):
    def body(it, t):
        cand = t + lax.shift_left(jnp.int32(1), jnp.int32(31) - it)
        return jnp.where(_count_rows(lambda k: k >= cand, key_ref, n_keys) >= topk, cand, t)

    t = lax.fori_loop(0, 32, body, jnp.full((1, LANES), INT_MIN, I32))
    return t, _count_rows(lambda k: k >= t, key_ref, n_keys)


def _selection_bias_t(key_ref, store_bias, t, n_ge, n_keys, topk):
    floor_t = jnp.maximum(t, jnp.int32(NEG_INF_KEY + 1))
    tie_lanes = jnp.where((n_ge > topk) & (t > NEG_INF_KEY), 1.0, 0.0)
    has_tie = jnp.max(tie_lanes) > 0.5

    @pl.when(jnp.logical_not(has_tie))
    def _():
        for c in range(n_keys // LANES):
            sl = slice(c * LANES, (c + 1) * LANES)
            store_bias(sl, jnp.where(key_ref[sl, :] >= floor_t, 0.0, NEG_BIG))

    @pl.when(has_tie)
    def _():
        r_i = lax.broadcasted_iota(I32, (LANES, LANES), 0)
        c_i = lax.broadcasted_iota(I32, (LANES, LANES), 1)
        strict_lower = jnp.where(c_i < r_i, 1.0, 0.0).astype(MXU_DTYPE)
        need = topk - _count_rows(lambda k: k > t, key_ref, n_keys)
        run = jnp.zeros((1, LANES), F32)
        for c in range(n_keys // LANES):
            sl = slice(c * LANES, (c + 1) * LANES)
            k = key_ref[sl, :]
            eq = jnp.where(k == t, 1.0, 0.0)
            before = run + _mm(strict_lower, eq.astype(MXU_DTYPE))
            run = run + jnp.sum(eq, axis=0, keepdims=True)
            keep = (k > t) | ((k == t) & (before < need))
            store_bias(sl, jnp.where(keep & (k > NEG_INF_KEY), 0.0, NEG_BIG))


Q_BLOCK = 128
N_KV_CLASSES = 8


def _prompt_attn_kernel(*refs, topk, lam_init, q0):
    (qa_ref, ka_ref, vat_ref, qi_ref, wit_ref, ki_ref, qb_ref, kb_ref, vbt_ref, qm_ref, mem_ref,
     lq1, lk1, lq2, lk2, g_ref, _, o_ref, kab_s, kbb_s, km_s, vmt_s, key_ref) = refs
    n_keys = ki_ref.shape[0]
    n_mem = km_s.shape[1]
    qi = pl.program_id(1)
    key_lanes = slice(0, LANES)
    mask_lanes = slice(LANES, 2 * LANES)

    @pl.when(qi == 0)
    def _():
        for g in range(HKV_A):
            kab_s[g, :, key_lanes] = ka_ref[g]
        for h in range(H_B):
            kbb_s[h, :, key_lanes] = kb_ref[h]
        for h in range(H_MEM):
            km_s[h] = mem_ref[pl.ds(h, n_mem, stride=N_KVM), :].astype(MXU_DTYPE)
            vmt_s[h] = mem_ref[pl.ds(H_MEM + h, n_mem, stride=N_KVM), :].T.astype(MXU_DTYPE)

    q_pos = (q0 + qi) * Q_BLOCK + lax.broadcasted_iota(I32, (n_keys, Q_BLOCK), 1)
    k_pos = lax.broadcasted_iota(I32, (n_keys, Q_BLOCK), 0)
    causal = k_pos <= q_pos

    causal_mask = jnp.where(causal, 0.0, NEG_BIG).astype(MXU_DTYPE)

    def store_selection(rows, block):
        for g in range(HKV_A):
            kab_s[g, rows, mask_lanes] = block.astype(MXU_DTYPE)

    if n_keys <= topk:
        store_selection(slice(0, n_keys), causal_mask)
    else:
        ki = ki_ref[...]
        sc = jnp.zeros((n_keys, Q_BLOCK), F32)
        for hp in range(H_IDX // 2):
            q2 = jnp.concatenate([qi_ref[:, (2 * hp + j) * D_IDX:(2 * hp + j + 1) * D_IDX] for j in range(2)],
                                 axis=0).astype(MXU_DTYPE)
            s = _mm_nt(ki, q2)
            pair = (wit_ref[2 * hp:2 * hp + 1, :] * jnp.maximum(s[:, :Q_BLOCK], 0.0)
                    + wit_ref[2 * hp + 1:2 * hp + 2, :] * jnp.maximum(s[:, Q_BLOCK:], 0.0))
            sc = sc + pair
        key_ref[...] = _ordered_key(jnp.where(causal, sc, NEG_INF))
        t, n_ge = _kth_threshold_t(key_ref, n_keys, topk)
        _selection_bias_t(key_ref, store_selection, t, n_ge, n_keys, topk)

    eye = jnp.where(lax.broadcasted_iota(I32, (Q_BLOCK, Q_BLOCK), 0) == lax.broadcasted_iota(I32, (Q_BLOCK, Q_BLOCK), 1),
                    1.0, 0.0)

    def with_mask_selector(q_blocks, log2_scale):
        q = jnp.concatenate([qb_ * log2_scale for qb_ in q_blocks], axis=0)
        return jnp.concatenate([q, jnp.concatenate([eye] * len(q_blocks), axis=0)], axis=1).astype(MXU_DTYPE)

    def softmax_cols(s, n_cols):
        es, ls = [], []
        for r in range(n_cols):
            x = s[:, r * Q_BLOCK:(r + 1) * Q_BLOCK]
            e = jnp.exp2(x - jnp.max(x, axis=0, keepdims=True))
            es.append(e.astype(MXU_DTYPE))
            ls.append(jnp.sum(e, axis=0, keepdims=True))
        return jnp.concatenate(es, axis=1), ls

    rep = H_A // HKV_A
    for g in range(HKV_A):
        q4 = with_mask_selector([qa_ref[:, (g * rep + r) * DH_A:(g * rep + r + 1) * DH_A] for r in range(rep)],
                                DH_A ** -0.5 * LOG2_E)
        e, ls = softmax_cols(_mm_nt(kab_s[g], q4), rep)
        ot = _mm(vat_ref[g], e)
        for r in range(rep):
            h = g * rep + r
            o_ref[:, h * DH_A:(h + 1) * DH_A] = (ot[:, r * Q_BLOCK:(r + 1) * Q_BLOCK] / ls[r]).T.astype(o_ref.dtype)

    lam = _lambda_value(lq1, lk1, lq2, lk2, lam_init)
    off_b = H_A * DH_A
    for h in range(H_B):
        kbb_s[h, :, mask_lanes] = causal_mask
        qh = qb_ref[:, h * DV_B:(h + 1) * DV_B]
        q2 = with_mask_selector([_half_lanes(qh, 0), _half_lanes(qh, 1)], DQK_B ** -0.5 * LOG2_E)
        e, ls = softmax_cols(_mm_nt(kbb_s[h], q2), 2)
        ot = _mm(vbt_ref[h], e)
        o = (ot[:, :Q_BLOCK] / ls[0] - lam * (ot[:, Q_BLOCK:] / ls[1])).T
        o_ref[:, off_b + h * DV_B:off_b + (h + 1) * DV_B] = _diff_finish(o, g_ref, lam_init).astype(o_ref.dtype)

    off_m = off_b + H_B * DV_B
    for h in range(H_MEM):
        q = (qm_ref[:, h * DH_MEM:(h + 1) * DH_MEM] * (DH_MEM ** -0.5 * LOG2_E)).astype(MXU_DTYPE)
        e, ls = softmax_cols(_mm_nt(km_s[h], q), 1)
        o_ref[:, off_m + h * DH_MEM:off_m + (h + 1) * DH_MEM] = (_mm(vmt_s[h], e) / ls[0]).T.astype(o_ref.dtype)


def _prompt_attention(qa, ka, vat, qi, wit, ki, qb, kb, vbt, qm, mem_kv, lam_params, g, batch, seq, layer, lam_init):
    depth = lam_params[0].shape[0]
    n_q = seq // Q_BLOCK
    n_classes = min(N_KV_CLASSES, n_q)
    q_per_class = n_q // n_classes
    n_mem = mem_kv.shape[0] // (depth * batch * N_KVM)
    topk = min(INDEX_TOPK_MAX, seq // 4)
    mem3 = mem_kv.reshape(depth * batch, n_mem * N_KVM, LANES)
    vec_spec = pl.BlockSpec((None, 1, DQK_B), lambda b, i: (layer, 0, 0))

    mixed = jnp.zeros((batch * seq, D_MIX), MXU_DTYPE)
    for c in range(n_classes):
        q0 = c * q_per_class
        n_keys = (c + 1) * q_per_class * Q_BLOCK

        def q_spec(w, q0=q0):
            return pl.BlockSpec((Q_BLOCK, w), lambda b, i: (b * n_q + q0 + i, 0))

        def k_spec(heads):
            return pl.BlockSpec((None, heads, n_keys, LANES), lambda b, i: (b, 0, 0, 0))

        def vt_spec(heads):
            return pl.BlockSpec((None, heads, LANES, n_keys), lambda b, i: (b, 0, 0, 0))

        in_specs = [q_spec(W_QA), k_spec(HKV_A), vt_spec(HKV_A), q_spec(W_QI),
                    pl.BlockSpec((H_IDX, Q_BLOCK), lambda b, i, q0=q0: (0, b * n_q + q0 + i)),
                    pl.BlockSpec((None, n_keys, D_IDX), lambda b, i: (b, 0, 0)),
                    q_spec(W_QB), k_spec(H_B), vt_spec(H_B), q_spec(W_QM),
                    pl.BlockSpec((None, n_mem * N_KVM, LANES), lambda b, i: (layer * batch + b, 0, 0)),
                    vec_spec, vec_spec, vec_spec, vec_spec,
                    pl.BlockSpec((None, 1, DV_B), lambda b, i: (layer, 0, 0)),
                    pl.BlockSpec(memory_space=pl.ANY)]
        args = [qa, ka, vat, qi, wit, ki, qb, kb, vbt, qm, mem3, *lam_params, g, mixed]
        mixed = pl.pallas_call(
            functools.partial(_prompt_attn_kernel, topk=topk, lam_init=lam_init, q0=q0),
            grid=(batch, q_per_class),
            in_specs=in_specs,
            out_specs=q_spec(D_MIX),
            out_shape=jax.ShapeDtypeStruct((batch * seq, D_MIX), MXU_DTYPE),
            scratch_shapes=[pltpu.VMEM((HKV_A, n_keys, 2 * LANES), MXU_DTYPE),
                            pltpu.VMEM((H_B, n_keys, 2 * LANES), MXU_DTYPE),
                            pltpu.VMEM((H_MEM, n_mem, DH_MEM), MXU_DTYPE),
                            pltpu.VMEM((H_MEM, DH_MEM, n_mem), MXU_DTYPE),
                            pltpu.VMEM((n_keys, Q_BLOCK), I32)],
            input_output_aliases={len(args) - 1: 0},
            compiler_params=_cparams(("parallel", "arbitrary")),
            name=f"prompt_attention_kv{c}",
        )(*args)
    return mixed


def _pad_rows(x, rows):
    return jnp.concatenate([x, jnp.zeros((rows - x.shape[0], x.shape[1]), x.dtype)], axis=0)


def _sample_index_kernel(pt_ref, qi_ref, wcol_ref, kin_ref, *rest, n_pages, group):
    del pt_ref
    idx_pages = rest[:group * n_pages]
    key_o = rest[group * n_pages]
    t_new = qi_ref.shape[0] // group
    past = n_pages * PAGE_SIZE

    lane = lax.broadcasted_iota(I32, (t_new, LANES), 1)
    tok = lax.broadcasted_iota(I32, (t_new, LANES), 0)
    new_ok = lane <= tok

    for b in range(group):
        rows = slice(b * t_new, (b + 1) * t_new)
        q_ht = jnp.concatenate([qi_ref[rows, h * D_IDX:(h + 1) * D_IDX] for h in range(H_IDX)],
                               axis=0).astype(MXU_DTYPE)
        wcol = wcol_ref[b * H_IDX * t_new:(b + 1) * H_IDX * t_new, :]

        def index_scores(qk):
            s = jnp.maximum(qk, 0.0) * wcol
            acc = s[0:t_new]
            for h in range(1, H_IDX):
                acc = acc + s[h * t_new:(h + 1) * t_new]
            return acc

        for j in range(n_pages):
            qk = _mm(q_ht, idx_pages[b * n_pages + j][...].astype(MXU_DTYPE))
            key_o[rows, j * LANES:(j + 1) * LANES] = _ordered_key(index_scores(qk))
        s_new = index_scores(_mm_nt(q_ht, _pad_rows(kin_ref[rows, :], LANES).astype(MXU_DTYPE)))
        key_o[rows, past:past + LANES] = _ordered_key(jnp.where(new_ok, s_new, NEG_INF))


def _sample_index(page_table, qi, wcol, ki_new, pool_idx, batch, t_new, layer, n_phys, group):
    n_pages = page_table.shape[1]
    width = (n_pages + 1) * LANES

    def row_spec(rows, w):
        return pl.BlockSpec((rows, w), lambda s, pt: (s, 0))

    def page_spec(b, j):
        return pl.BlockSpec((None, D_IDX, PAGE_SIZE),
                            lambda s, pt: (layer * n_phys + pt[s * group + b, j], 0, 0))

    in_specs = ([row_spec(group * t_new, W_QI), row_spec(group * H_IDX * t_new, 1), row_spec(group * t_new, D_IDX)]
                + [page_spec(b, j) for b in range(group) for j in range(n_pages)])
    grid_spec = pltpu.PrefetchScalarGridSpec(
        num_scalar_prefetch=1, grid=(batch // group,), in_specs=in_specs,
        out_specs=pl.BlockSpec((group * t_new, width), lambda s, pt: (s, 0)))
    return pl.pallas_call(
        functools.partial(_sample_index_kernel, n_pages=n_pages, group=group),
        grid_spec=grid_spec,
        out_shape=jax.ShapeDtypeStruct((batch * t_new, width), I32),
        compiler_params=_cparams(("parallel",)),
        name="sample_indexer",
    )(page_table, qi, wcol, ki_new, *([pool_idx] * (group * n_pages)))


def _sample_select_kernel(key_ref, bias_ref, *, topk):
    rows, width = key_ref.shape
    n_chunks = width // LANES
    t, n_ge = _kth_threshold(key_ref, rows, n_chunks, topk, groups=4 if rows % 64 == 0 else 1)
    _selection_bias(key_ref, bias_ref, t, n_ge, rows, n_chunks, topk)


def _sample_select(keys, topk, tr):
    m, width = keys.shape
    spec = pl.BlockSpec((tr, width), lambda i: (i, 0))
    return pl.pallas_call(
        functools.partial(_sample_select_kernel, topk=topk),
        grid=(m // tr,),
        in_specs=[spec],
        out_specs=spec,
        out_shape=jax.ShapeDtypeStruct((m, width), F32),
        compiler_params=_cparams(("parallel",)),
        name="sample_topk_select",
    )(keys)


def _sample_attn_kernel(pt_ref, qa_ref, kvan_ref, bias_ref, qb_ref, kvbn_ref, qm_ref, mem_ref,
                        lq1, lk1, lq2, lk2, g_ref, *rest, n_pages, lam_init):
    del pt_ref
    kva_pages = rest[:n_pages]
    kvb_pages = rest[n_pages:2 * n_pages]
    o_ref, sa_ref, sb_ref = rest[2 * n_pages:]
    t_new = qa_ref.shape[0]
    past = n_pages * PAGE_SIZE

    lane = lax.broadcasted_iota(I32, (t_new, LANES), 1)
    tok = lax.broadcasted_iota(I32, (t_new, LANES), 0)
    new_ok = lane <= tok

    def chunk_rows(ref, j, rows, n):
        return ref[pl.ds(j, rows, stride=n), :].astype(MXU_DTYPE)

    def new_rows(ref, j, n):
        return _pad_rows(ref[pl.ds(j, t_new, stride=n), :], LANES).astype(MXU_DTYPE)

    scale_a = DH_A ** -0.5
    rep = H_A // HKV_A
    bias_a = jnp.concatenate([bias_ref[...]] * rep, axis=0)
    for g in range(HKV_A):
        q = jnp.concatenate([qa_ref[:, (g * rep + r) * DH_A:(g * rep + r + 1) * DH_A] for r in range(rep)],
                            axis=0).astype(MXU_DTYPE)
        for j in range(n_pages):
            sa_ref[:, j * LANES:(j + 1) * LANES] = _mm_nt(q, chunk_rows(kva_pages[j], g, PAGE_SIZE, N_KVA))
        sa_ref[:, past:past + LANES] = _mm_nt(q, new_rows(kvan_ref, g, N_KVA))
        e, l = _exp_and_sum(sa_ref[...] * scale_a + bias_a)
        e = e.astype(MXU_DTYPE)
        acc = _mm(e[:, past:past + LANES], new_rows(kvan_ref, HKV_A + g, N_KVA))
        for j in range(n_pages):
            acc = acc + _mm(e[:, j * LANES:(j + 1) * LANES], chunk_rows(kva_pages[j], HKV_A + g, PAGE_SIZE, N_KVA))
        acc = acc / l
        for r in range(rep):
            h = g * rep + r
            o_ref[:, h * DH_A:(h + 1) * DH_A] = acc[r * t_new:(r + 1) * t_new]

    lam = _lambda_value(lq1, lk1, lq2, lk2, lam_init)
    scale_b = DQK_B ** -0.5
    for h in range(H_B):
        qh = qb_ref[:, h * DV_B:(h + 1) * DV_B]
        q = jnp.concatenate([_half_lanes(qh, 0), _half_lanes(qh, 1)], axis=0).astype(MXU_DTYPE)
        rows = slice(2 * h * t_new, (2 * h + 2) * t_new)
        for j in range(n_pages):
            sb_ref[rows, j * LANES:(j + 1) * LANES] = _mm_nt(q, chunk_rows(kvb_pages[j], h, PAGE_SIZE, N_KVB))
        sb_ref[rows, past:past + LANES] = _mm_nt(q, new_rows(kvbn_ref, h, N_KVB))
    new_bias = jnp.where(new_ok, 0.0, NEG_INF)
    bias_b = jnp.concatenate(
        [jnp.zeros((N_KVB * t_new, past), F32), jnp.concatenate([new_bias] * N_KVB, axis=0)], axis=1)
    e, l = _exp_and_sum(sb_ref[...] * scale_b + bias_b)
    e = e.astype(MXU_DTYPE)
    off_b = H_A * DH_A
    for h in range(H_B):
        rows = slice(2 * h * t_new, (2 * h + 2) * t_new)
        acc = _mm(e[rows, past:past + LANES], new_rows(kvbn_ref, H_B + h, N_KVB))
        for j in range(n_pages):
            acc = acc + _mm(e[rows, j * LANES:(j + 1) * LANES], chunk_rows(kvb_pages[j], H_B + h, PAGE_SIZE, N_KVB))
        acc = acc / l[rows]
        o = acc[0:t_new] - lam * acc[t_new:2 * t_new]
        o_ref[:, off_b + h * DV_B:off_b + (h + 1) * DV_B] = _diff_finish(o, g_ref, lam_init)

    scale_m = DH_MEM ** -0.5
    off_m = off_b + H_B * DV_B
    n_mem = mem_ref.shape[0] // N_KVM
    for h in range(H_MEM):
        q = qm_ref[:, h * DH_MEM:(h + 1) * DH_MEM].astype(MXU_DTYPE)
        e, l = _exp_and_sum(_mm_nt(q, chunk_rows(mem_ref, h, n_mem, N_KVM)) * scale_m)
        o_ref[:, off_m + h * DH_MEM:off_m + (h + 1) * DH_MEM] = (
            _mm(e.astype(MXU_DTYPE), chunk_rows(mem_ref, H_MEM + h, n_mem, N_KVM)) / l)


def _sample_attention(page_table, qa, kva_new, bias, qb, kvb_new, qm, pool_kva, pool_kvb, pool_mem,
                      lam_params, g, batch, t_new, layer, lam_init, n_phys, mem_rows):
    n_pages = page_table.shape[1]
    width = (n_pages + 1) * LANES

    def row_spec(rows, w):
        return pl.BlockSpec((rows, w), lambda b, pt: (b, 0))

    def page_spec(j, n):
        return pl.BlockSpec((PAGE_SIZE * n, LANES), lambda b, pt: (layer * n_phys + pt[b, j], 0))

    vec_spec = pl.BlockSpec((None, 1, DQK_B), lambda b, pt: (layer, 0, 0))
    in_specs = ([row_spec(t_new, W_QA), row_spec(t_new * N_KVA, LANES), row_spec(t_new, width),
                 row_spec(t_new, W_QB), row_spec(t_new * N_KVB, LANES), row_spec(t_new, W_QM),
                 pl.BlockSpec((mem_rows, LANES), lambda b, pt: (layer * batch + b, 0)),
                 vec_spec, vec_spec, vec_spec, vec_spec,
                 pl.BlockSpec((None, 1, DV_B), lambda b, pt: (layer, 0, 0))]
                + [page_spec(j, N_KVA) for j in range(n_pages)]
                + [page_spec(j, N_KVB) for j in range(n_pages)])
    grid_spec = pltpu.PrefetchScalarGridSpec(
        num_scalar_prefetch=1,
        grid=(batch,),
        in_specs=in_specs,
        out_specs=pl.BlockSpec((t_new, D_MIX), lambda b, pt: (b, 0)),
        scratch_shapes=[pltpu.VMEM((H_A // HKV_A * t_new, width), F32),
                        pltpu.VMEM((N_KVB * t_new, width), F32)],
    )
    return pl.pallas_call(
        functools.partial(_sample_attn_kernel, n_pages=n_pages, lam_init=lam_init),
        grid_spec=grid_spec,
        out_shape=jax.ShapeDtypeStruct((batch * t_new, D_MIX), F32),
        compiler_params=_cparams(("parallel",)),
        name="sample_attention",
    )(page_table, qa, kva_new, bias, qb, kvb_new, qm, pool_mem, *lam_params, g,
      *([pool_kva] * n_pages), *([pool_kvb] * n_pages))


def _attn_out_kernel(x_ref, mix_ref, wo_ref, g_ref, b_ref, h_ref, *, alpha):
    half = x_ref.shape[0] // 2
    for rows in (slice(0, half), slice(half, 2 * half)):
        a = _mm(mix_ref[rows, :].astype(MXU_DTYPE), wo_ref[...])
        h_ref[rows, :] = _layer_norm(alpha * x_ref[rows, :] + a, g_ref[...], b_ref[...])


def _attn_out(x2d, mixed, w_o, ln_g, ln_b, layer, alpha, tm):
    m, d = x2d.shape
    row = pl.BlockSpec((tm, d), lambda i: (i, 0))
    vec = pl.BlockSpec((None, 1, d), lambda i: (layer, 0, 0))
    return pl.pallas_call(
        functools.partial(_attn_out_kernel, alpha=alpha),
        grid=(m // tm,),
        in_specs=[row, pl.BlockSpec((tm, mixed.shape[1]), lambda i: (i, 0)),
                  pl.BlockSpec((None, mixed.shape[1], d), lambda i: (layer, 0, 0), pipeline_mode=pl.Buffered(1)),
                  vec, vec],
        out_specs=row,
        out_shape=jax.ShapeDtypeStruct((m, d), F32),
        compiler_params=_cparams(("parallel",)),
        name="attn_out_ln",
    )(x2d, mixed, w_o, ln_g, ln_b)


def _ffn_kernel(h_ref, wg_ref, wu_ref, wd_ref, g_ref, b_ref, o_ref, hb_ref, acc_ref, *, alpha):
    j = pl.program_id(1)

    @pl.when(j == 0)
    def _():
        hb_ref[...] = h_ref[...].astype(MXU_DTYPE)
        acc_ref[...] = jnp.zeros_like(acc_ref)

    hb = hb_ref[...]
    gate = _mm(hb, wg_ref[...])
    up = _mm(hb, wu_ref[...])
    act = gate * jax.nn.sigmoid(gate) * up
    acc_ref[...] += _mm(act.astype(MXU_DTYPE), wd_ref[...])

    @pl.when(j == pl.num_programs(1) - 1)
    def _():
        o_ref[...] = _layer_norm(alpha * h_ref[...] + acc_ref[...], g_ref[...], b_ref[...])


def _ffn(h2d, w_gate, w_up, w_down, ln_g, ln_b, layer, alpha, tm, tf):
    m, d = h2d.shape
    f = w_gate.shape[-1]
    row = pl.BlockSpec((tm, d), lambda i, j: (i, 0))
    vec = pl.BlockSpec((None, 1, d), lambda i, j: (layer, 0, 0))
    return pl.pallas_call(
        functools.partial(_ffn_kernel, alpha=alpha),
        grid=(m // tm, f // tf),
        in_specs=[row,
                  pl.BlockSpec((None, d, tf), lambda i, j: (layer, 0, j)),
                  pl.BlockSpec((None, d, tf), lambda i, j: (layer, 0, j)),
                  pl.BlockSpec((None, tf, d), lambda i, j: (layer, j, 0)),
                  vec, vec],
        out_specs=row,
        out_shape=jax.ShapeDtypeStruct((m, d), F32),
        scratch_shapes=[pltpu.VMEM((tm, d), MXU_DTYPE), pltpu.VMEM((tm, d), F32)],
        compiler_params=_cparams(("parallel", "arbitrary")),
        name="swiglu_ln",
    )(h2d, w_gate, w_up, w_down, ln_g, ln_b)


def _row_tile(m, cap):
    t = min(m, cap)
    while m % t:
        t //= 2
    return t


def _ff_tile(f, cap):
    best = LANES
    for t in range(LANES, cap + 1, LANES):
        if f % t == 0:
            best = t
    return best


def _largest_divisor(n, cap):
    return max(d for d in range(1, cap + 1) if n % d == 0)


def kernel(x_prompt, x_sample, mem_prompt, cache_kv_a, cache_idx_k, cache_kv_b, cache_mem_kv, page_table,
           w_in, w_mem_kv, lambda_q1, lambda_k1, lambda_q2, lambda_k2, subln_g, w_o,
           ln1_g, ln1_b, w_gate, w_up, w_down, ln2_g, ln2_b):
    depth, d_model, _ = w_in.shape
    batch, seq, _ = x_prompt.shape
    dec_batch, t_new, _ = x_sample.shape
    n_pages = page_table.shape[1]
    past = n_pages * PAGE_SIZE
    n_mem = mem_prompt.shape[1]
    n_phys = cache_idx_k.shape[1]
    alpha = (2.0 * depth) ** 0.25

    n_main = W_QA + W_KVA + W_QB + W_KVB
    w_t = jnp.swapaxes(w_in, 1, 2)
    w_in_p = jnp.concatenate(
        [w_t[:, :n_main], w_t[:, n_main + W_QM:n_main + W_QM + W_QI], w_t[:, n_main:n_main + W_QM],
         w_t[:, n_main + W_QM + W_QI:],
         jnp.zeros((depth, N_IN_PAD - w_in.shape[2], d_model), w_in.dtype)], axis=1).astype(MXU_DTYPE)
    w_mem_c = w_mem_kv.astype(MXU_DTYPE)
    w_o_c, w_gate_c, w_up_c, w_down_c = (w.astype(MXU_DTYPE) for w in (w_o, w_gate, w_up, w_down))
    vec3 = lambda a: a.astype(F32).reshape(depth, 1, a.shape[-1])
    lam_params = tuple(vec3(a) for a in (lambda_q1, lambda_k1, lambda_q2, lambda_k2))
    g_sub = vec3(subln_g)
    ln1g, ln1b, ln2g, ln2b = (vec3(a) for a in (ln1_g, ln1_b, ln2_g, ln2_b))

    pool_kva = cache_kv_a.reshape(-1, LANES)
    pool_idx = jnp.swapaxes(cache_idx_k, 2, 3).reshape(depth * n_phys, D_IDX, PAGE_SIZE)
    pool_kvb = cache_kv_b.reshape(-1, LANES)
    pool_mem = cache_mem_kv.reshape(-1, LANES)

    tm_p = _row_tile(seq, 512)
    m_s = dec_batch * t_new
    tm_s = _row_tile(m_s, 512)
    pos_p = jnp.arange(seq, dtype=I32)
    pos_s = past + (jnp.arange(tm_s, dtype=I32) % t_new)
    tabs_p = _rope_tables(pos_p, DH_A) + _rope_tables(pos_p, DQK_B)
    tabs_s = _rope_tables(pos_s, DH_A) + _rope_tables(pos_s, DQK_B)

    xp = x_prompt.reshape(batch * seq, d_model)
    xs = x_sample.reshape(m_s, d_model)
    mem2d = mem_prompt.reshape(batch * n_mem, d_model)
    tf = _ff_tile(w_gate.shape[-1], 512)
    topk_s = min(INDEX_TOPK_MAX, (past + t_new) // 4)
    idx_group = _largest_divisor(dec_batch, 4)

    outs = [[] for _ in range(3)]
    m_p = batch * seq
    caches = tuple(jnp.zeros((depth * m_p * r, w), F32) for r, w in ((N_KVA, LANES), (N_KVB, LANES), (1, D_IDX)))
    mem_kv = jnp.zeros((depth * batch * n_mem * N_KVM, LANES), F32)
    for l in range(depth):
        lam_init = 0.8 - 0.6 * math.exp(-0.3 * l)

        qa, kva, qb, kvb, qi, qm, ki, wi, ka_t, vat_t, kb_t, vbt_t, ki_t = _project(
            xp, w_in_p, l, tabs_p, seq // tm_p, tm_p, cache_slots=depth, prev_caches=caches, attn_batch=batch)
        caches = (kva, kvb, ki)
        mem_kv = _matmul(mem2d, w_mem_c, l, _row_tile(batch * n_mem, 512), prev=mem_kv)
        mixed = _prompt_attention(qa, ka_t, vat_t, qi, wi.T, ki_t, qb, kb_t, vbt_t, qm, mem_kv, lam_params, g_sub,
                                  batch, seq, l, lam_init)
        h = _attn_out(xp, mixed, w_o_c, ln1g, ln1b, l, alpha, tm_p)
        xp = _ffn(h, w_gate_c, w_up_c, w_down_c, ln2g, ln2b, l, alpha, tm_p, tf)

        qa, kva, qb, kvb, qi, qm, ki, wi = _project(xs, w_in_p, l, tabs_s, 1, tm_s)
        wcol = wi.reshape(dec_batch, t_new, H_IDX).transpose(0, 2, 1).reshape(dec_batch * H_IDX * t_new, 1)
        keys = _sample_index(page_table, qi, wcol, ki, pool_idx, dec_batch, t_new, l, n_phys, idx_group)
        bias = _sample_select(keys, topk_s, _row_tile(m_s, 256))
        mixed = _sample_attention(page_table, qa, kva, bias, qb, kvb, qm, pool_kva, pool_kvb, pool_mem,
                                  lam_params, g_sub, dec_batch, t_new, l, lam_init, n_phys, n_mem * N_KVM)
        h = _attn_out(xs, mixed, w_o_c, ln1g, ln1b, l, alpha, tm_s)
        xs = _ffn(h, w_gate_c, w_up_c, w_down_c, ln2g, ln2b, l, alpha, tm_s, tf)
        outs[0].append(kva.reshape(dec_batch, t_new, 2, HKV_A, DH_A))
        outs[1].append(ki.reshape(dec_batch, t_new, D_IDX))
        outs[2].append(kvb.reshape(dec_batch, t_new, 2, H_B, DV_B))

    kva_p, kvb_p, ki_p = caches
    return (xp.reshape(batch, seq, d_model), xs.reshape(dec_batch, t_new, d_model),
            kva_p.reshape(depth, batch, seq, 2, HKV_A, DH_A), ki_p.reshape(depth, batch, seq, D_IDX),
            kvb_p.reshape(depth, batch, seq, 2, H_B, DV_B), mem_kv.reshape(depth, batch, n_mem, 2, H_MEM, DH_MEM)
            ) + tuple(jnp.stack(o) for o in outs)
```

```python
import functools
import math

import jax
import jax.numpy as jnp
from jax import lax
from jax.experimental import pallas as pl
from jax.experimental.pallas import tpu as pltpu

F32 = jnp.float32
I32 = jnp.int32
MXU_DTYPE = jnp.bfloat16

H_A, DH_A, HKV_A = 8, 128, 2
H_IDX, D_IDX = 16, 64
INDEX_TOPK_MAX = 256
H_B, DQK_B, DV_B = 4, 64, 128
H_MEM, DH_MEM = 4, 128
PAGE_SIZE = 128
ROPE_THETA = 500000.0
ROPE_DIV = 4
LN_EPS = 1e-5
RMS_EPS = 1e-5
D_MIX = H_A * DH_A + H_B * DV_B + H_MEM * DH_MEM
N_KVA, N_KVB, N_KVM = 2 * HKV_A, 2 * H_B, 2 * H_MEM

LANES = 128
VMEM_LIMIT_BYTES = 56 * 1024 * 1024

W_QA, W_KVA, W_QB, W_KVB, W_QI, W_QM = 1024, 512, 512, 1024, 1024, 512
C_QA, C_KVA, C_QB, C_KVB, C_QI, C_QM, C_TAIL = 0, 8, 12, 16, 24, 32, 36
N_IN_PAD = 37 * LANES

LOG2_E = math.log2(math.e)
NEG_INF = float("-inf")
NEG_BIG = -1e30
INT_MIN = -(2 ** 31)
NEG_INF_KEY = -2139095041


def _cparams(sem):
    return pltpu.CompilerParams(dimension_semantics=sem, vmem_limit_bytes=VMEM_LIMIT_BYTES)


def _mm(a, b):
    return jnp.dot(a, b, preferred_element_type=F32)


def _mm_nt(a, b):
    return lax.dot_general(a, b, (((1,), (1,)), ((), ())), preferred_element_type=F32)


def _layer_norm(y, g, b):
    mu = jnp.mean(y, axis=-1, keepdims=True)
    d = y - mu
    var = jnp.mean(d * d, axis=-1, keepdims=True)
    return d * lax.rsqrt(var + LN_EPS) * g + b


def _exp_and_sum(s, axis=-1):
    m = jnp.max(s, axis=axis, keepdims=True)
    e = jnp.exp(s - m)
    return e, jnp.sum(e, axis=axis, keepdims=True)


def _ordered_key(x):
    k = pltpu.bitcast(x, I32)
    return jnp.where(k < 0, k ^ jnp.int32(0x7FFFFFFF), k)


def _half_lanes(x, c):
    lane = lax.broadcasted_iota(I32, x.shape, 1)
    return jnp.where((lane >= c * DQK_B) & (lane < (c + 1) * DQK_B), x, 0.0)


def _lambda_value(lq1, lk1, lq2, lk2, lam_init):
    a = jnp.sum(lq1[...] * lk1[...], axis=-1, keepdims=True)
    b = jnp.sum(lq2[...] * lk2[...], axis=-1, keepdims=True)
    return jnp.exp(a) - jnp.exp(b) + lam_init


def _diff_finish(o, g_ref, lam_init):
    o = o * lax.rsqrt(jnp.mean(o * o, axis=-1, keepdims=True) + RMS_EPS)
    return o * g_ref[...] * (1.0 - lam_init)


def _rope_tables(pos, head_dim):
    rot = head_dim // ROPE_DIV
    half = rot // 2
    t = pos.shape[0]
    inv_freq = jnp.float32(ROPE_THETA) ** (-jnp.arange(half, dtype=F32) / half)
    ang = pos.astype(F32)[:, None] * inv_freq[None, :]
    cos, sin = jnp.cos(ang), jnp.sin(ang)
    c = jnp.concatenate([cos, cos, jnp.ones((t, head_dim - rot), F32)], axis=-1)
    s_hi = jnp.concatenate([-sin, jnp.zeros((t, head_dim - half), F32)], axis=-1)
    s_lo = jnp.concatenate([jnp.zeros((t, half), F32), sin, jnp.zeros((t, head_dim - rot), F32)], axis=-1)
    rep = LANES // head_dim
    return tuple(jnp.tile(a, (1, rep)) for a in (c, s_hi, s_lo))


def _proj_kernel(x_ref, w_ref, ca, sa1, sa2, cb, sb1, sb2, *rest, n_prev, attn_layout):
    qa_o, kva_o, qb_o, kvb_o, qi_o, qm_o, ki_o, wi_o = rest[n_prev:n_prev + 8]
    ka_t, vat_t, kb_t, vbt_t, ki_t = rest[n_prev + 8:] if attn_layout else (None,) * 5
    xb = x_ref[...].astype(MXU_DTYPE)
    tm = x_ref.shape[0]

    def rope(z, c, s_hi, s_lo, half):
        return z * c[...] + pltpu.roll(z, LANES - half, 1) * s_hi[...] + pltpu.roll(z, half, 1) * s_lo[...]

    def rope_a(z):
        return rope(z, ca, sa1, sa2, DH_A // ROPE_DIV // 2)

    def rope_b(z):
        return rope(z, cb, sb1, sb2, DQK_B // ROPE_DIV // 2)

    def emit(out_ref, c0, kinds, interleave=False, key_t=None, value_t=None):
        n = len(kinds)
        z = _mm_nt(xb, w_ref[c0 * LANES:(c0 + n) * LANES, :])
        for j, kind in enumerate(kinds):
            zj = z[:, j * LANES:(j + 1) * LANES]
            if kind == "a":
                zj = rope_a(zj)
            elif kind == "b":
                zj = rope_b(zj)
            if interleave:
                out_ref[pl.ds(j, tm, stride=n), :] = zj
                if key_t is not None:
                    if j < n // 2:
                        key_t[j] = zj.astype(MXU_DTYPE)
                    else:
                        value_t[j - n // 2] = zj.T.astype(MXU_DTYPE)
            else:
                out_ref[:, j * LANES:(j + 1) * LANES] = zj

    emit(qa_o, C_QA, "a" * 8)
    emit(kva_o, C_KVA, "aa--", interleave=True, key_t=ka_t, value_t=vat_t)
    emit(qb_o, C_QB, "bbbb")
    emit(kvb_o, C_KVB, "bbbb----", interleave=True, key_t=kb_t, value_t=vbt_t)
    emit(qi_o, C_QI, "b" * 8)
    emit(qm_o, C_QM, "----")
    z = _mm_nt(xb, w_ref[C_TAIL * LANES:(C_TAIL + 1) * LANES, :])
    lane = lax.broadcasted_iota(I32, z.shape, 1)
    zt = jnp.where(lane < D_IDX, rope_b(z), z)
    ki_o[...] = zt[:, :D_IDX]
    wi_o[...] = zt[:, D_IDX:D_IDX + H_IDX]
    if ki_t is not None:
        ki_t[...] = zt[:, :D_IDX].astype(MXU_DTYPE)


CACHE_OUTPUTS = (1, 3, 6)


def _project(x2d, w_in_p, layer, tabs, n_tab_blocks, tm, cache_slots=1, prev_caches=None, attn_batch=None):
    m, d = x2d.shape
    n_blocks = m // tm
    shapes = ((1, W_QA), (N_KVA, LANES), (1, W_QB), (N_KVB, LANES), (1, W_QI), (1, W_QM), (1, D_IDX), (1, H_IDX))
    tab_spec = pl.BlockSpec((tm, LANES), lambda i: (i % n_tab_blocks, 0))
    slot = layer if cache_slots > 1 else 0
    out_specs, out_shape = [], []
    for k, (r, w) in enumerate(shapes):
        slots, base = (cache_slots, slot * n_blocks) if k in CACHE_OUTPUTS else (1, 0)
        out_specs.append(pl.BlockSpec((tm * r, w), lambda i, base=base: (base + i, 0)))
        out_shape.append(jax.ShapeDtypeStruct((slots * m * r, w), F32))
    if attn_batch is not None:
        seq = m // attn_batch
        per_seq = seq // tm
        for heads in (HKV_A, H_B):
            out_specs.append(pl.BlockSpec((None, heads, tm, LANES), lambda i: (i // per_seq, 0, i % per_seq, 0)))
            out_shape.append(jax.ShapeDtypeStruct((attn_batch, heads, seq, LANES), MXU_DTYPE))
            out_specs.append(pl.BlockSpec((None, heads, LANES, tm), lambda i: (i // per_seq, 0, 0, i % per_seq)))
            out_shape.append(jax.ShapeDtypeStruct((attn_batch, heads, LANES, seq), MXU_DTYPE))
        out_specs.append(pl.BlockSpec((None, tm, D_IDX), lambda i: (i // per_seq, i % per_seq, 0)))
        out_shape.append(jax.ShapeDtypeStruct((attn_batch, seq, D_IDX), MXU_DTYPE))
    in_specs = ([pl.BlockSpec((tm, d), lambda i: (i, 0)),
                 pl.BlockSpec((None, N_IN_PAD, d), lambda i: (layer, 0, 0), pipeline_mode=pl.Buffered(1))]
                + [tab_spec] * 6)
    args = [x2d, w_in_p, *tabs]
    aliases = {}
    n_prev = 0
    if prev_caches is not None:
        n_prev = len(CACHE_OUTPUTS)
        aliases = {len(args) + n: k for n, k in enumerate(CACHE_OUTPUTS)}
        in_specs += [pl.BlockSpec(memory_space=pl.ANY)] * n_prev
        args += list(prev_caches)
    return pl.pallas_call(
        functools.partial(_proj_kernel, n_prev=n_prev, attn_layout=attn_batch is not None),
        grid=(n_blocks,),
        in_specs=in_specs,
        out_specs=out_specs,
        out_shape=out_shape,
        input_output_aliases=aliases,
        compiler_params=_cparams(("parallel",)),
        name="proj_rope",
    )(*args)


def _matmul_kernel(x_ref, w_ref, *rest):
    o_ref = rest[-1]
    tm = x_ref.shape[0]
    n = w_ref.shape[1] // LANES
    z = _mm(x_ref[...].astype(MXU_DTYPE), w_ref[...])
    for j in range(n):
        o_ref[pl.ds(j, tm, stride=n), :] = z[:, j * LANES:(j + 1) * LANES]


def _matmul(x2d, w, layer, tm, prev=None):
    m, d = x2d.shape
    depth, _, n = w.shape
    n_blocks = m // tm
    in_specs = [pl.BlockSpec((tm, d), lambda i: (i, 0)),
                pl.BlockSpec((None, d, n), lambda i: (layer, 0, 0), pipeline_mode=pl.Buffered(1))]
    args = [x2d, w]
    aliases = {}
    if prev is not None:
        in_specs.append(pl.BlockSpec(memory_space=pl.ANY))
        aliases = {len(args): 0}
        args.append(prev)
    return pl.pallas_call(
        _matmul_kernel,
        grid=(n_blocks,),
        in_specs=in_specs,
        out_specs=pl.BlockSpec((tm * n // LANES, LANES), lambda i: (layer * n_blocks + i, 0)),
        out_shape=jax.ShapeDtypeStruct((depth * m * n // LANES, LANES), F32),
        input_output_aliases=aliases,
        compiler_params=_cparams(("parallel",)),
        name="mem_kv_proj",
    )(*args)


def _kth_threshold(key_ref, rows, n_chunks, topk, groups=1):
    ones = jnp.ones((LANES, LANES), MXU_DTYPE)
    gr = rows // groups

    def count_ge(cand, g):
        acc = jnp.zeros((gr, LANES), F32)
        for c in range(n_chunks):
            acc = acc + jnp.where(key_ref[g * gr:(g + 1) * gr, c * LANES:(c + 1) * LANES] >= cand, 1.0, 0.0)
        return _mm(acc.astype(MXU_DTYPE), ones)

    def body(it, ts):
        bit = lax.shift_left(jnp.int32(1), jnp.int32(31) - it)
        return tuple(jnp.where(count_ge(t + bit, g) >= topk, t + bit, t) for g, t in enumerate(ts))

    ts = lax.fori_loop(0, 32, body, tuple(jnp.full((gr, LANES), INT_MIN, I32) for _ in range(groups)))
    n_ge = [count_ge(t, g) for g, t in enumerate(ts)]
    return jnp.concatenate(ts, axis=0), jnp.concatenate(n_ge, axis=0)


def _selection_bias(key_ref, bias_ref, t, n_ge, rows, n_chunks, topk):
    floor_t = jnp.maximum(t, jnp.int32(NEG_INF_KEY + 1))
    tie_rows = jnp.where((n_ge > topk) & (t > NEG_INF_KEY), 1.0, 0.0)
    has_tie = jnp.max(tie_rows) > 0.5

    @pl.when(jnp.logical_not(has_tie))
    def _():
        for c in range(n_chunks):
            sl = slice(c * LANES, (c + 1) * LANES)
            bias_ref[:, sl] = jnp.where(key_ref[:, sl] >= floor_t, 0.0, NEG_INF)

    @pl.when(has_tie)
    def _():
        ones = jnp.ones((LANES, LANES), MXU_DTYPE)
        r_i = lax.broadcasted_iota(I32, (LANES, LANES), 0)
        c_i = lax.broadcasted_iota(I32, (LANES, LANES), 1)
        strict_upper = jnp.where(r_i < c_i, 1.0, 0.0).astype(MXU_DTYPE)
        n_gt = jnp.zeros((rows, LANES), F32)
        for c in range(n_chunks):
            sl = slice(c * LANES, (c + 1) * LANES)
            n_gt = n_gt + _mm(jnp.where(key_ref[:, sl] > t, 1.0, 0.0).astype(MXU_DTYPE), ones)
        need = topk - n_gt
        run = jnp.zeros((rows, LANES), F32)
        for c in range(n_chunks):
            sl = slice(c * LANES, (c + 1) * LANES)
            k = key_ref[:, sl]
            eq = jnp.where(k == t, 1.0, 0.0).astype(MXU_DTYPE)
            before = run + _mm(eq, strict_upper)
            run = run + _mm(eq, ones)
            keep = (k > t) | ((k == t) & (before < need))
            bias_ref[:, sl] = jnp.where(keep & (k > NEG_INF_KEY), 0.0, NEG_INF)


def _count_rows(pred_fn, key_ref, n_keys):
    acc = jnp.zeros((LANES, LANES), F32)
    for c in range(n_keys // LANES):
        acc = acc + jnp.where(pred_fn(key_ref[c * LANES:(c + 1) * LANES, :]), 1.0, 0.0)
    return jnp.sum(acc, axis=0, keepdims=True)


def _kth_threshold_t(key_ref, n_keys, topk):
    def body(it, t):
        cand = t + lax.shift_left(jnp.int32(1), jnp.int32(31) - it)
        return jnp.where(_count_rows(lambda k: k >= cand, key_ref, n_keys) >= topk, cand, t)

    t = lax.fori_loop(0, 32, body, jnp.full((1, LANES), INT_MIN, I32))
    return t, _count_rows(lambda k: k >= t, key_ref, n_keys)


def _selection_bias_t(key_ref, store_bias, t, n_ge, n_keys, topk):
    floor_t = jnp.maximum(t, jnp.int32(NEG_INF_KEY + 1))
    tie_lanes = jnp.where((n_ge > topk) & (t > NEG_INF_KEY), 1.0, 0.0)
    has_tie = jnp.max(tie_lanes) > 0.5

    @pl.when(jnp.logical_not(has_tie))
    def _():
        for c in range(n_keys // LANES):
            sl = slice(c * LANES, (c + 1) * LANES)
            store_bias(sl, jnp.where(key_ref[sl, :] >= floor_t, 0.0, NEG_BIG))

    @pl.when(has_tie)
    def _():
        r_i = lax.broadcasted_iota(I32, (LANES, LANES), 0)
        c_i = lax.broadcasted_iota(I32, (LANES, LANES), 1)
        strict_lower = jnp.where(c_i < r_i, 1.0, 0.0).astype(MXU_DTYPE)
        need = topk - _count_rows(lambda k: k > t, key_ref, n_keys)
        run = jnp.zeros((1, LANES), F32)
        for c in range(n_keys // LANES):
            sl = slice(c * LANES, (c + 1) * LANES)
            k = key_ref[sl, :]
            eq = jnp.where(k == t, 1.0, 0.0)
            before = run + _mm(strict_lower, eq.astype(MXU_DTYPE))
            run = run + jnp.sum(eq, axis=0, keepdims=True)
            keep = (k > t) | ((k == t) & (before < need))
            store_bias(sl, jnp.where(keep & (k > NEG_INF_KEY), 0.0, NEG_BIG))


Q_BLOCK = 128
N_KV_CLASSES = 8


def _prompt_attn_kernel(*refs, topk, lam_init, q0):
    (qa_ref, ka_ref, vat_ref, qi_ref, wit_ref, ki_ref, qb_ref, kb_ref, vbt_ref, qm_ref, mem_ref,
     lq1, lk1, lq2, lk2, g_ref, _, o_ref, kab_s, kbb_s, km_s, vmt_s, key_ref) = refs
    n_keys = ki_ref.shape[0]
    n_mem = km_s.shape[1]
    qi = pl.program_id(1)
    key_lanes = slice(0, LANES)
    mask_lanes = slice(LANES, 2 * LANES)

    @pl.when(qi == 0)
    def _():
        for g in range(HKV_A):
            kab_s[g, :, key_lanes] = ka_ref[g]
        for h in range(H_B):
            kbb_s[h, :, key_lanes] = kb_ref[h]
        for h in range(H_MEM):
            km_s[h] = mem_ref[pl.ds(h, n_mem, stride=N_KVM), :].astype(MXU_DTYPE)
            vmt_s[h] = mem_ref[pl.ds(H_MEM + h, n_mem, stride=N_KVM), :].T.astype(MXU_DTYPE)

    q_pos = (q0 + qi) * Q_BLOCK + lax.broadcasted_iota(I32, (n_keys, Q_BLOCK), 1)
    k_pos = lax.broadcasted_iota(I32, (n_keys, Q_BLOCK), 0)
    causal = k_pos <= q_pos

    causal_mask = jnp.where(causal, 0.0, NEG_BIG).astype(MXU_DTYPE)

    eye = jnp.where(lax.broadcasted_iota(I32, (Q_BLOCK, Q_BLOCK), 0) == lax.broadcasted_iota(I32, (Q_BLOCK, Q_BLOCK), 1),
                    1.0, 0.0)

    def with_mask_selector(q_blocks, log2_scale):
        q = jnp.concatenate([qb_ * log2_scale for qb_ in q_blocks], axis=0)
        return jnp.concatenate([q, jnp.concatenate([eye] * len(q_blocks), axis=0)], axis=1).astype(MXU_DTYPE)

    def softmax_cols(s, n_cols):
        es, ls = [], []
        for r in range(n_cols):
            x = s[:, r * Q_BLOCK:(r + 1) * Q_BLOCK]
            e = jnp.exp2(x - jnp.max(x, axis=0, keepdims=True))
            es.append(e.astype(MXU_DTYPE))
            ls.append(jnp.sum(e, axis=0, keepdims=True))
        return jnp.concatenate(es, axis=1), ls


    lam = _lambda_value(lq1, lk1, lq2, lk2, lam_init)
    off_b = H_A * DH_A
    for h in range(H_B):
        kbb_s[h, :, mask_lanes] = causal_mask
        qh = qb_ref[:, h * DV_B:(h + 1) * DV_B]
        q2 = with_mask_selector([_half_lanes(qh, 0), _half_lanes(qh, 1)], DQK_B ** -0.5 * LOG2_E)
        e, ls = softmax_cols(_mm_nt(kbb_s[h], q2), 2)
        ot = _mm(vbt_ref[h], e)
        o = (ot[:, :Q_BLOCK] / ls[0] - lam * (ot[:, Q_BLOCK:] / ls[1])).T
        o_ref[:, off_b + h * DV_B:off_b + (h + 1) * DV_B] = _diff_finish(o, g_ref, lam_init).astype(o_ref.dtype)

    off_m = off_b + H_B * DV_B
    for h in range(H_MEM):
        q = (qm_ref[:, h * DH_MEM:(h + 1) * DH_MEM] * (DH_MEM ** -0.5 * LOG2_E)).astype(MXU_DTYPE)
        e, ls = softmax_cols(_mm_nt(km_s[h], q), 1)
        o_ref[:, off_m + h * DH_MEM:off_m + (h + 1) * DH_MEM] = (_mm(vmt_s[h], e) / ls[0]).T.astype(o_ref.dtype)

    def store_selection(rows, block):
        for g in range(HKV_A):
            kab_s[g, rows, mask_lanes] = block.astype(MXU_DTYPE)

    if n_keys <= topk:
        store_selection(slice(0, n_keys), causal_mask)
    else:
        ki = ki_ref[...]
        sc = jnp.zeros((n_keys, Q_BLOCK), F32)
        for hp in range(H_IDX // 2):
            q2 = jnp.concatenate([qi_ref[:, (2 * hp + j) * D_IDX:(2 * hp + j + 1) * D_IDX] for j in range(2)],
                                 axis=0).astype(MXU_DTYPE)
            s = _mm_nt(ki, q2)
            pair = (wit_ref[2 * hp:2 * hp + 1, :] * jnp.maximum(s[:, :Q_BLOCK], 0.0)
                    + wit_ref[2 * hp + 1:2 * hp + 2, :] * jnp.maximum(s[:, Q_BLOCK:], 0.0))
            sc = sc + pair
        key_ref[...] = _ordered_key(jnp.where(causal, sc, NEG_INF))
        t, n_ge = _kth_threshold_t(key_ref, n_keys, topk)
        _selection_bias_t(key_ref, store_selection, t, n_ge, n_keys, topk)

    rep = H_A // HKV_A
    for g in range(HKV_A):
        q4 = with_mask_selector([qa_ref[:, (g * rep + r) * DH_A:(g * rep + r + 1) * DH_A] for r in range(rep)],
                                DH_A ** -0.5 * LOG2_E)
        e, ls = softmax_cols(_mm_nt(kab_s[g], q4), rep)
        ot = _mm(vat_ref[g], e)
        for r in range(rep):
            h = g * rep + r
            o_ref[:, h * DH_A:(h + 1) * DH_A] = (ot[:, r * Q_BLOCK:(r + 1) * Q_BLOCK] / ls[r]).T.astype(o_ref.dtype)


def _prompt_attention(qa, ka, vat, qi, wit, ki, qb, kb, vbt, qm, mem_kv, lam_params, g, batch, seq, layer, lam_init):
    depth = lam_params[0].shape[0]
    n_q = seq // Q_BLOCK
    n_classes = min(N_KV_CLASSES, n_q)
    q_per_class = n_q // n_classes
    n_mem = mem_kv.shape[0] // (depth * batch * N_KVM)
    topk = min(INDEX_TOPK_MAX, seq // 4)
    mem3 = mem_kv.reshape(depth * batch, n_mem * N_KVM, LANES)
    vec_spec = pl.BlockSpec((None, 1, DQK_B), lambda b, i: (layer, 0, 0))

    mixed = jnp.zeros((batch * seq, D_MIX), MXU_DTYPE)
    for c in range(n_classes):
        q0 = c * q_per_class
        n_keys = (c + 1) * q_per_class * Q_BLOCK

        def q_spec(w, q0=q0):
            return pl.BlockSpec((Q_BLOCK, w), lambda b, i: (b * n_q + q0 + i, 0))

        def k_spec(heads):
            return pl.BlockSpec((None, heads, n_keys, LANES), lambda b, i: (b, 0, 0, 0))

        def vt_spec(heads):
            return pl.BlockSpec((None, heads, LANES, n_keys), lambda b, i: (b, 0, 0, 0))

        in_specs = [q_spec(W_QA), k_spec(HKV_A), vt_spec(HKV_A), q_spec(W_QI),
                    pl.BlockSpec((H_IDX, Q_BLOCK), lambda b, i, q0=q0: (0, b * n_q + q0 + i)),
                    pl.BlockSpec((None, n_keys, D_IDX), lambda b, i: (b, 0, 0)),
                    q_spec(W_QB), k_spec(H_B), vt_spec(H_B), q_spec(W_QM),
                    pl.BlockSpec((None, n_mem * N_KVM, LANES), lambda b, i: (layer * batch + b, 0, 0)),
                    vec_spec, vec_spec, vec_spec, vec_spec,
                    pl.BlockSpec((None, 1, DV_B), lambda b, i: (layer, 0, 0)),
                    pl.BlockSpec(memory_space=pl.ANY)]
        args = [qa, ka, vat, qi, wit, ki, qb, kb, vbt, qm, mem3, *lam_params, g, mixed]
        mixed = pl.pallas_call(
            functools.partial(_prompt_attn_kernel, topk=topk, lam_init=lam_init, q0=q0),
            grid=(batch, q_per_class),
            in_specs=in_specs,
            out_specs=q_spec(D_MIX),
            out_shape=jax.ShapeDtypeStruct((batch * seq, D_MIX), MXU_DTYPE),
            scratch_shapes=[pltpu.VMEM((HKV_A, n_keys, 2 * LANES), MXU_DTYPE),
                            pltpu.VMEM((H_B, n_keys, 2 * LANES), MXU_DTYPE),
                            pltpu.VMEM((H_MEM, n_mem, DH_MEM), MXU_DTYPE),
                            pltpu.VMEM((H_MEM, DH_MEM, n_mem), MXU_DTYPE),
                            pltpu.VMEM((n_keys, Q_BLOCK), I32)],
            input_output_aliases={len(args) - 1: 0},
            compiler_params=_cparams(("parallel", "arbitrary")),
            name=f"prompt_attention_kv{c}",
        )(*args)
    return mixed


def _pad_rows(x, rows):
    return jnp.concatenate([x, jnp.zeros((rows - x.shape[0], x.shape[1]), x.dtype)], axis=0)


def _sample_index_kernel(pt_ref, qi_ref, wcol_ref, kin_ref, *rest, n_pages, group):
    del pt_ref
    idx_pages = rest[:group * n_pages]
    key_o = rest[group * n_pages]
    t_new = qi_ref.shape[0] // group
    past = n_pages * PAGE_SIZE

    lane = lax.broadcasted_iota(I32, (t_new, LANES), 1)
    tok = lax.broadcasted_iota(I32, (t_new, LANES), 0)
    new_ok = lane <= tok

    for b in range(group):
        rows = slice(b * t_new, (b + 1) * t_new)
        q_ht = jnp.concatenate([qi_ref[rows, h * D_IDX:(h + 1) * D_IDX] for h in range(H_IDX)],
                               axis=0).astype(MXU_DTYPE)
        wcol = wcol_ref[b * H_IDX * t_new:(b + 1) * H_IDX * t_new, :]

        def index_scores(qk):
            s = jnp.maximum(qk, 0.0) * wcol
            acc = s[0:t_new]
            for h in range(1, H_IDX):
                acc = acc + s[h * t_new:(h + 1) * t_new]
            return acc

        for j in range(n_pages):
            qk = _mm(q_ht, idx_pages[b * n_pages + j][...].astype(MXU_DTYPE))
            key_o[rows, j * LANES:(j + 1) * LANES] = _ordered_key(index_scores(qk))
        s_new = index_scores(_mm_nt(q_ht, _pad_rows(kin_ref[rows, :], LANES).astype(MXU_DTYPE)))
        key_o[rows, past:past + LANES] = _ordered_key(jnp.where(new_ok, s_new, NEG_INF))


def _sample_index(page_table, qi, wcol, ki_new, pool_idx, batch, t_new, layer, n_phys, group):
    n_pages = page_table.shape[1]
    width = (n_pages + 1) * LANES

    def row_spec(rows, w):
        return pl.BlockSpec((rows, w), lambda s, pt: (s, 0))

    def page_spec(b, j):
        return pl.BlockSpec((None, D_IDX, PAGE_SIZE),
                            lambda s, pt: (layer * n_phys + pt[s * group + b, j], 0, 0))

    in_specs = ([row_spec(group * t_new, W_QI), row_spec(group * H_IDX * t_new, 1), row_spec(group * t_new, D_IDX)]
                + [page_spec(b, j) for b in range(group) for j in range(n_pages)])
    grid_spec = pltpu.PrefetchScalarGridSpec(
        num_scalar_prefetch=1, grid=(batch // group,), in_specs=in_specs,
        out_specs=pl.BlockSpec((group * t_new, width), lambda s, pt: (s, 0)))
    return pl.pallas_call(
        functools.partial(_sample_index_kernel, n_pages=n_pages, group=group),
        grid_spec=grid_spec,
        out_shape=jax.ShapeDtypeStruct((batch * t_new, width), I32),
        compiler_params=_cparams(("parallel",)),
        name="sample_indexer",
    )(page_table, qi, wcol, ki_new, *([pool_idx] * (group * n_pages)))


def _sample_select_kernel(key_ref, bias_ref, *, topk):
    rows, width = key_ref.shape
    n_chunks = width // LANES
    t, n_ge = _kth_threshold(key_ref, rows, n_chunks, topk, groups=4 if rows % 64 == 0 else 1)
    _selection_bias(key_ref, bias_ref, t, n_ge, rows, n_chunks, topk)


def _sample_select(keys, topk, tr):
    m, width = keys.shape
    spec = pl.BlockSpec((tr, width), lambda i: (i, 0))
    return pl.pallas_call(
        functools.partial(_sample_select_kernel, topk=topk),
        grid=(m // tr,),
        in_specs=[spec],
        out_specs=spec,
        out_shape=jax.ShapeDtypeStruct((m, width), F32),
        compiler_params=_cparams(("parallel",)),
        name="sample_topk_select",
    )(keys)


def _sample_attn_kernel(pt_ref, qa_ref, kvan_ref, bias_ref, qb_ref, kvbn_ref, qm_ref, mem_ref,
                        lq1, lk1, lq2, lk2, g_ref, *rest, n_pages, lam_init):
    del pt_ref
    kva_pages = rest[:n_pages]
    kvb_pages = rest[n_pages:2 * n_pages]
    o_ref, sa_ref, sb_ref = rest[2 * n_pages:]
    t_new = qa_ref.shape[0]
    past = n_pages * PAGE_SIZE

    lane = lax.broadcasted_iota(I32, (t_new, LANES), 1)
    tok = lax.broadcasted_iota(I32, (t_new, LANES), 0)
    new_ok = lane <= tok

    def chunk_rows(ref, j, rows, n):
        return ref[pl.ds(j, rows, stride=n), :].astype(MXU_DTYPE)

    def new_rows(ref, j, n):
        return _pad_rows(ref[pl.ds(j, t_new, stride=n), :], LANES).astype(MXU_DTYPE)

    scale_a = DH_A ** -0.5
    rep = H_A // HKV_A
    bias_a = jnp.concatenate([bias_ref[...]] * rep, axis=0)
    for g in range(HKV_A):
        q = jnp.concatenate([qa_ref[:, (g * rep + r) * DH_A:(g * rep + r + 1) * DH_A] for r in range(rep)],
                            axis=0).astype(MXU_DTYPE)
        for j in range(n_pages):
            sa_ref[:, j * LANES:(j + 1) * LANES] = _mm_nt(q, chunk_rows(kva_pages[j], g, PAGE_SIZE, N_KVA))
        sa_ref[:, past:past + LANES] = _mm_nt(q, new_rows(kvan_ref, g, N_KVA))
        e, l = _exp_and_sum(sa_ref[...] * scale_a + bias_a)
        e = e.astype(MXU_DTYPE)
        acc = _mm(e[:, past:past + LANES], new_rows(kvan_ref, HKV_A + g, N_KVA))
        for j in range(n_pages):
            acc = acc + _mm(e[:, j * LANES:(j + 1) * LANES], chunk_rows(kva_pages[j], HKV_A + g, PAGE_SIZE, N_KVA))
        acc = acc / l
        for r in range(rep):
            h = g * rep + r
            o_ref[:, h * DH_A:(h + 1) * DH_A] = acc[r * t_new:(r + 1) * t_new]

    lam = _lambda_value(lq1, lk1, lq2, lk2, lam_init)
    scale_b = DQK_B ** -0.5
    for h in range(H_B):
        qh = qb_ref[:, h * DV_B:(h + 1) * DV_B]
        q = jnp.concatenate([_half_lanes(qh, 0), _half_lanes(qh, 1)], axis=0).astype(MXU_DTYPE)
        rows = slice(2 * h * t_new, (2 * h + 2) * t_new)
        for j in range(n_pages):
            sb_ref[rows, j * LANES:(j + 1) * LANES] = _mm_nt(q, chunk_rows(kvb_pages[j], h, PAGE_SIZE, N_KVB))
        sb_ref[rows, past:past + LANES] = _mm_nt(q, new_rows(kvbn_ref, h, N_KVB))
    new_bias = jnp.where(new_ok, 0.0, NEG_INF)
    bias_b = jnp.concatenate(
        [jnp.zeros((N_KVB * t_new, past), F32), jnp.concatenate([new_bias] * N_KVB, axis=0)], axis=1)
    e, l = _exp_and_sum(sb_ref[...] * scale_b + bias_b)
    e = e.astype(MXU_DTYPE)
    off_b = H_A * DH_A
    for h in range(H_B):
        rows = slice(2 * h * t_new, (2 * h + 2) * t_new)
        acc = _mm(e[rows, past:past + LANES], new_rows(kvbn_ref, H_B + h, N_KVB))
        for j in range(n_pages):
            acc = acc + _mm(e[rows, j * LANES:(j + 1) * LANES], chunk_rows(kvb_pages[j], H_B + h, PAGE_SIZE, N_KVB))
        acc = acc / l[rows]
        o = acc[0:t_new] - lam * acc[t_new:2 * t_new]
        o_ref[:, off_b + h * DV_B:off_b + (h + 1) * DV_B] = _diff_finish(o, g_ref, lam_init)

    scale_m = DH_MEM ** -0.5
    off_m = off_b + H_B * DV_B
    n_mem = mem_ref.shape[0] // N_KVM
    for h in range(H_MEM):
        q = qm_ref[:, h * DH_MEM:(h + 1) * DH_MEM].astype(MXU_DTYPE)
        e, l = _exp_and_sum(_mm_nt(q, chunk_rows(mem_ref, h, n_mem, N_KVM)) * scale_m)
        o_ref[:, off_m + h * DH_MEM:off_m + (h + 1) * DH_MEM] = (
            _mm(e.astype(MXU_DTYPE), chunk_rows(mem_ref, H_MEM + h, n_mem, N_KVM)) / l)


def _sample_attention(page_table, qa, kva_new, bias, qb, kvb_new, qm, pool_kva, pool_kvb, pool_mem,
                      lam_params, g, batch, t_new, layer, lam_init, n_phys, mem_rows):
    n_pages = page_table.shape[1]
    width = (n_pages + 1) * LANES

    def row_spec(rows, w):
        return pl.BlockSpec((rows, w), lambda b, pt: (b, 0))

    def page_spec(j, n):
        return pl.BlockSpec((PAGE_SIZE * n, LANES), lambda b, pt: (layer * n_phys + pt[b, j], 0))

    vec_spec = pl.BlockSpec((None, 1, DQK_B), lambda b, pt: (layer, 0, 0))
    in_specs = ([row_spec(t_new, W_QA), row_spec(t_new * N_KVA, LANES), row_spec(t_new, width),
                 row_spec(t_new, W_QB), row_spec(t_new * N_KVB, LANES), row_spec(t_new, W_QM),
                 pl.BlockSpec((mem_rows, LANES), lambda b, pt: (layer * batch + b, 0)),
                 vec_spec, vec_spec, vec_spec, vec_spec,
                 pl.BlockSpec((None, 1, DV_B), lambda b, pt: (layer, 0, 0))]
                + [page_spec(j, N_KVA) for j in range(n_pages)]
                + [page_spec(j, N_KVB) for j in range(n_pages)])
    grid_spec = pltpu.PrefetchScalarGridSpec(
        num_scalar_prefetch=1,
        grid=(batch,),
        in_specs=in_specs,
        out_specs=pl.BlockSpec((t_new, D_MIX), lambda b, pt: (b, 0)),
        scratch_shapes=[pltpu.VMEM((H_A // HKV_A * t_new, width), F32),
                        pltpu.VMEM((N_KVB * t_new, width), F32)],
    )
    return pl.pallas_call(
        functools.partial(_sample_attn_kernel, n_pages=n_pages, lam_init=lam_init),
        grid_spec=grid_spec,
        out_shape=jax.ShapeDtypeStruct((batch * t_new, D_MIX), F32),
        compiler_params=_cparams(("parallel",)),
        name="sample_attention",
    )(page_table, qa, kva_new, bias, qb, kvb_new, qm, pool_mem, *lam_params, g,
      *([pool_kva] * n_pages), *([pool_kvb] * n_pages))


def _attn_out_kernel(x_ref, mix_ref, wo_ref, g_ref, b_ref, h_ref, *, alpha):
    tm = x_ref.shape[0]
    for rows in [slice(k * tm // FFN_ROW_SPLITS, (k + 1) * tm // FFN_ROW_SPLITS) for k in range(FFN_ROW_SPLITS)]:
        a = _mm(mix_ref[rows, :].astype(MXU_DTYPE), wo_ref[...])
        h_ref[rows, :] = _layer_norm(alpha * x_ref[rows, :] + a, g_ref[...], b_ref[...])


def _attn_out(x2d, mixed, w_o, ln_g, ln_b, layer, alpha, tm):
    m, d = x2d.shape
    row = pl.BlockSpec((tm, d), lambda i: (i, 0))
    vec = pl.BlockSpec((None, 1, d), lambda i: (layer, 0, 0))
    return pl.pallas_call(
        functools.partial(_attn_out_kernel, alpha=alpha),
        grid=(m // tm,),
        in_specs=[row, pl.BlockSpec((tm, mixed.shape[1]), lambda i: (i, 0)),
                  pl.BlockSpec((None, mixed.shape[1], d), lambda i: (layer, 0, 0), pipeline_mode=pl.Buffered(1)),
                  vec, vec],
        out_specs=row,
        out_shape=jax.ShapeDtypeStruct((m, d), F32),
        compiler_params=_cparams(("parallel",)),
        name="attn_out_ln",
    )(x2d, mixed, w_o, ln_g, ln_b)


FFN_ROW_SPLITS = 4


def _ffn_kernel(h_ref, wg_ref, wu_ref, wd_ref, g_ref, b_ref, o_ref, hb_ref, acc_ref, *, alpha, n_steps):
    j = pl.program_id(1)
    tm = h_ref.shape[0]
    groups = [slice(k * tm // FFN_ROW_SPLITS, (k + 1) * tm // FFN_ROW_SPLITS) for k in range(FFN_ROW_SPLITS)]

    def down(rows):
        hb = hb_ref[rows, :]
        gate = _mm(hb, wg_ref[...])
        up = _mm(hb, wu_ref[...])
        return _mm((gate * jax.nn.sigmoid(gate) * up).astype(MXU_DTYPE), wd_ref[...])

    def finish(rows, y):
        o_ref[rows, :] = _layer_norm(alpha * h_ref[rows, :] + y, g_ref[...], b_ref[...])

    if n_steps == 1:
        for rows in groups:
            hb_ref[rows, :] = h_ref[rows, :].astype(MXU_DTYPE)
            finish(rows, down(rows))
        return

    @pl.when(j == 0)
    def _():
        for rows in groups:
            hb_ref[rows, :] = h_ref[rows, :].astype(MXU_DTYPE)
            acc_ref[rows, :] = down(rows)

    @pl.when((j > 0) & (j < n_steps - 1))
    def _():
        acc_ref[...] += down(slice(None))

    @pl.when(j == n_steps - 1)
    def _():
        for rows in groups:
            finish(rows, acc_ref[rows, :] + down(rows))


def _ffn(h2d, w_gate, w_up, w_down, ln_g, ln_b, layer, alpha, tm, tf):
    m, d = h2d.shape
    f = w_gate.shape[-1]
    row = pl.BlockSpec((tm, d), lambda i, j: (i, 0))
    vec = pl.BlockSpec((None, 1, d), lambda i, j: (layer, 0, 0))
    return pl.pallas_call(
        functools.partial(_ffn_kernel, alpha=alpha, n_steps=f // tf),
        grid=(m // tm, f // tf),
        in_specs=[row,
                  pl.BlockSpec((None, d, tf), lambda i, j: (layer, 0, j)),
                  pl.BlockSpec((None, d, tf), lambda i, j: (layer, 0, j)),
                  pl.BlockSpec((None, tf, d), lambda i, j: (layer, j, 0)),
                  vec, vec],
        out_specs=row,
        out_shape=jax.ShapeDtypeStruct((m, d), F32),
        scratch_shapes=[pltpu.VMEM((tm, d), MXU_DTYPE), pltpu.VMEM((tm, d), F32)],
        compiler_params=_cparams(("parallel", "arbitrary")),
        name="swiglu_ln",
    )(h2d, w_gate, w_up, w_down, ln_g, ln_b)


def _row_tile(m, cap):
    t = min(m, cap)
    while m % t:
        t //= 2
    return t


def _ff_tile(f, cap):
    best = LANES
    for t in range(LANES, cap + 1, LANES):
        if f % t == 0:
            best = t
    return best


def _largest_divisor(n, cap):
    return max(d for d in range(1, cap + 1) if n % d == 0)


def kernel(x_prompt, x_sample, mem_prompt, cache_kv_a, cache_idx_k, cache_kv_b, cache_mem_kv, page_table,
           w_in, w_mem_kv, lambda_q1, lambda_k1, lambda_q2, lambda_k2, subln_g, w_o,
           ln1_g, ln1_b, w_gate, w_up, w_down, ln2_g, ln2_b):
    depth, d_model, _ = w_in.shape
    batch, seq, _ = x_prompt.shape
    dec_batch, t_new, _ = x_sample.shape
    n_pages = page_table.shape[1]
    past = n_pages * PAGE_SIZE
    n_mem = mem_prompt.shape[1]
    n_phys = cache_idx_k.shape[1]
    alpha = (2.0 * depth) ** 0.25

    n_main = W_QA + W_KVA + W_QB + W_KVB
    w_t = jnp.swapaxes(w_in, 1, 2)
    w_in_p = jnp.concatenate(
        [w_t[:, :n_main], w_t[:, n_main + W_QM:n_main + W_QM + W_QI], w_t[:, n_main:n_main + W_QM],
         w_t[:, n_main + W_QM + W_QI:],
         jnp.zeros((depth, N_IN_PAD - w_in.shape[2], d_model), w_in.dtype)], axis=1).astype(MXU_DTYPE)
    w_mem_c = w_mem_kv.astype(MXU_DTYPE)
    w_o_c, w_gate_c, w_up_c, w_down_c = (w.astype(MXU_DTYPE) for w in (w_o, w_gate, w_up, w_down))
    vec3 = lambda a: a.astype(F32).reshape(depth, 1, a.shape[-1])
    lam_params = tuple(vec3(a) for a in (lambda_q1, lambda_k1, lambda_q2, lambda_k2))
    g_sub = vec3(subln_g)
    ln1g, ln1b, ln2g, ln2b = (vec3(a) for a in (ln1_g, ln1_b, ln2_g, ln2_b))

    pool_kva = cache_kv_a.reshape(-1, LANES)
    pool_idx = jnp.swapaxes(cache_idx_k, 2, 3).reshape(depth * n_phys, D_IDX, PAGE_SIZE)
    pool_kvb = cache_kv_b.reshape(-1, LANES)
    pool_mem = cache_mem_kv.reshape(-1, LANES)

    tm_p = _row_tile(seq, 512)
    m_s = dec_batch * t_new
    tm_s = _row_tile(m_s, 512)
    pos_p = jnp.arange(seq, dtype=I32)
    pos_s = past + (jnp.arange(tm_s, dtype=I32) % t_new)
    tabs_p = _rope_tables(pos_p, DH_A) + _rope_tables(pos_p, DQK_B)
    tabs_s = _rope_tables(pos_s, DH_A) + _rope_tables(pos_s, DQK_B)

    xp = x_prompt.reshape(batch * seq, d_model)
    xs = x_sample.reshape(m_s, d_model)
    mem2d = mem_prompt.reshape(batch * n_mem, d_model)
    tf = _ff_tile(w_gate.shape[-1], 512)
    topk_s = min(INDEX_TOPK_MAX, (past + t_new) // 4)
    idx_group = _largest_divisor(dec_batch, 4)

    outs = [[] for _ in range(3)]
    m_p = batch * seq
    caches = tuple(jnp.zeros((depth * m_p * r, w), F32) for r, w in ((N_KVA, LANES), (N_KVB, LANES), (1, D_IDX)))
    mem_kv = jnp.zeros((depth * batch * n_mem * N_KVM, LANES), F32)
    for l in range(depth):
        lam_init = 0.8 - 0.6 * math.exp(-0.3 * l)

        qa, kva, qb, kvb, qi, qm, ki, wi, ka_t, vat_t, kb_t, vbt_t, ki_t = _project(
            xp, w_in_p, l, tabs_p, seq // tm_p, tm_p, cache_slots=depth, prev_caches=caches, attn_batch=batch)
        caches = (kva, kvb, ki)
        mem_kv = _matmul(mem2d, w_mem_c, l, _row_tile(batch * n_mem, 512), prev=mem_kv)
        mixed = _prompt_attention(qa, ka_t, vat_t, qi, wi.T, ki_t, qb, kb_t, vbt_t, qm, mem_kv, lam_params, g_sub,
                                  batch, seq, l, lam_init)
        h = _attn_out(xp, mixed, w_o_c, ln1g, ln1b, l, alpha, tm_p)
        xp = _ffn(h, w_gate_c, w_up_c, w_down_c, ln2g, ln2b, l, alpha, tm_p, tf)

        qa, kva, qb, kvb, qi, qm, ki, wi = _project(xs, w_in_p, l, tabs_s, 1, tm_s)
        wcol = wi.reshape(dec_batch, t_new, H_IDX).transpose(0, 2, 1).reshape(dec_batch * H_IDX * t_new, 1)
        keys = _sample_index(page_table, qi, wcol, ki, pool_idx, dec_batch, t_new, l, n_phys, idx_group)
        bias = _sample_select(keys, topk_s, _row_tile(m_s, 256))
        mixed = _sample_attention(page_table, qa, kva, bias, qb, kvb, qm, pool_kva, pool_kvb, pool_mem,
                                  lam_params, g_sub, dec_batch, t_new, l, lam_init, n_phys, n_mem * N_KVM)
        h = _attn_out(xs, mixed, w_o_c, ln1g, ln1b, l, alpha, tm_s)
        xs = _ffn(h, w_gate_c, w_up_c, w_down_c, ln2g, ln2b, l, alpha, tm_s, tf)
        outs[0].append(kva.reshape(dec_batch, t_new, 2, HKV_A, DH_A))
        outs[1].append(ki.reshape(dec_batch, t_new, D_IDX))
        outs[2].append(kvb.reshape(dec_batch, t_new, 2, H_B, DV_B))

    kva_p, kvb_p, ki_p = caches
    return (xp.reshape(batch, seq, d_model), xs.reshape(dec_batch, t_new, d_model),
            kva_p.reshape(depth, batch, seq, 2, HKV_A, DH_A), ki_p.reshape(depth, batch, seq, D_IDX),
            kvb_p.reshape(depth, batch, seq, 2, H_B, DV_B), mem_kv.reshape(depth, batch, n_mem, 2, H_MEM, DH_MEM)
            ) + tuple(jnp.stack(o) for o in outs)
```

```python
import functools
import math

import jax
import jax.numpy as jnp
from jax import lax
from jax.experimental import pallas as pl
from jax.experimental.pallas import tpu as pltpu

F32 = jnp.float32
I32 = jnp.int32
MXU_DTYPE = jnp.bfloat16

H_A, DH_A, HKV_A = 8, 128, 2
H_IDX, D_IDX = 16, 64
INDEX_TOPK_MAX = 256
H_B, DQK_B, DV_B = 4, 64, 128
H_MEM, DH_MEM = 4, 128
PAGE_SIZE = 128
ROPE_THETA = 500000.0
ROPE_DIV = 4
LN_EPS = 1e-5
RMS_EPS = 1e-5
D_MIX = H_A * DH_A + H_B * DV_B + H_MEM * DH_MEM
N_KVA, N_KVB, N_KVM = 2 * HKV_A, 2 * H_B, 2 * H_MEM

LANES = 128
VMEM_LIMIT_BYTES = 56 * 1024 * 1024

W_QA, W_KVA, W_QB, W_KVB, W_QI, W_QM = 1024, 512, 512, 1024, 1024, 512
C_QA, C_KVA, C_QB, C_KVB, C_QI, C_QM, C_TAIL = 0, 8, 12, 16, 24, 32, 36
N_IN_PAD = 37 * LANES

LOG2_E = math.log2(math.e)
NEG_INF = float("-inf")
NEG_BIG = -1e30
INT_MIN = -(2 ** 31)
NEG_INF_KEY = -2139095041


def _cparams(sem):
    return pltpu.CompilerParams(dimension_semantics=sem, vmem_limit_bytes=VMEM_LIMIT_BYTES)


def _mm(a, b):
    return jnp.dot(a, b, preferred_element_type=F32)


def _mm_nt(a, b):
    return lax.dot_general(a, b, (((1,), (1,)), ((), ())), preferred_element_type=F32)


def _layer_norm(y, g, b):
    mu = jnp.mean(y, axis=-1, keepdims=True)
    d = y - mu
    var = jnp.mean(d * d, axis=-1, keepdims=True)
    return d * lax.rsqrt(var + LN_EPS) * g + b


def _exp_and_sum(s, axis=-1):
    m = jnp.max(s, axis=axis, keepdims=True)
    e = jnp.exp(s - m)
    return e, jnp.sum(e, axis=axis, keepdims=True)


def _ordered_key(x):
    k = pltpu.bitcast(x, I32)
    return jnp.where(k < 0, k ^ jnp.int32(0x7FFFFFFF), k)


def _half_lanes(x, c):
    lane = lax.broadcasted_iota(I32, x.shape, 1)
    return jnp.where((lane >= c * DQK_B) & (lane < (c + 1) * DQK_B), x, 0.0)


def _lambda_value(lq1, lk1, lq2, lk2, lam_init):
    a = jnp.sum(lq1[...] * lk1[...], axis=-1, keepdims=True)
    b = jnp.sum(lq2[...] * lk2[...], axis=-1, keepdims=True)
    return jnp.exp(a) - jnp.exp(b) + lam_init


def _diff_finish(o, g_ref, lam_init):
    o = o * lax.rsqrt(jnp.mean(o * o, axis=-1, keepdims=True) + RMS_EPS)
    return o * g_ref[...] * (1.0 - lam_init)


def _rope_tables(pos, head_dim):
    rot = head_dim // ROPE_DIV
    half = rot // 2
    t = pos.shape[0]
    inv_freq = jnp.float32(ROPE_THETA) ** (-jnp.arange(half, dtype=F32) / half)
    ang = pos.astype(F32)[:, None] * inv_freq[None, :]
    cos, sin = jnp.cos(ang), jnp.sin(ang)
    c = jnp.concatenate([cos, cos, jnp.ones((t, head_dim - rot), F32)], axis=-1)
    s_hi = jnp.concatenate([-sin, jnp.zeros((t, head_dim - half), F32)], axis=-1)
    s_lo = jnp.concatenate([jnp.zeros((t, half), F32), sin, jnp.zeros((t, head_dim - rot), F32)], axis=-1)
    rep = LANES // head_dim
    return tuple(jnp.tile(a, (1, rep)) for a in (c, s_hi, s_lo))


def _proj_kernel(x_ref, w_ref, ca, sa1, sa2, cb, sb1, sb2, *rest, n_prev, attn_layout):
    qa_o, kva_o, qb_o, kvb_o, qi_o, qm_o, ki_o, wi_o = rest[n_prev:n_prev + 8]
    ka_t, vat_t, kb_t, vbt_t, ki_t = rest[n_prev + 8:] if attn_layout else (None,) * 5
    xb = x_ref[...].astype(MXU_DTYPE)
    tm = x_ref.shape[0]

    def rope(z, c, s_hi, s_lo, half):
        return z * c[...] + pltpu.roll(z, LANES - half, 1) * s_hi[...] + pltpu.roll(z, half, 1) * s_lo[...]

    def rope_a(z):
        return rope(z, ca, sa1, sa2, DH_A // ROPE_DIV // 2)

    def rope_b(z):
        return rope(z, cb, sb1, sb2, DQK_B // ROPE_DIV // 2)

    def emit(out_ref, c0, kinds, interleave=False, key_t=None, value_t=None):
        n = len(kinds)
        z = _mm_nt(xb, w_ref[c0 * LANES:(c0 + n) * LANES, :])
        for j, kind in enumerate(kinds):
            zj = z[:, j * LANES:(j + 1) * LANES]
            if kind == "a":
                zj = rope_a(zj)
            elif kind == "b":
                zj = rope_b(zj)
            if interleave:
                out_ref[pl.ds(j, tm, stride=n), :] = zj
                if key_t is not None:
                    if j < n // 2:
                        key_t[j] = zj.astype(MXU_DTYPE)
                    else:
                        value_t[j - n // 2] = zj.T.astype(MXU_DTYPE)
            else:
                out_ref[:, j * LANES:(j + 1) * LANES] = zj

    emit(qa_o, C_QA, "a" * 8)
    emit(kva_o, C_KVA, "aa--", interleave=True, key_t=ka_t, value_t=vat_t)
    emit(qb_o, C_QB, "bbbb")
    emit(kvb_o, C_KVB, "bbbb----", interleave=True, key_t=kb_t, value_t=vbt_t)
    emit(qi_o, C_QI, "b" * 8)
    emit(qm_o, C_QM, "----")
    z = _mm_nt(xb, w_ref[C_TAIL * LANES:(C_TAIL + 1) * LANES, :])
    lane = lax.broadcasted_iota(I32, z.shape, 1)
    zt = jnp.where(lane < D_IDX, rope_b(z), z)
    ki_o[...] = zt[:, :D_IDX]
    wi_o[...] = zt[:, D_IDX:D_IDX + H_IDX]
    if ki_t is not None:
        ki_t[...] = zt[:, :D_IDX].astype(MXU_DTYPE)


CACHE_OUTPUTS = (1, 3, 6)


def _project(x2d, w_in_p, layer, tabs, n_tab_blocks, tm, cache_slots=1, prev_caches=None, attn_batch=None):
    m, d = x2d.shape
    n_blocks = m // tm
    shapes = ((1, W_QA), (N_KVA, LANES), (1, W_QB), (N_KVB, LANES), (1, W_QI), (1, W_QM), (1, D_IDX), (1, H_IDX))
    tab_spec = pl.BlockSpec((tm, LANES), lambda i: (i % n_tab_blocks, 0))
    slot = layer if cache_slots > 1 else 0
    out_specs, out_shape = [], []
    for k, (r, w) in enumerate(shapes):
        slots, base = (cache_slots, slot * n_blocks) if k in CACHE_OUTPUTS else (1, 0)
        out_specs.append(pl.BlockSpec((tm * r, w), lambda i, base=base: (base + i, 0)))
        out_shape.append(jax.ShapeDtypeStruct((slots * m * r, w), F32))
    if attn_batch is not None:
        seq = m // attn_batch
        per_seq = seq // tm
        for heads in (HKV_A, H_B):
            out_specs.append(pl.BlockSpec((None, heads, tm, LANES), lambda i: (i // per_seq, 0, i % per_seq, 0)))
            out_shape.append(jax.ShapeDtypeStruct((attn_batch, heads, seq, LANES), MXU_DTYPE))
            out_specs.append(pl.BlockSpec((None, heads, LANES, tm), lambda i: (i // per_seq, 0, 0, i % per_seq)))
            out_shape.append(jax.ShapeDtypeStruct((attn_batch, heads, LANES, seq), MXU_DTYPE))
        out_specs.append(pl.BlockSpec((None, tm, D_IDX), lambda i: (i // per_seq, i % per_seq, 0)))
        out_shape.append(jax.ShapeDtypeStruct((attn_batch, seq, D_IDX), MXU_DTYPE))
    in_specs = ([pl.BlockSpec((tm, d), lambda i: (i, 0)),
                 pl.BlockSpec((None, N_IN_PAD, d), lambda i: (layer, 0, 0), pipeline_mode=pl.Buffered(1))]
                + [tab_spec] * 6)
    args = [x2d, w_in_p, *tabs]
    aliases = {}
    n_prev = 0
    if prev_caches is not None:
        n_prev = len(CACHE_OUTPUTS)
        aliases = {len(args) + n: k for n, k in enumerate(CACHE_OUTPUTS)}
        in_specs += [pl.BlockSpec(memory_space=pl.ANY)] * n_prev
        args += list(prev_caches)
    return pl.pallas_call(
        functools.partial(_proj_kernel, n_prev=n_prev, attn_layout=attn_batch is not None),
        grid=(n_blocks,),
        in_specs=in_specs,
        out_specs=out_specs,
        out_shape=out_shape,
        input_output_aliases=aliases,
        compiler_params=_cparams(("parallel",)),
        name="proj_rope",
    )(*args)


def _matmul_kernel(x_ref, w_ref, *rest):
    o_ref = rest[-1]
    tm = x_ref.shape[0]
    n = w_ref.shape[1] // LANES
    z = _mm(x_ref[...].astype(MXU_DTYPE), w_ref[...])
    for j in range(n):
        o_ref[pl.ds(j, tm, stride=n), :] = z[:, j * LANES:(j + 1) * LANES]


def _matmul(x2d, w, layer, tm, prev=None):
    m, d = x2d.shape
    depth, _, n = w.shape
    n_blocks = m // tm
    in_specs = [pl.BlockSpec((tm, d), lambda i: (i, 0)),
                pl.BlockSpec((None, d, n), lambda i: (layer, 0, 0), pipeline_mode=pl.Buffered(1))]
    args = [x2d, w]
    aliases = {}
    if prev is not None:
        in_specs.append(pl.BlockSpec(memory_space=pl.ANY))
        aliases = {len(args): 0}
        args.append(prev)
    return pl.pallas_call(
        _matmul_kernel,
        grid=(n_blocks,),
        in_specs=in_specs,
        out_specs=pl.BlockSpec((tm * n // LANES, LANES), lambda i: (layer * n_blocks + i, 0)),
        out_shape=jax.ShapeDtypeStruct((depth * m * n // LANES, LANES), F32),
        input_output_aliases=aliases,
        compiler_params=_cparams(("parallel",)),
        name="mem_kv_proj",
    )(*args)


def _kth_threshold(key_ref, rows, n_chunks, topk, groups=1):
    ones = jnp.ones((LANES, LANES), MXU_DTYPE)
    gr = rows // groups

    def count_ge(cand, g):
        acc = jnp.zeros((gr, LANES), F32)
        for c in range(n_chunks):
            acc = acc + jnp.where(key_ref[g * gr:(g + 1) * gr, c * LANES:(c + 1) * LANES] >= cand, 1.0, 0.0)
        return _mm(acc.astype(MXU_DTYPE), ones)

    def body(it, ts):
        bit = lax.shift_left(jnp.int32(1), jnp.int32(31) - it)
        return tuple(jnp.where(count_ge(t + bit, g) >= topk, t + bit, t) for g, t in enumerate(ts))

    ts = lax.fori_loop(0, 32, body, tuple(jnp.full((gr, LANES), INT_MIN, I32) for _ in range(groups)))
    n_ge = [count_ge(t, g) for g, t in enumerate(ts)]
    return jnp.concatenate(ts, axis=0), jnp.concatenate(n_ge, axis=0)


def _selection_bias(key_ref, bias_ref, t, n_ge, rows, n_chunks, topk):
    floor_t = jnp.maximum(t, jnp.int32(NEG_INF_KEY + 1))
    tie_rows = jnp.where((n_ge > topk) & (t > NEG_INF_KEY), 1.0, 0.0)
    has_tie = jnp.max(tie_rows) > 0.5

    @pl.when(jnp.logical_not(has_tie))
    def _():
        for c in range(n_chunks):
            sl = slice(c * LANES, (c + 1) * LANES)
            bias_ref[:, sl] = jnp.where(key_ref[:, sl] >= floor_t, 0.0, NEG_INF)

    @pl.when(has_tie)
    def _():
        ones = jnp.ones((LANES, LANES), MXU_DTYPE)
        r_i = lax.broadcasted_iota(I32, (LANES, LANES), 0)
        c_i = lax.broadcasted_iota(I32, (LANES, LANES), 1)
        strict_upper = jnp.where(r_i < c_i, 1.0, 0.0).astype(MXU_DTYPE)
        n_gt = jnp.zeros((rows, LANES), F32)
        for c in range(n_chunks):
            sl = slice(c * LANES, (c + 1) * LANES)
            n_gt = n_gt + _mm(jnp.where(key_ref[:, sl] > t, 1.0, 0.0).astype(MXU_DTYPE), ones)
        need = topk - n_gt
        run = jnp.zeros((rows, LANES), F32)
        for c in range(n_chunks):
            sl = slice(c * LANES, (c + 1) * LANES)
            k = key_ref[:, sl]
            eq = jnp.where(k == t, 1.0, 0.0).astype(MXU_DTYPE)
            before = run + _mm(eq, strict_upper)
            run = run + _mm(eq, ones)
            keep = (k > t) | ((k == t) & (before < need))
            bias_ref[:, sl] = jnp.where(keep & (k > NEG_INF_KEY), 0.0, NEG_INF)


def _count_rows(pred_fn, key_ref, n_keys):
    acc = jnp.zeros((LANES, LANES), F32)
    for c in range(n_keys // LANES):
        acc = acc + jnp.where(pred_fn(key_ref[c * LANES:(c + 1) * LANES, :]), 1.0, 0.0)
    return jnp.sum(acc, axis=0, keepdims=True)


def _kth_threshold_t(key_ref, n_keys, topk):
    def body(it, t):
        cand = t + lax.shift_left(jnp.int32(1), jnp.int32(31) - it)
        return jnp.where(_count_rows(lambda k: k >= cand, key_ref, n_keys) >= topk, cand, t)

    t = lax.fori_loop(0, 32, body, jnp.full((1, LANES), INT_MIN, I32))
    return t, _count_rows(lambda k: k >= t, key_ref, n_keys)


def _selection_bias_t(key_ref, store_bias, t, n_ge, n_keys, topk):
    floor_t = jnp.maximum(t, jnp.int32(NEG_INF_KEY + 1))
    tie_lanes = jnp.where((n_ge > topk) & (t > NEG_INF_KEY), 1.0, 0.0)
    has_tie = jnp.max(tie_lanes) > 0.5

    @pl.when(jnp.logical_not(has_tie))
    def _():
        for c in range(n_keys // LANES):
            sl = slice(c * LANES, (c + 1) * LANES)
            store_bias(sl, jnp.where(key_ref[sl, :] >= floor_t, 0.0, NEG_BIG))

    @pl.when(has_tie)
    def _():
        r_i = lax.broadcasted_iota(I32, (LANES, LANES), 0)
        c_i = lax.broadcasted_iota(I32, (LANES, LANES), 1)
        strict_lower = jnp.where(c_i < r_i, 1.0, 0.0).astype(MXU_DTYPE)
        need = topk - _count_rows(lambda k: k > t, key_ref, n_keys)
        run = jnp.zeros((1, LANES), F32)
        for c in range(n_keys // LANES):
            sl = slice(c * LANES, (c + 1) * LANES)
            k = key_ref[sl, :]
            eq = jnp.where(k == t, 1.0, 0.0)
            before = run + _mm(strict_lower, eq.astype(MXU_DTYPE))
            run = run + jnp.sum(eq, axis=0, keepdims=True)
            keep = (k > t) | ((k == t) & (before < need))
            store_bias(sl, jnp.where(keep & (k > NEG_INF_KEY), 0.0, NEG_BIG))


Q_BLOCK = 128
N_KV_CLASSES = 8


def _prompt_attn_kernel(*refs, topk, lam_init, q0):
    (qa_ref, ka_ref, vat_ref, qi_ref, wit_ref, ki_ref, qb_ref, kb_ref, vbt_ref, qm_ref, mem_ref,
     lq1, lk1, lq2, lk2, g_ref, _, o_ref, kab_s, kbb_s, km_s, vmt_s, key_ref) = refs
    n_keys = ki_ref.shape[0]
    n_mem = km_s.shape[1]
    qi = pl.program_id(1)
    key_lanes = slice(0, LANES)
    mask_lanes = slice(LANES, 2 * LANES)

    @pl.when(qi == 0)
    def _():
        for g in range(HKV_A):
            kab_s[g, :, key_lanes] = ka_ref[g]
        for h in range(H_B):
            kbb_s[h, :, key_lanes] = kb_ref[h]
        for h in range(H_MEM):
            km_s[h] = mem_ref[pl.ds(h, n_mem, stride=N_KVM), :].astype(MXU_DTYPE)
            vmt_s[h] = mem_ref[pl.ds(H_MEM + h, n_mem, stride=N_KVM), :].T.astype(MXU_DTYPE)

    q_pos = (q0 + qi) * Q_BLOCK + lax.broadcasted_iota(I32, (n_keys, Q_BLOCK), 1)
    k_pos = lax.broadcasted_iota(I32, (n_keys, Q_BLOCK), 0)
    causal = k_pos <= q_pos

    causal_mask = jnp.where(causal, 0.0, NEG_BIG).astype(MXU_DTYPE)

    eye = jnp.where(lax.broadcasted_iota(I32, (Q_BLOCK, Q_BLOCK), 0) == lax.broadcasted_iota(I32, (Q_BLOCK, Q_BLOCK), 1),
                    1.0, 0.0)

    def with_mask_selector(q_blocks, log2_scale):
        q = jnp.concatenate([qb_ * log2_scale for qb_ in q_blocks], axis=0)
        return jnp.concatenate([q, jnp.concatenate([eye] * len(q_blocks), axis=0)], axis=1).astype(MXU_DTYPE)

    def softmax_cols(s, n_cols):
        es, ls = [], []
        for r in range(n_cols):
            x = s[:, r * Q_BLOCK:(r + 1) * Q_BLOCK]
            e = jnp.exp2(x - jnp.max(x, axis=0, keepdims=True))
            es.append(e.astype(MXU_DTYPE))
            ls.append(jnp.sum(e, axis=0, keepdims=True))
        return jnp.concatenate(es, axis=1), ls


    lam = _lambda_value(lq1, lk1, lq2, lk2, lam_init)
    off_b = H_A * DH_A
    for h in range(H_B):
        kbb_s[h, :, mask_lanes] = causal_mask
        qh = qb_ref[:, h * DV_B:(h + 1) * DV_B]
        q2 = with_mask_selector([_half_lanes(qh, 0), _half_lanes(qh, 1)], DQK_B ** -0.5 * LOG2_E)
        e, ls = softmax_cols(_mm_nt(kbb_s[h], q2), 2)
        ot = _mm(vbt_ref[h], e)
        o = (ot[:, :Q_BLOCK] / ls[0] - lam * (ot[:, Q_BLOCK:] / ls[1])).T
        o_ref[:, off_b + h * DV_B:off_b + (h + 1) * DV_B] = _diff_finish(o, g_ref, lam_init).astype(o_ref.dtype)

    off_m = off_b + H_B * DV_B
    for h in range(H_MEM):
        q = (qm_ref[:, h * DH_MEM:(h + 1) * DH_MEM] * (DH_MEM ** -0.5 * LOG2_E)).astype(MXU_DTYPE)
        e, ls = softmax_cols(_mm_nt(km_s[h], q), 1)
        o_ref[:, off_m + h * DH_MEM:off_m + (h + 1) * DH_MEM] = (_mm(vmt_s[h], e) / ls[0]).T.astype(o_ref.dtype)

    def store_selection(rows, block):
        for g in range(HKV_A):
            kab_s[g, rows, mask_lanes] = block.astype(MXU_DTYPE)

    if n_keys <= topk:
        store_selection(slice(0, n_keys), causal_mask)
    else:
        ki = ki_ref[...]
        sc = jnp.zeros((n_keys, Q_BLOCK), F32)
        for hp in range(H_IDX // 2):
            q2 = jnp.concatenate([qi_ref[:, (2 * hp + j) * D_IDX:(2 * hp + j + 1) * D_IDX] for j in range(2)],
                                 axis=0).astype(MXU_DTYPE)
            s = _mm_nt(ki, q2)
            pair = (wit_ref[2 * hp:2 * hp + 1, :] * jnp.maximum(s[:, :Q_BLOCK], 0.0)
                    + wit_ref[2 * hp + 1:2 * hp + 2, :] * jnp.maximum(s[:, Q_BLOCK:], 0.0))
            sc = sc + pair
        key_ref[...] = _ordered_key(jnp.where(causal, sc, NEG_INF))
        t, n_ge = _kth_threshold_t(key_ref, n_keys, topk)
        _selection_bias_t(key_ref, store_selection, t, n_ge, n_keys, topk)

    rep = H_A // HKV_A
    for g in range(HKV_A):
        q4 = with_mask_selector([qa_ref[:, (g * rep + r) * DH_A:(g * rep + r + 1) * DH_A] for r in range(rep)],
                                DH_A ** -0.5 * LOG2_E)
        e, ls = softmax_cols(_mm_nt(kab_s[g], q4), rep)
        ot = _mm(vat_ref[g], e)
        for r in range(rep):
            h = g * rep + r
            o_ref[:, h * DH_A:(h + 1) * DH_A] = (ot[:, r * Q_BLOCK:(r + 1) * Q_BLOCK] / ls[r]).T.astype(o_ref.dtype)


def _prompt_attention(qa, ka, vat, qi, wit, ki, qb, kb, vbt, qm, mem_kv, lam_params, g, batch, seq, layer, lam_init):
    depth = lam_params[0].shape[0]
    n_q = seq // Q_BLOCK
    n_classes = min(N_KV_CLASSES, n_q)
    q_per_class = n_q // n_classes
    n_mem = mem_kv.shape[0] // (depth * batch * N_KVM)
    topk = min(INDEX_TOPK_MAX, seq // 4)
    mem3 = mem_kv.reshape(depth * batch, n_mem * N_KVM, LANES)
    vec_spec = pl.BlockSpec((None, 1, DQK_B), lambda b, i: (layer, 0, 0))

    mixed = jnp.zeros((batch * seq, D_MIX), MXU_DTYPE)
    for c in range(n_classes):
        q0 = c * q_per_class
        n_keys = (c + 1) * q_per_class * Q_BLOCK

        def q_spec(w, q0=q0):
            return pl.BlockSpec((Q_BLOCK, w), lambda b, i: (b * n_q + q0 + i, 0))

        def k_spec(heads):
            return pl.BlockSpec((None, heads, n_keys, LANES), lambda b, i: (b, 0, 0, 0))

        def vt_spec(heads):
            return pl.BlockSpec((None, heads, LANES, n_keys), lambda b, i: (b, 0, 0, 0))

        in_specs = [q_spec(W_QA), k_spec(HKV_A), vt_spec(HKV_A), q_spec(W_QI),
                    pl.BlockSpec((H_IDX, Q_BLOCK), lambda b, i, q0=q0: (0, b * n_q + q0 + i)),
                    pl.BlockSpec((None, n_keys, D_IDX), lambda b, i: (b, 0, 0)),
                    q_spec(W_QB), k_spec(H_B), vt_spec(H_B), q_spec(W_QM),
                    pl.BlockSpec((None, n_mem * N_KVM, LANES), lambda b, i: (layer * batch + b, 0, 0)),
                    vec_spec, vec_spec, vec_spec, vec_spec,
                    pl.BlockSpec((None, 1, DV_B), lambda b, i: (layer, 0, 0)),
                    pl.BlockSpec(memory_space=pl.ANY)]
        args = [qa, ka, vat, qi, wit, ki, qb, kb, vbt, qm, mem3, *lam_params, g, mixed]
        mixed = pl.pallas_call(
            functools.partial(_prompt_attn_kernel, topk=topk, lam_init=lam_init, q0=q0),
            grid=(batch, q_per_class),
            in_specs=in_specs,
            out_specs=q_spec(D_MIX),
            out_shape=jax.ShapeDtypeStruct((batch * seq, D_MIX), MXU_DTYPE),
            scratch_shapes=[pltpu.VMEM((HKV_A, n_keys, 2 * LANES), MXU_DTYPE),
                            pltpu.VMEM((H_B, n_keys, 2 * LANES), MXU_DTYPE),
                            pltpu.VMEM((H_MEM, n_mem, DH_MEM), MXU_DTYPE),
                            pltpu.VMEM((H_MEM, DH_MEM, n_mem), MXU_DTYPE),
                            pltpu.VMEM((n_keys, Q_BLOCK), I32)],
            input_output_aliases={len(args) - 1: 0},
            compiler_params=_cparams(("parallel", "arbitrary")),
            name=f"prompt_attention_kv{c}",
        )(*args)
    return mixed


def _pad_rows(x, rows):
    return jnp.concatenate([x, jnp.zeros((rows - x.shape[0], x.shape[1]), x.dtype)], axis=0)


def _sample_index_kernel(pt_ref, qi_ref, wcol_ref, kin_ref, *rest, n_pages, group):
    del pt_ref
    idx_pages = rest[:group * n_pages]
    key_o = rest[group * n_pages]
    t_new = qi_ref.shape[0] // group
    past = n_pages * PAGE_SIZE

    lane = lax.broadcasted_iota(I32, (t_new, LANES), 1)
    tok = lax.broadcasted_iota(I32, (t_new, LANES), 0)
    new_ok = lane <= tok

    for b in range(group):
        rows = slice(b * t_new, (b + 1) * t_new)
        q_ht = jnp.concatenate([qi_ref[rows, h * D_IDX:(h + 1) * D_IDX] for h in range(H_IDX)],
                               axis=0).astype(MXU_DTYPE)
        wcol = wcol_ref[b * H_IDX * t_new:(b + 1) * H_IDX * t_new, :]

        def index_scores(qk):
            s = jnp.maximum(qk, 0.0) * wcol
            acc = s[0:t_new]
            for h in range(1, H_IDX):
                acc = acc + s[h * t_new:(h + 1) * t_new]
            return acc

        for j in range(n_pages):
            qk = _mm(q_ht, idx_pages[b * n_pages + j][...].astype(MXU_DTYPE))
            key_o[rows, j * LANES:(j + 1) * LANES] = _ordered_key(index_scores(qk))
        s_new = index_scores(_mm_nt(q_ht, _pad_rows(kin_ref[rows, :], LANES).astype(MXU_DTYPE)))
        key_o[rows, past:past + LANES] = _ordered_key(jnp.where(new_ok, s_new, NEG_INF))


def _sample_index(page_table, qi, wcol, ki_new, pool_idx, batch, t_new, layer, n_phys, group):
    n_pages = page_table.shape[1]
    width = (n_pages + 1) * LANES

    def row_spec(rows, w):
        return pl.BlockSpec((rows, w), lambda s, pt: (s, 0))

    def page_spec(b, j):
        return pl.BlockSpec((None, D_IDX, PAGE_SIZE),
                            lambda s, pt: (layer * n_phys + pt[s * group + b, j], 0, 0))

    in_specs = ([row_spec(group * t_new, W_QI), row_spec(group * H_IDX * t_new, 1), row_spec(group * t_new, D_IDX)]
                + [page_spec(b, j) for b in range(group) for j in range(n_pages)])
    grid_spec = pltpu.PrefetchScalarGridSpec(
        num_scalar_prefetch=1, grid=(batch // group,), in_specs=in_specs,
        out_specs=pl.BlockSpec((group * t_new, width), lambda s, pt: (s, 0)))
    return pl.pallas_call(
        functools.partial(_sample_index_kernel, n_pages=n_pages, group=group),
        grid_spec=grid_spec,
        out_shape=jax.ShapeDtypeStruct((batch * t_new, width), I32),
        compiler_params=_cparams(("parallel",)),
        name="sample_indexer",
    )(page_table, qi, wcol, ki_new, *([pool_idx] * (group * n_pages)))


def _sample_select_kernel(key_ref, bias_ref, *, topk):
    rows, width = key_ref.shape
    n_chunks = width // LANES
    t, n_ge = _kth_threshold(key_ref, rows, n_chunks, topk, groups=4 if rows % 64 == 0 else 1)
    _selection_bias(key_ref, bias_ref, t, n_ge, rows, n_chunks, topk)


def _sample_select(keys, topk, tr):
    m, width = keys.shape
    spec = pl.BlockSpec((tr, width), lambda i: (i, 0))
    return pl.pallas_call(
        functools.partial(_sample_select_kernel, topk=topk),
        grid=(m // tr,),
        in_specs=[spec],
        out_specs=spec,
        out_shape=jax.ShapeDtypeStruct((m, width), F32),
        compiler_params=_cparams(("parallel",)),
        name="sample_topk_select",
    )(keys)


def _sample_attn_kernel(pt_ref, qa_ref, kvan_ref, bias_ref, qb_ref, kvbn_ref, qm_ref, mem_ref,
                        lq1, lk1, lq2, lk2, g_ref, *rest, n_pages, lam_init):
    del pt_ref
    kva_pages = rest[:n_pages]
    kvb_pages = rest[n_pages:2 * n_pages]
    o_ref, sa_ref, sb_ref = rest[2 * n_pages:]
    t_new = qa_ref.shape[0]
    past = n_pages * PAGE_SIZE

    lane = lax.broadcasted_iota(I32, (t_new, LANES), 1)
    tok = lax.broadcasted_iota(I32, (t_new, LANES), 0)
    new_ok = lane <= tok

    def chunk_rows(ref, j, rows, n):
        return ref[pl.ds(j, rows, stride=n), :].astype(MXU_DTYPE)

    def new_rows(ref, j, n):
        return _pad_rows(ref[pl.ds(j, t_new, stride=n), :], LANES).astype(MXU_DTYPE)

    scale_a = DH_A ** -0.5
    rep = H_A // HKV_A
    bias_a = jnp.concatenate([bias_ref[...]] * rep, axis=0)
    for g in range(HKV_A):
        q = jnp.concatenate([qa_ref[:, (g * rep + r) * DH_A:(g * rep + r + 1) * DH_A] for r in range(rep)],
                            axis=0).astype(MXU_DTYPE)
        for j in range(n_pages):
            sa_ref[:, j * LANES:(j + 1) * LANES] = _mm_nt(q, chunk_rows(kva_pages[j], g, PAGE_SIZE, N_KVA))
        sa_ref[:, past:past + LANES] = _mm_nt(q, new_rows(kvan_ref, g, N_KVA))
        e, l = _exp_and_sum(sa_ref[...] * scale_a + bias_a)
        e = e.astype(MXU_DTYPE)
        acc = _mm(e[:, past:past + LANES], new_rows(kvan_ref, HKV_A + g, N_KVA))
        for j in range(n_pages):
            acc = acc + _mm(e[:, j * LANES:(j + 1) * LANES], chunk_rows(kva_pages[j], HKV_A + g, PAGE_SIZE, N_KVA))
        acc = acc / l
        for r in range(rep):
            h = g * rep + r
            o_ref[:, h * DH_A:(h + 1) * DH_A] = acc[r * t_new:(r + 1) * t_new]

    lam = _lambda_value(lq1, lk1, lq2, lk2, lam_init)
    scale_b = DQK_B ** -0.5
    for h in range(H_B):
        qh = qb_ref[:, h * DV_B:(h + 1) * DV_B]
        q = jnp.concatenate([_half_lanes(qh, 0), _half_lanes(qh, 1)], axis=0).astype(MXU_DTYPE)
        rows = slice(2 * h * t_new, (2 * h + 2) * t_new)
        for j in range(n_pages):
            sb_ref[rows, j * LANES:(j + 1) * LANES] = _mm_nt(q, chunk_rows(kvb_pages[j], h, PAGE_SIZE, N_KVB))
        sb_ref[rows, past:past + LANES] = _mm_nt(q, new_rows(kvbn_ref, h, N_KVB))
    new_bias = jnp.where(new_ok, 0.0, NEG_INF)
    bias_b = jnp.concatenate(
        [jnp.zeros((N_KVB * t_new, past), F32), jnp.concatenate([new_bias] * N_KVB, axis=0)], axis=1)
    e, l = _exp_and_sum(sb_ref[...] * scale_b + bias_b)
    e = e.astype(MXU_DTYPE)
    off_b = H_A * DH_A
    for h in range(H_B):
        rows = slice(2 * h * t_new, (2 * h + 2) * t_new)
        acc = _mm(e[rows, past:past + LANES], new_rows(kvbn_ref, H_B + h, N_KVB))
        for j in range(n_pages):
            acc = acc + _mm(e[rows, j * LANES:(j + 1) * LANES], chunk_rows(kvb_pages[j], H_B + h, PAGE_SIZE, N_KVB))
        acc = acc / l[rows]
        o = acc[0:t_new] - lam * acc[t_new:2 * t_new]
        o_ref[:, off_b + h * DV_B:off_b + (h + 1) * DV_B] = _diff_finish(o, g_ref, lam_init)

    scale_m = DH_MEM ** -0.5
    off_m = off_b + H_B * DV_B
    n_mem = mem_ref.shape[0] // N_KVM
    for h in range(H_MEM):
        q = qm_ref[:, h * DH_MEM:(h + 1) * DH_MEM].astype(MXU_DTYPE)
        e, l = _exp_and_sum(_mm_nt(q, chunk_rows(mem_ref, h, n_mem, N_KVM)) * scale_m)
        o_ref[:, off_m + h * DH_MEM:off_m + (h + 1) * DH_MEM] = (
            _mm(e.astype(MXU_DTYPE), chunk_rows(mem_ref, H_MEM + h, n_mem, N_KVM)) / l)


def _sample_attention(page_table, qa, kva_new, bias, qb, kvb_new, qm, pool_kva, pool_kvb, pool_mem,
                      lam_params, g, batch, t_new, layer, lam_init, n_phys, mem_rows):
    n_pages = page_table.shape[1]
    width = (n_pages + 1) * LANES

    def row_spec(rows, w):
        return pl.BlockSpec((rows, w), lambda b, pt: (b, 0))

    def page_spec(j, n):
        return pl.BlockSpec((PAGE_SIZE * n, LANES), lambda b, pt: (layer * n_phys + pt[b, j], 0))

    vec_spec = pl.BlockSpec((None, 1, DQK_B), lambda b, pt: (layer, 0, 0))
    in_specs = ([row_spec(t_new, W_QA), row_spec(t_new * N_KVA, LANES), row_spec(t_new, width),
                 row_spec(t_new, W_QB), row_spec(t_new * N_KVB, LANES), row_spec(t_new, W_QM),
                 pl.BlockSpec((mem_rows, LANES), lambda b, pt: (layer * batch + b, 0)),
                 vec_spec, vec_spec, vec_spec, vec_spec,
                 pl.BlockSpec((None, 1, DV_B), lambda b, pt: (layer, 0, 0))]
                + [page_spec(j, N_KVA) for j in range(n_pages)]
                + [page_spec(j, N_KVB) for j in range(n_pages)])
    grid_spec = pltpu.PrefetchScalarGridSpec(
        num_scalar_prefetch=1,
        grid=(batch,),
        in_specs=in_specs,
        out_specs=pl.BlockSpec((t_new, D_MIX), lambda b, pt: (b, 0)),
        scratch_shapes=[pltpu.VMEM((H_A // HKV_A * t_new, width), F32),
                        pltpu.VMEM((N_KVB * t_new, width), F32)],
    )
    return pl.pallas_call(
        functools.partial(_sample_attn_kernel, n_pages=n_pages, lam_init=lam_init),
        grid_spec=grid_spec,
        out_shape=jax.ShapeDtypeStruct((batch * t_new, D_MIX), F32),
        compiler_params=_cparams(("parallel",)),
        name="sample_attention",
    )(page_table, qa, kva_new, bias, qb, kvb_new, qm, pool_mem, *lam_params, g,
      *([pool_kva] * n_pages), *([pool_kvb] * n_pages))


def _attn_out_kernel(x_ref, mix_ref, wo_ref, g_ref, b_ref, h_ref, *, alpha):
    half = x_ref.shape[0] // 2
    for rows in (slice(0, half), slice(half, 2 * half)):
        a = _mm(mix_ref[rows, :].astype(MXU_DTYPE), wo_ref[...])
        h_ref[rows, :] = _layer_norm(alpha * x_ref[rows, :] + a, g_ref[...], b_ref[...])


def _attn_out(x2d, mixed, w_o, ln_g, ln_b, layer, alpha, tm):
    m, d = x2d.shape
    row = pl.BlockSpec((tm, d), lambda i: (i, 0))
    vec = pl.BlockSpec((None, 1, d), lambda i: (layer, 0, 0))
    return pl.pallas_call(
        functools.partial(_attn_out_kernel, alpha=alpha),
        grid=(m // tm,),
        in_specs=[row, pl.BlockSpec((tm, mixed.shape[1]), lambda i: (i, 0)),
                  pl.BlockSpec((None, mixed.shape[1], d), lambda i: (layer, 0, 0), pipeline_mode=pl.Buffered(1)),
                  vec, vec],
        out_specs=row,
        out_shape=jax.ShapeDtypeStruct((m, d), F32),
        compiler_params=_cparams(("parallel",)),
        name="attn_out_ln",
    )(x2d, mixed, w_o, ln_g, ln_b)


def _ffn_kernel(h_ref, wg_ref, wu_ref, wd_ref, g_ref, b_ref, o_ref, hb_ref, acc_ref, *, alpha):
    j = pl.program_id(1)

    @pl.when(j == 0)
    def _():
        hb_ref[...] = h_ref[...].astype(MXU_DTYPE)
        acc_ref[...] = jnp.zeros_like(acc_ref)

    hb = hb_ref[...]
    gate = _mm(hb, wg_ref[...])
    up = _mm(hb, wu_ref[...])
    act = gate * jax.nn.sigmoid(gate) * up
    acc_ref[...] += _mm(act.astype(MXU_DTYPE), wd_ref[...])

    @pl.when(j == pl.num_programs(1) - 1)
    def _():
        o_ref[...] = _layer_norm(alpha * h_ref[...] + acc_ref[...], g_ref[...], b_ref[...])


def _ffn(h2d, w_gate, w_up, w_down, ln_g, ln_b, layer, alpha, tm, tf):
    m, d = h2d.shape
    f = w_gate.shape[-1]
    row = pl.BlockSpec((tm, d), lambda i, j: (i, 0))
    vec = pl.BlockSpec((None, 1, d), lambda i, j: (layer, 0, 0))
    return pl.pallas_call(
        functools.partial(_ffn_kernel, alpha=alpha),
        grid=(m // tm, f // tf),
        in_specs=[row,
                  pl.BlockSpec((None, d, tf), lambda i, j: (layer, 0, j)),
                  pl.BlockSpec((None, d, tf), lambda i, j: (layer, 0, j)),
                  pl.BlockSpec((None, tf, d), lambda i, j: (layer, j, 0)),
                  vec, vec],
        out_specs=row,
        out_shape=jax.ShapeDtypeStruct((m, d), F32),
        scratch_shapes=[pltpu.VMEM((tm, d), MXU_DTYPE), pltpu.VMEM((tm, d), F32)],
        compiler_params=_cparams(("parallel", "arbitrary")),
        name="swiglu_ln",
    )(h2d, w_gate, w_up, w_down, ln_g, ln_b)


def _row_tile(m, cap):
    t = min(m, cap)
    while m % t:
        t //= 2
    return t


def _ff_tile(f, cap):
    best = LANES
    for t in range(LANES, cap + 1, LANES):
        if f % t == 0:
            best = t
    return best


def _largest_divisor(n, cap):
    return max(d for d in range(1, cap + 1) if n % d == 0)


def kernel(x_prompt, x_sample, mem_prompt, cache_kv_a, cache_idx_k, cache_kv_b, cache_mem_kv, page_table,
           w_in, w_mem_kv, lambda_q1, lambda_k1, lambda_q2, lambda_k2, subln_g, w_o,
           ln1_g, ln1_b, w_gate, w_up, w_down, ln2_g, ln2_b):
    depth, d_model, _ = w_in.shape
    batch, seq, _ = x_prompt.shape
    dec_batch, t_new, _ = x_sample.shape
    n_pages = page_table.shape[1]
    past = n_pages * PAGE_SIZE
    n_mem = mem_prompt.shape[1]
    n_phys = cache_idx_k.shape[1]
    alpha = (2.0 * depth) ** 0.25

    n_main = W_QA + W_KVA + W_QB + W_KVB
    w_t = jnp.swapaxes(w_in, 1, 2)
    w_in_p = jnp.concatenate(
        [w_t[:, :n_main], w_t[:, n_main + W_QM:n_main + W_QM + W_QI], w_t[:, n_main:n_main + W_QM],
         w_t[:, n_main + W_QM + W_QI:],
         jnp.zeros((depth, N_IN_PAD - w_in.shape[2], d_model), w_in.dtype)], axis=1).astype(MXU_DTYPE)
    w_mem_c = w_mem_kv.astype(MXU_DTYPE)
    w_o_c, w_gate_c, w_up_c, w_down_c = (w.astype(MXU_DTYPE) for w in (w_o, w_gate, w_up, w_down))
    vec3 = lambda a: a.astype(F32).reshape(depth, 1, a.shape[-1])
    lam_params = tuple(vec3(a) for a in (lambda_q1, lambda_k1, lambda_q2, lambda_k2))
    g_sub = vec3(subln_g)
    ln1g, ln1b, ln2g, ln2b = (vec3(a) for a in (ln1_g, ln1_b, ln2_g, ln2_b))

    pool_kva = cache_kv_a.reshape(-1, LANES)
    pool_idx = jnp.swapaxes(cache_idx_k, 2, 3).reshape(depth * n_phys, D_IDX, PAGE_SIZE)
    pool_kvb = cache_kv_b.reshape(-1, LANES)
    pool_mem = cache_mem_kv.reshape(-1, LANES)

    tm_p = _row_tile(seq, 512)
    m_s = dec_batch * t_new
    tm_s = _row_tile(m_s, 512)
    pos_p = jnp.arange(seq, dtype=I32)
    pos_s = past + (jnp.arange(tm_s, dtype=I32) % t_new)
    tabs_p = _rope_tables(pos_p, DH_A) + _rope_tables(pos_p, DQK_B)
    tabs_s = _rope_tables(pos_s, DH_A) + _rope_tables(pos_s, DQK_B)

    xp = x_prompt.reshape(batch * seq, d_model)
    xs = x_sample.reshape(m_s, d_model)
    mem2d = mem_prompt.reshape(batch * n_mem, d_model)
    tf = _ff_tile(w_gate.shape[-1], 512)
    topk_s = min(INDEX_TOPK_MAX, (past + t_new) // 4)
    idx_group = _largest_divisor(dec_batch, 4)

    outs = [[] for _ in range(3)]
    m_p = batch * seq
    caches = tuple(jnp.zeros((depth * m_p * r, w), F32) for r, w in ((N_KVA, LANES), (N_KVB, LANES), (1, D_IDX)))
    mem_kv = jnp.zeros((depth * batch * n_mem * N_KVM, LANES), F32)
    for l in range(depth):
        lam_init = 0.8 - 0.6 * math.exp(-0.3 * l)

        qa, kva, qb, kvb, qi, qm, ki, wi, ka_t, vat_t, kb_t, vbt_t, ki_t = _project(
            xp, w_in_p, l, tabs_p, seq // tm_p, tm_p, cache_slots=depth, prev_caches=caches, attn_batch=batch)
        caches = (kva, kvb, ki)
        mem_kv = _matmul(mem2d, w_mem_c, l, _row_tile(batch * n_mem, 512), prev=mem_kv)
        mixed = _prompt_attention(qa, ka_t, vat_t, qi, wi.T, ki_t, qb, kb_t, vbt_t, qm, mem_kv, lam_params, g_sub,
                                  batch, seq, l, lam_init)
        h = _attn_out(xp, mixed, w_o_c, ln1g, ln1b, l, alpha, tm_p)
        xp = _ffn(h, w_gate_c, w_up_c, w_down_c, ln2g, ln2b, l, alpha, tm_p, tf)

        qa, kva, qb, kvb, qi, qm, ki, wi = _project(xs, w_in_p, l, tabs_s, 1, tm_s)
        wcol = wi.reshape(dec_batch, t_new, H_IDX).transpose(0, 2, 1).reshape(dec_batch * H_IDX * t_new, 1)
        keys = _sample_index(page_table, qi, wcol, ki, pool_idx, dec_batch, t_new, l, n_phys, idx_group)
        bias = _sample_select(keys, topk_s, _row_tile(m_s, 256))
        mixed = _sample_attention(page_table, qa, kva, bias, qb, kvb, qm, pool_kva, pool_kvb, pool_mem,
                                  lam_params, g_sub, dec_batch, t_new, l, lam_init, n_phys, n_mem * N_KVM)
        h = _attn_out(xs, mixed, w_o_c, ln1g, ln1b, l, alpha, tm_s)
        xs = _ffn(h, w_gate_c, w_up_c, w_down_c, ln2g, ln2b, l, alpha, tm_s, tf)
        outs[0].append(kva.reshape(dec_batch, t_new, 2, HKV_A, DH_A))
        outs[1].append(ki.reshape(dec_batch, t_new, D_IDX))
        outs[2].append(kvb.reshape(dec_batch, t_new, 2, H_B, DV_B))

    kva_p, kvb_p, ki_p = caches
    return (xp.reshape(batch, seq, d_model), xs.reshape(dec_batch, t_new, d_model),
            kva_p.reshape(depth, batch, seq, 2, HKV_A, DH_A), ki_p.reshape(depth, batch, seq, D_IDX),
            kvb_p.reshape(depth, batch, seq, 2, H_B, DV_B), mem_kv.reshape(depth, batch, n_mem, 2, H_MEM, DH_MEM)
            ) + tuple(jnp.stack(o) for o in outs)
```

```python
import functools
import math

import jax
import jax.numpy as jnp
from jax import lax
from jax.experimental import pallas as pl
from jax.experimental.pallas import tpu as pltpu

F32 = jnp.float32
I32 = jnp.int32
MXU_DTYPE = jnp.bfloat16

H_A, DH_A, HKV_A = 8, 128, 2
H_IDX, D_IDX = 16, 64
INDEX_TOPK_MAX = 256
H_B, DQK_B, DV_B = 4, 64, 128
H_MEM, DH_MEM = 4, 128
PAGE_SIZE = 128
ROPE_THETA = 500000.0
ROPE_DIV = 4
LN_EPS = 1e-5
RMS_EPS = 1e-5
D_MIX = H_A * DH_A + H_B * DV_B + H_MEM * DH_MEM
N_KVA, N_KVB, N_KVM = 2 * HKV_A, 2 * H_B, 2 * H_MEM

LANES = 128
VMEM_LIMIT_BYTES = 56 * 1024 * 1024

W_QA, W_KVA, W_QB, W_KVB, W_QI, W_QM = 1024, 512, 512, 1024, 1024, 512
C_QA, C_KVA, C_QB, C_KVB, C_QI, C_QM, C_TAIL = 0, 8, 12, 16, 24, 32, 36
N_IN_PAD = 37 * LANES

LOG2_E = math.log2(math.e)
NEG_INF = float("-inf")
NEG_BIG = -1e30
INT_MIN = -(2 ** 31)
NEG_INF_KEY = -2139095041


def _cparams(sem):
    return pltpu.CompilerParams(dimension_semantics=sem, vmem_limit_bytes=VMEM_LIMIT_BYTES)


def _mm(a, b):
    return jnp.dot(a, b, preferred_element_type=F32)


def _mm_nt(a, b):
    return lax.dot_general(a, b, (((1,), (1,)), ((), ())), preferred_element_type=F32)


def _layer_norm(y, g, b):
    mu = jnp.mean(y, axis=-1, keepdims=True)
    d = y - mu
    var = jnp.mean(d * d, axis=-1, keepdims=True)
    return d * lax.rsqrt(var + LN_EPS) * g + b


def _exp_and_sum(s, axis=-1):
    m = jnp.max(s, axis=axis, keepdims=True)
    e = jnp.exp(s - m)
    return e, jnp.sum(e, axis=axis, keepdims=True)


def _ordered_key(x):
    k = pltpu.bitcast(x, I32)
    return jnp.where(k < 0, k ^ jnp.int32(0x7FFFFFFF), k)


def _half_lanes(x, c):
    lane = lax.broadcasted_iota(I32, x.shape, 1)
    return jnp.where((lane >= c * DQK_B) & (lane < (c + 1) * DQK_B), x, 0.0)


def _lambda_value(lq1, lk1, lq2, lk2, lam_init):
    a = jnp.sum(lq1[...] * lk1[...], axis=-1, keepdims=True)
    b = jnp.sum(lq2[...] * lk2[...], axis=-1, keepdims=True)
    return jnp.exp(a) - jnp.exp(b) + lam_init


def _diff_finish(o, g_ref, lam_init):
    o = o * lax.rsqrt(jnp.mean(o * o, axis=-1, keepdims=True) + RMS_EPS)
    return o * g_ref[...] * (1.0 - lam_init)


def _rope_tables(pos, head_dim):
    rot = head_dim // ROPE_DIV
    half = rot // 2
    t = pos.shape[0]
    inv_freq = jnp.float32(ROPE_THETA) ** (-jnp.arange(half, dtype=F32) / half)
    ang = pos.astype(F32)[:, None] * inv_freq[None, :]
    cos, sin = jnp.cos(ang), jnp.sin(ang)
    c = jnp.concatenate([cos, cos, jnp.ones((t, head_dim - rot), F32)], axis=-1)
    s_hi = jnp.concatenate([-sin, jnp.zeros((t, head_dim - half), F32)], axis=-1)
    s_lo = jnp.concatenate([jnp.zeros((t, half), F32), sin, jnp.zeros((t, head_dim - rot), F32)], axis=-1)
    rep = LANES // head_dim
    return tuple(jnp.tile(a, (1, rep)) for a in (c, s_hi, s_lo))


def _proj_kernel(x_ref, w_ref, ca, sa1, sa2, cb, sb1, sb2, *rest, n_prev, attn_layout):
    qa_o, kva_o, qb_o, kvb_o, qi_o, qm_o, ki_o, wi_o = rest[n_prev:n_prev + 8]
    ka_t, vat_t, kb_t, vbt_t, ki_t = rest[n_prev + 8:] if attn_layout else (None,) * 5
    xb = x_ref[...].astype(MXU_DTYPE)
    tm = x_ref.shape[0]

    def rope(z, c, s_hi, s_lo, half):
        return z * c[...] + pltpu.roll(z, LANES - half, 1) * s_hi[...] + pltpu.roll(z, half, 1) * s_lo[...]

    def rope_a(z):
        return rope(z, ca, sa1, sa2, DH_A // ROPE_DIV // 2)

    def rope_b(z):
        return rope(z, cb, sb1, sb2, DQK_B // ROPE_DIV // 2)

    def emit(out_ref, c0, kinds, interleave=False, key_t=None, value_t=None):
        n = len(kinds)
        z = _mm_nt(xb, w_ref[c0 * LANES:(c0 + n) * LANES, :])
        for j, kind in enumerate(kinds):
            zj = z[:, j * LANES:(j + 1) * LANES]
            if kind == "a":
                zj = rope_a(zj)
            elif kind == "b":
                zj = rope_b(zj)
            if interleave:
                out_ref[pl.ds(j, tm, stride=n), :] = zj
                if key_t is not None:
                    if j < n // 2:
                        key_t[j] = zj.astype(MXU_DTYPE)
                    else:
                        value_t[j - n // 2] = zj.T.astype(MXU_DTYPE)
            else:
                out_ref[:, j * LANES:(j + 1) * LANES] = zj

    emit(qa_o, C_QA, "a" * 8)
    emit(kva_o, C_KVA, "aa--", interleave=True, key_t=ka_t, value_t=vat_t)
    emit(qb_o, C_QB, "bbbb")
    emit(kvb_o, C_KVB, "bbbb----", interleave=True, key_t=kb_t, value_t=vbt_t)
    emit(qi_o, C_QI, "b" * 8)
    emit(qm_o, C_QM, "----")
    z = _mm_nt(xb, w_ref[C_TAIL * LANES:(C_TAIL + 1) * LANES, :])
    lane = lax.broadcasted_iota(I32, z.shape, 1)
    zt = jnp.where(lane < D_IDX, rope_b(z), z)
    ki_o[...] = zt[:, :D_IDX]
    wi_o[...] = zt[:, D_IDX:D_IDX + H_IDX]
    if ki_t is not None:
        ki_t[...] = zt[:, :D_IDX].astype(MXU_DTYPE)


CACHE_OUTPUTS = (1, 3, 6)


def _project(x2d, w_in_p, layer, tabs, n_tab_blocks, tm, cache_slots=1, prev_caches=None, attn_batch=None):
    m, d = x2d.shape
    n_blocks = m // tm
    shapes = ((1, W_QA), (N_KVA, LANES), (1, W_QB), (N_KVB, LANES), (1, W_QI), (1, W_QM), (1, D_IDX), (1, H_IDX))
    tab_spec = pl.BlockSpec((tm, LANES), lambda i: (i % n_tab_blocks, 0))
    slot = layer if cache_slots > 1 else 0
    out_specs, out_shape = [], []
    for k, (r, w) in enumerate(shapes):
        slots, base = (cache_slots, slot * n_blocks) if k in CACHE_OUTPUTS else (1, 0)
        out_specs.append(pl.BlockSpec((tm * r, w), lambda i, base=base: (base + i, 0)))
        out_shape.append(jax.ShapeDtypeStruct((slots * m * r, w), F32))
    if attn_batch is not None:
        seq = m // attn_batch
        per_seq = seq // tm
        for heads in (HKV_A, H_B):
            out_specs.append(pl.BlockSpec((None, heads, tm, LANES), lambda i: (i // per_seq, 0, i % per_seq, 0)))
            out_shape.append(jax.ShapeDtypeStruct((attn_batch, heads, seq, LANES), MXU_DTYPE))
            out_specs.append(pl.BlockSpec((None, heads, LANES, tm), lambda i: (i // per_seq, 0, 0, i % per_seq)))
            out_shape.append(jax.ShapeDtypeStruct((attn_batch, heads, LANES, seq), MXU_DTYPE))
        out_specs.append(pl.BlockSpec((None, tm, D_IDX), lambda i: (i // per_seq, i % per_seq, 0)))
        out_shape.append(jax.ShapeDtypeStruct((attn_batch, seq, D_IDX), MXU_DTYPE))
    in_specs = ([pl.BlockSpec((tm, d), lambda i: (i, 0)),
                 pl.BlockSpec((None, N_IN_PAD, d), lambda i: (layer, 0, 0), pipeline_mode=pl.Buffered(1))]
                + [tab_spec] * 6)
    args = [x2d, w_in_p, *tabs]
    aliases = {}
    n_prev = 0
    if prev_caches is not None:
        n_prev = len(CACHE_OUTPUTS)
        aliases = {len(args) + n: k for n, k in enumerate(CACHE_OUTPUTS)}
        in_specs += [pl.BlockSpec(memory_space=pl.ANY)] * n_prev
        args += list(prev_caches)
    return pl.pallas_call(
        functools.partial(_proj_kernel, n_prev=n_prev, attn_layout=attn_batch is not None),
        grid=(n_blocks,),
        in_specs=in_specs,
        out_specs=out_specs,
        out_shape=out_shape,
        input_output_aliases=aliases,
        compiler_params=_cparams(("parallel",)),
        name="proj_rope",
    )(*args)


def _matmul_kernel(x_ref, w_ref, *rest):
    o_ref = rest[-1]
    tm = x_ref.shape[0]
    n = w_ref.shape[1] // LANES
    z = _mm(x_ref[...].astype(MXU_DTYPE), w_ref[...])
    for j in range(n):
        o_ref[pl.ds(j, tm, stride=n), :] = z[:, j * LANES:(j + 1) * LANES]


def _matmul(x2d, w, layer, tm, prev=None):
    m, d = x2d.shape
    depth, _, n = w.shape
    n_blocks = m // tm
    in_specs = [pl.BlockSpec((tm, d), lambda i: (i, 0)),
                pl.BlockSpec((None, d, n), lambda i: (layer, 0, 0), pipeline_mode=pl.Buffered(1))]
    args = [x2d, w]
    aliases = {}
    if prev is not None:
        in_specs.append(pl.BlockSpec(memory_space=pl.ANY))
        aliases = {len(args): 0}
        args.append(prev)
    return pl.pallas_call(
        _matmul_kernel,
        grid=(n_blocks,),
        in_specs=in_specs,
        out_specs=pl.BlockSpec((tm * n // LANES, LANES), lambda i: (layer * n_blocks + i, 0)),
        out_shape=jax.ShapeDtypeStruct((depth * m * n // LANES, LANES), F32),
        input_output_aliases=aliases,
        compiler_params=_cparams(("parallel",)),
        name="mem_kv_proj",
    )(*args)


def _kth_threshold(key_ref, rows, n_chunks, topk, groups=1):
    ones = jnp.ones((LANES, LANES), MXU_DTYPE)
    gr = rows // groups

    def count_ge(cand, g):
        acc = jnp.zeros((gr, LANES), F32)
        for c in range(n_chunks):
            acc = acc + jnp.where(key_ref[g * gr:(g + 1) * gr, c * LANES:(c + 1) * LANES] >= cand, 1.0, 0.0)
        return _mm(acc.astype(MXU_DTYPE), ones)

    def body(it, ts):
        bit = lax.shift_left(jnp.int32(1), jnp.int32(31) - it)
        return tuple(jnp.where(count_ge(t + bit, g) >= topk, t + bit, t) for g, t in enumerate(ts))

    ts = lax.fori_loop(0, 32, body, tuple(jnp.full((gr, LANES), INT_MIN, I32) for _ in range(groups)))
    n_ge = [count_ge(t, g) for g, t in enumerate(ts)]
    return jnp.concatenate(ts, axis=0), jnp.concatenate(n_ge, axis=0)


def _selection_bias(key_ref, bias_ref, t, n_ge, rows, n_chunks, topk):
    floor_t = jnp.maximum(t, jnp.int32(NEG_INF_KEY + 1))
    tie_rows = jnp.where((n_ge > topk) & (t > NEG_INF_KEY), 1.0, 0.0)
    has_tie = jnp.max(tie_rows) > 0.5

    @pl.when(jnp.logical_not(has_tie))
    def _():
        for c in range(n_chunks):
            sl = slice(c * LANES, (c + 1) * LANES)
            bias_ref[:, sl] = jnp.where(key_ref[:, sl] >= floor_t, 0.0, NEG_INF)

    @pl.when(has_tie)
    def _():
        ones = jnp.ones((LANES, LANES), MXU_DTYPE)
        r_i = lax.broadcasted_iota(I32, (LANES, LANES), 0)
        c_i = lax.broadcasted_iota(I32, (LANES, LANES), 1)
        strict_upper = jnp.where(r_i < c_i, 1.0, 0.0).astype(MXU_DTYPE)
        n_gt = jnp.zeros((rows, LANES), F32)
        for c in range(n_chunks):
            sl = slice(c * LANES, (c + 1) * LANES)
            n_gt = n_gt + _mm(jnp.where(key_ref[:, sl] > t, 1.0, 0.0).astype(MXU_DTYPE), ones)
        need = topk - n_gt
        run = jnp.zeros((rows, LANES), F32)
        for c in range(n_chunks):
            sl = slice(c * LANES, (c + 1) * LANES)
            k = key_ref[:, sl]
            eq = jnp.where(k == t, 1.0, 0.0).astype(MXU_DTYPE)
            before = run + _mm(eq, strict_upper)
            run = run + _mm(eq, ones)
            keep = (k > t) | ((k == t) & (before < need))
            bias_ref[:, sl] = jnp.where(keep & (k > NEG_INF_KEY), 0.0, NEG_INF)


def _count_rows(pred_fn, key_ref, n_keys):
    acc = jnp.zeros((LANES, LANES), F32)
    for c in range(n_keys // LANES):
        acc = acc + jnp.where(pred_fn(key_ref[c * LANES:(c + 1) * LANES, :]), 1.0, 0.0)
    return jnp.sum(acc, axis=0, keepdims=True)


def _kth_threshold_t(key_ref, n_keys, topk):
    def body(it, t):
        cand = t + lax.shift_left(jnp.int32(1), jnp.int32(31) - it)
        return jnp.where(_count_rows(lambda k: k >= cand, key_ref, n_keys) >= topk, cand, t)

    t = lax.fori_loop(0, 32, body, jnp.full((1, LANES), INT_MIN, I32))
    return t, _count_rows(lambda k: k >= t, key_ref, n_keys)


def _selection_bias_t(key_ref, store_bias, t, n_ge, n_keys, topk):
    floor_t = jnp.maximum(t, jnp.int32(NEG_INF_KEY + 1))
    tie_lanes = jnp.where((n_ge > topk) & (t > NEG_INF_KEY), 1.0, 0.0)
    has_tie = jnp.max(tie_lanes) > 0.5

    @pl.when(jnp.logical_not(has_tie))
    def _():
        for c in range(n_keys // LANES):
            sl = slice(c * LANES, (c + 1) * LANES)
            store_bias(sl, jnp.where(key_ref[sl, :] >= floor_t, 0.0, NEG_BIG))

    @pl.when(has_tie)
    def _():
        r_i = lax.broadcasted_iota(I32, (LANES, LANES), 0)
        c_i = lax.broadcasted_iota(I32, (LANES, LANES), 1)
        strict_lower = jnp.where(c_i < r_i, 1.0, 0.0).astype(MXU_DTYPE)
        need = topk - _count_rows(lambda k: k > t, key_ref, n_keys)
        run = jnp.zeros((1, LANES), F32)
        for c in range(n_keys // LANES):
            sl = slice(c * LANES, (c + 1) * LANES)
            k = key_ref[sl, :]
            eq = jnp.where(k == t, 1.0, 0.0)
            before = run + _mm(strict_lower, eq.astype(MXU_DTYPE))
            run = run + jnp.sum(eq, axis=0, keepdims=True)
            keep = (k > t) | ((k == t) & (before < need))
            store_bias(sl, jnp.where(keep & (k > NEG_INF_KEY), 0.0, NEG_BIG))


Q_BLOCK = 128
N_KV_CLASSES = 16


def _prompt_attn_kernel(*refs, topk, lam_init, q0):
    (qa_ref, ka_ref, vat_ref, qi_ref, wit_ref, ki_ref, qb_ref, kb_ref, vbt_ref, qm_ref, mem_ref,
     lq1, lk1, lq2, lk2, g_ref, _, o_ref, kab_s, kbb_s, km_s, vmt_s, key_ref) = refs
    n_keys = ki_ref.shape[0]
    n_mem = km_s.shape[1]
    qi = pl.program_id(1)
    key_lanes = slice(0, LANES)
    mask_lanes = slice(LANES, 2 * LANES)

    @pl.when(qi == 0)
    def _():
        for g in range(HKV_A):
            kab_s[g, :, key_lanes] = ka_ref[g]
        for h in range(H_B):
            kbb_s[h, :, key_lanes] = kb_ref[h]
        for h in range(H_MEM):
            km_s[h] = mem_ref[pl.ds(h, n_mem, stride=N_KVM), :].astype(MXU_DTYPE)
            vmt_s[h] = mem_ref[pl.ds(H_MEM + h, n_mem, stride=N_KVM), :].T.astype(MXU_DTYPE)

    q_pos = (q0 + qi) * Q_BLOCK + lax.broadcasted_iota(I32, (n_keys, Q_BLOCK), 1)
    k_pos = lax.broadcasted_iota(I32, (n_keys, Q_BLOCK), 0)
    causal = k_pos <= q_pos

    causal_mask = jnp.where(causal, 0.0, NEG_BIG).astype(MXU_DTYPE)

    eye = jnp.where(lax.broadcasted_iota(I32, (Q_BLOCK, Q_BLOCK), 0) == lax.broadcasted_iota(I32, (Q_BLOCK, Q_BLOCK), 1),
                    1.0, 0.0)

    def with_mask_selector(q_blocks, log2_scale):
        q = jnp.concatenate([qb_ * log2_scale for qb_ in q_blocks], axis=0)
        return jnp.concatenate([q, jnp.concatenate([eye] * len(q_blocks), axis=0)], axis=1).astype(MXU_DTYPE)

    def softmax_cols(s, n_cols):
        es, ls = [], []
        for r in range(n_cols):
            x = s[:, r * Q_BLOCK:(r + 1) * Q_BLOCK]
            e = jnp.exp2(x - jnp.max(x, axis=0, keepdims=True))
            es.append(e.astype(MXU_DTYPE))
            ls.append(jnp.sum(e, axis=0, keepdims=True))
        return jnp.concatenate(es, axis=1), ls


    lam = _lambda_value(lq1, lk1, lq2, lk2, lam_init)
    off_b = H_A * DH_A
    for h in range(H_B):
        kbb_s[h, :, mask_lanes] = causal_mask
        qh = qb_ref[:, h * DV_B:(h + 1) * DV_B]
        q2 = with_mask_selector([_half_lanes(qh, 0), _half_lanes(qh, 1)], DQK_B ** -0.5 * LOG2_E)
        e, ls = softmax_cols(_mm_nt(kbb_s[h], q2), 2)
        ot = _mm(vbt_ref[h], e)
        o = (ot[:, :Q_BLOCK] / ls[0] - lam * (ot[:, Q_BLOCK:] / ls[1])).T
        o_ref[:, off_b + h * DV_B:off_b + (h + 1) * DV_B] = _diff_finish(o, g_ref, lam_init).astype(o_ref.dtype)

    off_m = off_b + H_B * DV_B
    for h in range(H_MEM):
        q = (qm_ref[:, h * DH_MEM:(h + 1) * DH_MEM] * (DH_MEM ** -0.5 * LOG2_E)).astype(MXU_DTYPE)
        e, ls = softmax_cols(_mm_nt(km_s[h], q), 1)
        o_ref[:, off_m + h * DH_MEM:off_m + (h + 1) * DH_MEM] = (_mm(vmt_s[h], e) / ls[0]).T.astype(o_ref.dtype)

    def store_selection(rows, block):
        for g in range(HKV_A):
            kab_s[g, rows, mask_lanes] = block.astype(MXU_DTYPE)

    if n_keys <= topk:
        store_selection(slice(0, n_keys), causal_mask)
    else:
        ki = ki_ref[...]
        sc = jnp.zeros((n_keys, Q_BLOCK), F32)
        for hp in range(H_IDX // 2):
            q2 = jnp.concatenate([qi_ref[:, (2 * hp + j) * D_IDX:(2 * hp + j + 1) * D_IDX] for j in range(2)],
                                 axis=0).astype(MXU_DTYPE)
            s = _mm_nt(ki, q2)
            pair = (wit_ref[2 * hp:2 * hp + 1, :] * jnp.maximum(s[:, :Q_BLOCK], 0.0)
                    + wit_ref[2 * hp + 1:2 * hp + 2, :] * jnp.maximum(s[:, Q_BLOCK:], 0.0))
            sc = sc + pair
        key_ref[...] = _ordered_key(jnp.where(causal, sc, NEG_INF))
        t, n_ge = _kth_threshold_t(key_ref, n_keys, topk)
        _selection_bias_t(key_ref, store_selection, t, n_ge, n_keys, topk)

    rep = H_A // HKV_A
    for g in range(HKV_A):
        q4 = with_mask_selector([qa_ref[:, (g * rep + r) * DH_A:(g * rep + r + 1) * DH_A] for r in range(rep)],
                                DH_A ** -0.5 * LOG2_E)
        e, ls = softmax_cols(_mm_nt(kab_s[g], q4), rep)
        ot = _mm(vat_ref[g], e)
        for r in range(rep):
            h = g * rep + r
            o_ref[:, h * DH_A:(h + 1) * DH_A] = (ot[:, r * Q_BLOCK:(r + 1) * Q_BLOCK] / ls[r]).T.astype(o_ref.dtype)


def _prompt_attention(qa, ka, vat, qi, wit, ki, qb, kb, vbt, qm, mem_kv, lam_params, g, batch, seq, layer, lam_init):
    depth = lam_params[0].shape[0]
    n_q = seq // Q_BLOCK
    n_classes = min(N_KV_CLASSES, n_q)
    q_per_class = n_q // n_classes
    n_mem = mem_kv.shape[0] // (depth * batch * N_KVM)
    topk = min(INDEX_TOPK_MAX, seq // 4)
    mem3 = mem_kv.reshape(depth * batch, n_mem * N_KVM, LANES)
    vec_spec = pl.BlockSpec((None, 1, DQK_B), lambda b, i: (layer, 0, 0))

    mixed = jnp.zeros((batch * seq, D_MIX), MXU_DTYPE)
    for c in range(n_classes):
        q0 = c * q_per_class
        n_keys = (c + 1) * q_per_class * Q_BLOCK

        def q_spec(w, q0=q0):
            return pl.BlockSpec((Q_BLOCK, w), lambda b, i: (b * n_q + q0 + i, 0))

        def k_spec(heads):
            return pl.BlockSpec((None, heads, n_keys, LANES), lambda b, i: (b, 0, 0, 0))

        def vt_spec(heads):
            return pl.BlockSpec((None, heads, LANES, n_keys), lambda b, i: (b, 0, 0, 0))

        in_specs = [q_spec(W_QA), k_spec(HKV_A), vt_spec(HKV_A), q_spec(W_QI),
                    pl.BlockSpec((H_IDX, Q_BLOCK), lambda b, i, q0=q0: (0, b * n_q + q0 + i)),
                    pl.BlockSpec((None, n_keys, D_IDX), lambda b, i: (b, 0, 0)),
                    q_spec(W_QB), k_spec(H_B), vt_spec(H_B), q_spec(W_QM),
                    pl.BlockSpec((None, n_mem * N_KVM, LANES), lambda b, i: (layer * batch + b, 0, 0)),
                    vec_spec, vec_spec, vec_spec, vec_spec,
                    pl.BlockSpec((None, 1, DV_B), lambda b, i: (layer, 0, 0)),
                    pl.BlockSpec(memory_space=pl.ANY)]
        args = [qa, ka, vat, qi, wit, ki, qb, kb, vbt, qm, mem3, *lam_params, g, mixed]
        mixed = pl.pallas_call(
            functools.partial(_prompt_attn_kernel, topk=topk, lam_init=lam_init, q0=q0),
            grid=(batch, q_per_class),
            in_specs=in_specs,
            out_specs=q_spec(D_MIX),
            out_shape=jax.ShapeDtypeStruct((batch * seq, D_MIX), MXU_DTYPE),
            scratch_shapes=[pltpu.VMEM((HKV_A, n_keys, 2 * LANES), MXU_DTYPE),
                            pltpu.VMEM((H_B, n_keys, 2 * LANES), MXU_DTYPE),
                            pltpu.VMEM((H_MEM, n_mem, DH_MEM), MXU_DTYPE),
                            pltpu.VMEM((H_MEM, DH_MEM, n_mem), MXU_DTYPE),
                            pltpu.VMEM((n_keys, Q_BLOCK), I32)],
            input_output_aliases={len(args) - 1: 0},
            compiler_params=_cparams(("parallel", "arbitrary")),
            name=f"prompt_attention_kv{c}",
        )(*args)
    return mixed


def _pad_rows(x, rows):
    return jnp.concatenate([x, jnp.zeros((rows - x.shape[0], x.shape[1]), x.dtype)], axis=0)


def _sample_index_kernel(pt_ref, qi_ref, wcol_ref, kin_ref, *rest, n_pages, group):
    del pt_ref
    idx_pages = rest[:group * n_pages]
    key_o = rest[group * n_pages]
    t_new = qi_ref.shape[0] // group
    past = n_pages * PAGE_SIZE

    lane = lax.broadcasted_iota(I32, (t_new, LANES), 1)
    tok = lax.broadcasted_iota(I32, (t_new, LANES), 0)
    new_ok = lane <= tok

    for b in range(group):
        rows = slice(b * t_new, (b + 1) * t_new)
        q_ht = jnp.concatenate([qi_ref[rows, h * D_IDX:(h + 1) * D_IDX] for h in range(H_IDX)],
                               axis=0).astype(MXU_DTYPE)
        wcol = wcol_ref[b * H_IDX * t_new:(b + 1) * H_IDX * t_new, :]

        def index_scores(qk):
            s = jnp.maximum(qk, 0.0) * wcol
            acc = s[0:t_new]
            for h in range(1, H_IDX):
                acc = acc + s[h * t_new:(h + 1) * t_new]
            return acc

        for j in range(n_pages):
            qk = _mm(q_ht, idx_pages[b * n_pages + j][...].astype(MXU_DTYPE))
            key_o[rows, j * LANES:(j + 1) * LANES] = _ordered_key(index_scores(qk))
        s_new = index_scores(_mm_nt(q_ht, _pad_rows(kin_ref[rows, :], LANES).astype(MXU_DTYPE)))
        key_o[rows, past:past + LANES] = _ordered_key(jnp.where(new_ok, s_new, NEG_INF))


def _sample_index(page_table, qi, wcol, ki_new, pool_idx, batch, t_new, layer, n_phys, group):
    n_pages = page_table.shape[1]
    width = (n_pages + 1) * LANES

    def row_spec(rows, w):
        return pl.BlockSpec((rows, w), lambda s, pt: (s, 0))

    def page_spec(b, j):
        return pl.BlockSpec((None, D_IDX, PAGE_SIZE),
                            lambda s, pt: (layer * n_phys + pt[s * group + b, j], 0, 0))

    in_specs = ([row_spec(group * t_new, W_QI), row_spec(group * H_IDX * t_new, 1), row_spec(group * t_new, D_IDX)]
                + [page_spec(b, j) for b in range(group) for j in range(n_pages)])
    grid_spec = pltpu.PrefetchScalarGridSpec(
        num_scalar_prefetch=1, grid=(batch // group,), in_specs=in_specs,
        out_specs=pl.BlockSpec((group * t_new, width), lambda s, pt: (s, 0)))
    return pl.pallas_call(
        functools.partial(_sample_index_kernel, n_pages=n_pages, group=group),
        grid_spec=grid_spec,
        out_shape=jax.ShapeDtypeStruct((batch * t_new, width), I32),
        compiler_params=_cparams(("parallel",)),
        name="sample_indexer",
    )(page_table, qi, wcol, ki_new, *([pool_idx] * (group * n_pages)))


def _sample_select_kernel(key_ref, bias_ref, *, topk):
    rows, width = key_ref.shape
    n_chunks = width // LANES
    t, n_ge = _kth_threshold(key_ref, rows, n_chunks, topk, groups=4 if rows % 64 == 0 else 1)
    _selection_bias(key_ref, bias_ref, t, n_ge, rows, n_chunks, topk)


def _sample_select(keys, topk, tr):
    m, width = keys.shape
    spec = pl.BlockSpec((tr, width), lambda i: (i, 0))
    return pl.pallas_call(
        functools.partial(_sample_select_kernel, topk=topk),
        grid=(m // tr,),
        in_specs=[spec],
        out_specs=spec,
        out_shape=jax.ShapeDtypeStruct((m, width), F32),
        compiler_params=_cparams(("parallel",)),
        name="sample_topk_select",
    )(keys)


def _sample_attn_kernel(pt_ref, qa_ref, kvan_ref, bias_ref, qb_ref, kvbn_ref, qm_ref, mem_ref,
                        lq1, lk1, lq2, lk2, g_ref, *rest, n_pages, lam_init):
    del pt_ref
    kva_pages = rest[:n_pages]
    kvb_pages = rest[n_pages:2 * n_pages]
    o_ref, sa_ref, sb_ref = rest[2 * n_pages:]
    t_new = qa_ref.shape[0]
    past = n_pages * PAGE_SIZE

    lane = lax.broadcasted_iota(I32, (t_new, LANES), 1)
    tok = lax.broadcasted_iota(I32, (t_new, LANES), 0)
    new_ok = lane <= tok

    def chunk_rows(ref, j, rows, n):
        return ref[pl.ds(j, rows, stride=n), :].astype(MXU_DTYPE)

    def new_rows(ref, j, n):
        return _pad_rows(ref[pl.ds(j, t_new, stride=n), :], LANES).astype(MXU_DTYPE)

    scale_a = DH_A ** -0.5
    rep = H_A // HKV_A
    bias_a = jnp.concatenate([bias_ref[...]] * rep, axis=0)
    for g in range(HKV_A):
        q = jnp.concatenate([qa_ref[:, (g * rep + r) * DH_A:(g * rep + r + 1) * DH_A] for r in range(rep)],
                            axis=0).astype(MXU_DTYPE)
        for j in range(n_pages):
            sa_ref[:, j * LANES:(j + 1) * LANES] = _mm_nt(q, chunk_rows(kva_pages[j], g, PAGE_SIZE, N_KVA))
        sa_ref[:, past:past + LANES] = _mm_nt(q, new_rows(kvan_ref, g, N_KVA))
        e, l = _exp_and_sum(sa_ref[...] * scale_a + bias_a)
        e = e.astype(MXU_DTYPE)
        acc = _mm(e[:, past:past + LANES], new_rows(kvan_ref, HKV_A + g, N_KVA))
        for j in range(n_pages):
            acc = acc + _mm(e[:, j * LANES:(j + 1) * LANES], chunk_rows(kva_pages[j], HKV_A + g, PAGE_SIZE, N_KVA))
        acc = acc / l
        for r in range(rep):
            h = g * rep + r
            o_ref[:, h * DH_A:(h + 1) * DH_A] = acc[r * t_new:(r + 1) * t_new]

    lam = _lambda_value(lq1, lk1, lq2, lk2, lam_init)
    scale_b = DQK_B ** -0.5
    for h in range(H_B):
        qh = qb_ref[:, h * DV_B:(h + 1) * DV_B]
        q = jnp.concatenate([_half_lanes(qh, 0), _half_lanes(qh, 1)], axis=0).astype(MXU_DTYPE)
        rows = slice(2 * h * t_new, (2 * h + 2) * t_new)
        for j in range(n_pages):
            sb_ref[rows, j * LANES:(j + 1) * LANES] = _mm_nt(q, chunk_rows(kvb_pages[j], h, PAGE_SIZE, N_KVB))
        sb_ref[rows, past:past + LANES] = _mm_nt(q, new_rows(kvbn_ref, h, N_KVB))
    new_bias = jnp.where(new_ok, 0.0, NEG_INF)
    bias_b = jnp.concatenate(
        [jnp.zeros((N_KVB * t_new, past), F32), jnp.concatenate([new_bias] * N_KVB, axis=0)], axis=1)
    e, l = _exp_and_sum(sb_ref[...] * scale_b + bias_b)
    e = e.astype(MXU_DTYPE)
    off_b = H_A * DH_A
    for h in range(H_B):
        rows = slice(2 * h * t_new, (2 * h + 2) * t_new)
        acc = _mm(e[rows, past:past + LANES], new_rows(kvbn_ref, H_B + h, N_KVB))
        for j in range(n_pages):
            acc = acc + _mm(e[rows, j * LANES:(j + 1) * LANES], chunk_rows(kvb_pages[j], H_B + h, PAGE_SIZE, N_KVB))
        acc = acc / l[rows]
        o = acc[0:t_new] - lam * acc[t_new:2 * t_new]
        o_ref[:, off_b + h * DV_B:off_b + (h + 1) * DV_B] = _diff_finish(o, g_ref, lam_init)

    scale_m = DH_MEM ** -0.5
    off_m = off_b + H_B * DV_B
    n_mem = mem_ref.shape[0] // N_KVM
    for h in range(H_MEM):
        q = qm_ref[:, h * DH_MEM:(h + 1) * DH_MEM].astype(MXU_DTYPE)
        e, l = _exp_and_sum(_mm_nt(q, chunk_rows(mem_ref, h, n_mem, N_KVM)) * scale_m)
        o_ref[:, off_m + h * DH_MEM:off_m + (h + 1) * DH_MEM] = (
            _mm(e.astype(MXU_DTYPE), chunk_rows(mem_ref, H_MEM + h, n_mem, N_KVM)) / l)


def _sample_attention(page_table, qa, kva_new, bias, qb, kvb_new, qm, pool_kva, pool_kvb, pool_mem,
                      lam_params, g, batch, t_new, layer, lam_init, n_phys, mem_rows):
    n_pages = page_table.shape[1]
    width = (n_pages + 1) * LANES

    def row_spec(rows, w):
        return pl.BlockSpec((rows, w), lambda b, pt: (b, 0))

    def page_spec(j, n):
        return pl.BlockSpec((PAGE_SIZE * n, LANES), lambda b, pt: (layer * n_phys + pt[b, j], 0))

    vec_spec = pl.BlockSpec((None, 1, DQK_B), lambda b, pt: (layer, 0, 0))
    in_specs = ([row_spec(t_new, W_QA), row_spec(t_new * N_KVA, LANES), row_spec(t_new, width),
                 row_spec(t_new, W_QB), row_spec(t_new * N_KVB, LANES), row_spec(t_new, W_QM),
                 pl.BlockSpec((mem_rows, LANES), lambda b, pt: (layer * batch + b, 0)),
                 vec_spec, vec_spec, vec_spec, vec_spec,
                 pl.BlockSpec((None, 1, DV_B), lambda b, pt: (layer, 0, 0))]
                + [page_spec(j, N_KVA) for j in range(n_pages)]
                + [page_spec(j, N_KVB) for j in range(n_pages)])
    grid_spec = pltpu.PrefetchScalarGridSpec(
        num_scalar_prefetch=1,
        grid=(batch,),
        in_specs=in_specs,
        out_specs=pl.BlockSpec((t_new, D_MIX), lambda b, pt: (b, 0)),
        scratch_shapes=[pltpu.VMEM((H_A // HKV_A * t_new, width), F32),
                        pltpu.VMEM((N_KVB * t_new, width), F32)],
    )
    return pl.pallas_call(
        functools.partial(_sample_attn_kernel, n_pages=n_pages, lam_init=lam_init),
        grid_spec=grid_spec,
        out_shape=jax.ShapeDtypeStruct((batch * t_new, D_MIX), F32),
        compiler_params=_cparams(("parallel",)),
        name="sample_attention",
    )(page_table, qa, kva_new, bias, qb, kvb_new, qm, pool_mem, *lam_params, g,
      *([pool_kva] * n_pages), *([pool_kvb] * n_pages))


def _attn_out_kernel(x_ref, mix_ref, wo_ref, g_ref, b_ref, h_ref, *, alpha):
    half = x_ref.shape[0] // 2
    for rows in (slice(0, half), slice(half, 2 * half)):
        a = _mm(mix_ref[rows, :].astype(MXU_DTYPE), wo_ref[...])
        h_ref[rows, :] = _layer_norm(alpha * x_ref[rows, :] + a, g_ref[...], b_ref[...])


def _attn_out(x2d, mixed, w_o, ln_g, ln_b, layer, alpha, tm):
    m, d = x2d.shape
    row = pl.BlockSpec((tm, d), lambda i: (i, 0))
    vec = pl.BlockSpec((None, 1, d), lambda i: (layer, 0, 0))
    return pl.pallas_call(
        functools.partial(_attn_out_kernel, alpha=alpha),
        grid=(m // tm,),
        in_specs=[row, pl.BlockSpec((tm, mixed.shape[1]), lambda i: (i, 0)),
                  pl.BlockSpec((None, mixed.shape[1], d), lambda i: (layer, 0, 0), pipeline_mode=pl.Buffered(1)),
                  vec, vec],
        out_specs=row,
        out_shape=jax.ShapeDtypeStruct((m, d), F32),
        compiler_params=_cparams(("parallel",)),
        name="attn_out_ln",
    )(x2d, mixed, w_o, ln_g, ln_b)


def _ffn_kernel(h_ref, wg_ref, wu_ref, wd_ref, g_ref, b_ref, o_ref, hb_ref, acc_ref, *, alpha):
    j = pl.program_id(1)

    @pl.when(j == 0)
    def _():
        hb_ref[...] = h_ref[...].astype(MXU_DTYPE)
        acc_ref[...] = jnp.zeros_like(acc_ref)

    hb = hb_ref[...]
    gate = _mm(hb, wg_ref[...])
    up = _mm(hb, wu_ref[...])
    act = gate * jax.nn.sigmoid(gate) * up
    acc_ref[...] += _mm(act.astype(MXU_DTYPE), wd_ref[...])

    @pl.when(j == pl.num_programs(1) - 1)
    def _():
        o_ref[...] = _layer_norm(alpha * h_ref[...] + acc_ref[...], g_ref[...], b_ref[...])


def _ffn(h2d, w_gate, w_up, w_down, ln_g, ln_b, layer, alpha, tm, tf):
    m, d = h2d.shape
    f = w_gate.shape[-1]
    row = pl.BlockSpec((tm, d), lambda i, j: (i, 0))
    vec = pl.BlockSpec((None, 1, d), lambda i, j: (layer, 0, 0))
    return pl.pallas_call(
        functools.partial(_ffn_kernel, alpha=alpha),
        grid=(m // tm, f // tf),
        in_specs=[row,
                  pl.BlockSpec((None, d, tf), lambda i, j: (layer, 0, j)),
                  pl.BlockSpec((None, d, tf), lambda i, j: (layer, 0, j)),
                  pl.BlockSpec((None, tf, d), lambda i, j: (layer, j, 0)),
                  vec, vec],
        out_specs=row,
        out_shape=jax.ShapeDtypeStruct((m, d), F32),
        scratch_shapes=[pltpu.VMEM((tm, d), MXU_DTYPE), pltpu.VMEM((tm, d), F32)],
        compiler_params=_cparams(("parallel", "arbitrary")),
        name="swiglu_ln",
    )(h2d, w_gate, w_up, w_down, ln_g, ln_b)


def _row_tile(m, cap):
    t = min(m, cap)
    while m % t:
        t //= 2
    return t


def _ff_tile(f, cap):
    best = LANES
    for t in range(LANES, cap + 1, LANES):
        if f % t == 0:
            best = t
    return best


def _largest_divisor(n, cap):
    return max(d for d in range(1, cap + 1) if n % d == 0)


def kernel(x_prompt, x_sample, mem_prompt, cache_kv_a, cache_idx_k, cache_kv_b, cache_mem_kv, page_table,
           w_in, w_mem_kv, lambda_q1, lambda_k1, lambda_q2, lambda_k2, subln_g, w_o,
           ln1_g, ln1_b, w_gate, w_up, w_down, ln2_g, ln2_b):
    depth, d_model, _ = w_in.shape
    batch, seq, _ = x_prompt.shape
    dec_batch, t_new, _ = x_sample.shape
    n_pages = page_table.shape[1]
    past = n_pages * PAGE_SIZE
    n_mem = mem_prompt.shape[1]
    n_phys = cache_idx_k.shape[1]
    alpha = (2.0 * depth) ** 0.25

    n_main = W_QA + W_KVA + W_QB + W_KVB
    w_t = jnp.swapaxes(w_in, 1, 2)
    w_in_p = jnp.concatenate(
        [w_t[:, :n_main], w_t[:, n_main + W_QM:n_main + W_QM + W_QI], w_t[:, n_main:n_main + W_QM],
         w_t[:, n_main + W_QM + W_QI:],
         jnp.zeros((depth, N_IN_PAD - w_in.shape[2], d_model), w_in.dtype)], axis=1).astype(MXU_DTYPE)
    w_mem_c = w_mem_kv.astype(MXU_DTYPE)
    w_o_c, w_gate_c, w_up_c, w_down_c = (w.astype(MXU_DTYPE) for w in (w_o, w_gate, w_up, w_down))
    vec3 = lambda a: a.astype(F32).reshape(depth, 1, a.shape[-1])
    lam_params = tuple(vec3(a) for a in (lambda_q1, lambda_k1, lambda_q2, lambda_k2))
    g_sub = vec3(subln_g)
    ln1g, ln1b, ln2g, ln2b = (vec3(a) for a in (ln1_g, ln1_b, ln2_g, ln2_b))

    pool_kva = cache_kv_a.reshape(-1, LANES)
    pool_idx = jnp.swapaxes(cache_idx_k, 2, 3).reshape(depth * n_phys, D_IDX, PAGE_SIZE)
    pool_kvb = cache_kv_b.reshape(-1, LANES)
    pool_mem = cache_mem_kv.reshape(-1, LANES)

    tm_p = _row_tile(seq, 512)
    m_s = dec_batch * t_new
    tm_s = _row_tile(m_s, 512)
    pos_p = jnp.arange(seq, dtype=I32)
    pos_s = past + (jnp.arange(tm_s, dtype=I32) % t_new)
    tabs_p = _rope_tables(pos_p, DH_A) + _rope_tables(pos_p, DQK_B)
    tabs_s = _rope_tables(pos_s, DH_A) + _rope_tables(pos_s, DQK_B)

    xp = x_prompt.reshape(batch * seq, d_model)
    xs = x_sample.reshape(m_s, d_model)
    mem2d = mem_prompt.reshape(batch * n_mem, d_model)
    tf = _ff_tile(w_gate.shape[-1], 512)
    topk_s = min(INDEX_TOPK_MAX, (past + t_new) // 4)
    idx_group = _largest_divisor(dec_batch, 4)

    outs = [[] for _ in range(3)]
    m_p = batch * seq
    caches = tuple(jnp.zeros((depth * m_p * r, w), F32) for r, w in ((N_KVA, LANES), (N_KVB, LANES), (1, D_IDX)))
    mem_kv = jnp.zeros((depth * batch * n_mem * N_KVM, LANES), F32)
    for l in range(depth):
        lam_init = 0.8 - 0.6 * math.exp(-0.3 * l)

        qa, kva, qb, kvb, qi, qm, ki, wi, ka_t, vat_t, kb_t, vbt_t, ki_t = _project(
            xp, w_in_p, l, tabs_p, seq // tm_p, tm_p, cache_slots=depth, prev_caches=caches, attn_batch=batch)
        caches = (kva, kvb, ki)
        mem_kv = _matmul(mem2d, w_mem_c, l, _row_tile(batch * n_mem, 512), prev=mem_kv)
        mixed = _prompt_attention(qa, ka_t, vat_t, qi, wi.T, ki_t, qb, kb_t, vbt_t, qm, mem_kv, lam_params, g_sub,
                                  batch, seq, l, lam_init)
        h = _attn_out(xp, mixed, w_o_c, ln1g, ln1b, l, alpha, tm_p)
        xp = _ffn(h, w_gate_c, w_up_c, w_down_c, ln2g, ln2b, l, alpha, tm_p, tf)

        qa, kva, qb, kvb, qi, qm, ki, wi = _project(xs, w_in_p, l, tabs_s, 1, tm_s)
        wcol = wi.reshape(dec_batch, t_new, H_IDX).transpose(0, 2, 1).reshape(dec_batch * H_IDX * t_new, 1)
        keys = _sample_index(page_table, qi, wcol, ki, pool_idx, dec_batch, t_new, l, n_phys, idx_group)
        bias = _sample_select(keys, topk_s, _row_tile(m_s, 256))
        mixed = _sample_attention(page_table, qa, kva, bias, qb, kvb, qm, pool_kva, pool_kvb, pool_mem,
                                  lam_params, g_sub, dec_batch, t_new, l, lam_init, n_phys, n_mem * N_KVM)
        h = _attn_out(xs, mixed, w_o_c, ln1g, ln1b, l, alpha, tm_s)
        xs = _ffn(h, w_gate_c, w_up_c, w_down_c, ln2g, ln2b, l, alpha, tm_s, tf)
        outs[0].append(kva.reshape(dec_batch, t_new, 2, HKV_A, DH_A))
        outs[1].append(ki.reshape(dec_batch, t_new, D_IDX))
        outs[2].append(kvb.reshape(dec_batch, t_new, 2, H_B, DV_B))

    kva_p, kvb_p, ki_p = caches
    return (xp.reshape(batch, seq, d_model), xs.reshape(dec_batch, t_new, d_model),
            kva_p.reshape(depth, batch, seq, 2, HKV_A, DH_A), ki_p.reshape(depth, batch, seq, D_IDX),
            kvb_p.reshape(depth, batch, seq, 2, H_B, DV_B), mem_kv.reshape(depth, batch, n_mem, 2, H_MEM, DH_MEM)
            ) + tuple(jnp.stack(o) for o in outs)
```
